```python
import math
import jax, jax.numpy as jnp
from jax import lax
import numpy as np

D_MODEL = 1024
BATCH = 8
SEQ = 2048
DEPTH = 2

MLA_HEADS = 8
Q_LORA = 384
KV_LORA = 256
QK_NOPE = 64
QK_ROPE = 32
QK_HEAD = QK_NOPE + QK_ROPE
V_HEAD = 64
MLA_WIDTH = MLA_HEADS * V_HEAD
ROPE_BASE = 10000.0
Q_BLOCK = 128
POOL_WINDOWS = (2, 4, 8, 16)
POOL_GROUPS = len(POOL_WINDOWS)
POOL_WIDTH = 512
POOL_GC = POOL_WIDTH // POOL_GROUPS
N_BRANCHES = 2
IN_SPLITS = (Q_LORA, KV_LORA, QK_ROPE, POOL_WIDTH, N_BRANCHES * D_MODEL)
IN_WIDTH = sum(IN_SPLITS)
PEER_HEADS = 8
N_KEYS = 128
N_EXPERTS = N_KEYS * N_KEYS
PEER_KEY_DIM = 256
PEER_HALF = PEER_KEY_DIM // 2
PEER_TOPK = 16
PEER_CHUNK = 128
RMS_EPS = 1e-6

kernel_name = "hybrid_mla_pool_peer_encoder"


def rms_norm(x, g):
    xf = x.astype(jnp.float32)
    y = xf * lax.rsqrt(jnp.mean(xf * xf, axis=-1, keepdims=True) + RMS_EPS)
    return (y * g.astype(jnp.float32)).astype(x.dtype)


def rope_cos_sin(positions):
    inv = ROPE_BASE ** (-jnp.arange(0, QK_ROPE, 2, dtype=jnp.float32) / QK_ROPE)
    ang = positions.astype(jnp.float32)[..., None] * inv
    return jnp.cos(ang), jnp.sin(ang)


def apply_rope(x, cos, sin):
    cos = cos.astype(x.dtype)
    sin = sin.astype(x.dtype)
    x1, x2 = jnp.split(x, 2, axis=-1)
    return jnp.concatenate([x1 * cos - x2 * sin, x1 * sin + x2 * cos], axis=-1)


def block_attention(q, k, v):
    B, S, H, Dh = q.shape
    nb = S // Q_BLOCK
    scale = 1.0 / math.sqrt(Dh)
    qb = q.reshape(B, nb, Q_BLOCK, H, Dh).transpose(1, 0, 2, 3, 4)

    def one(qblk):
        s = jnp.einsum('bqhd,bkhd->bhqk', qblk, k).astype(jnp.float32) * scale
        p = jax.nn.softmax(s, axis=-1).astype(v.dtype)
        return jnp.einsum('bhqk,bkhd->bqhd', p, v)

    o = lax.map(one, qb)
    return o.transpose(1, 0, 2, 3, 4).reshape(B, S, H * V_HEAD)


def mla_branch(c_q, c_kv, k_pe, cos, sin, q_lora_g, kv_lora_g, w_uq, w_ukv, q_head_g, k_head_g, w_o):
    B, S, _ = c_q.shape
    q = (rms_norm(c_q, q_lora_g) @ w_uq).reshape(B, S, MLA_HEADS, QK_HEAD)
    q_nope, q_pe = q[..., :QK_NOPE], q[..., QK_NOPE:]
    q_pe = apply_rope(q_pe, cos[:, :, None, :], sin[:, :, None, :])
    kv = (rms_norm(c_kv, kv_lora_g) @ w_ukv).reshape(B, S, MLA_HEADS, QK_NOPE + V_HEAD)
    k_nope, v = kv[..., :QK_NOPE], kv[..., QK_NOPE:]
    k_pe = apply_rope(k_pe, cos, sin)
    k_pe = jnp.broadcast_to(k_pe[:, :, None, :], (B, S, MLA_HEADS, QK_ROPE))
    q = rms_norm(jnp.concatenate([q_nope, q_pe], axis=-1), q_head_g)
    k = rms_norm(jnp.concatenate([k_nope, k_pe], axis=-1), k_head_g)
    o = block_attention(q, k, v)
    return o @ w_o


def pool_branch(p, pool_w, pool_scale, w_pool_out):
    B, S, _ = p.shape
    pf = p.reshape(B, S, POOL_GROUPS, POOL_GC).astype(jnp.float32)
    cs = jnp.concatenate([jnp.zeros((B, 1, POOL_GROUPS, POOL_GC), jnp.float32),
                          jnp.cumsum(pf, axis=1)], axis=1)
    half = jnp.array(POOL_WINDOWS, dtype=jnp.int32) // 2
    t = jnp.arange(S, dtype=jnp.int32)[:, None]
    lo = jnp.clip(t - half[None, :], 0, S)
    hi = jnp.clip(t + half[None, :], 0, S)
    gidx = jnp.arange(POOL_GROUPS, dtype=jnp.int32)[None, :]
    win_sum = cs[:, hi, gidx] - cs[:, lo, gidx]
    count = (hi - lo).astype(jnp.float32)[None, :, :, None]
    mixed = (win_sum / count - pf).astype(p.dtype)
    y = jnp.einsum('bsgc,gcd->bsgd', mixed, pool_w).reshape(B, S, POOL_WIDTH) * pool_scale
    return y @ w_pool_out


def peer_ffn(xn, wq, sub_keys, expert_u, expert_v):
    B, S, D = xn.shape
    xc = xn.reshape((B * S) // PEER_CHUNK, PEER_CHUNK, D)
    K = PEER_TOPK

    def one(x):
        C = x.shape[0]
        q = (x @ wq).reshape(C, PEER_HEADS, 2, PEER_HALF)
        s = jnp.einsum('chpk,hpnk->chpn', q, sub_keys).astype(jnp.float32)
        top_s, top_i = lax.top_k(s, K)
        cand = top_s[:, :, 0, :, None] + top_s[:, :, 1, None, :]
        best_s, best_j = lax.top_k(cand.reshape(C, PEER_HEADS, K * K), K)
        i1 = jnp.take_along_axis(top_i[:, :, 0], best_j // K, axis=-1)
        i2 = jnp.take_along_axis(top_i[:, :, 1], best_j % K, axis=-1)
        e = i1 * N_KEYS + i2
        g = jax.nn.softmax(best_s, axis=-1).astype(x.dtype)
        u = jnp.take(expert_u, e, axis=0)
        h = jax.nn.gelu(jnp.einsum('cd,chkd->chk', x, u))
        v = jnp.take(expert_v, e, axis=0)
        return jnp.einsum('chk,chkd->cd', g * h, v)

    return lax.map(one, xc).reshape(B, S, D)


def setup_inputs(seed: int = 0) -> dict:
    key = jax.random.key(seed)
    ks = jax.random.split(key, 24)
    f32 = jnp.float32

    def nrm(k, shape, scale):
        return jax.random.normal(k, shape, f32) * scale

    def gain(k, shape):
        return 1.0 + 0.1 * jax.random.normal(k, shape, f32)

    L = DEPTH
    x = jax.random.normal(ks[0], (BATCH, SEQ, D_MODEL), f32)
    offsets = jax.random.randint(ks[1], (BATCH, 1), 0, 4096, dtype=jnp.int32)
    positions = jnp.arange(SEQ, dtype=jnp.int32)[None, :] + offsets
    return {
        "x": x,
        "positions": positions,
        "norm1_g": gain(ks[2], (L, D_MODEL)),
        "w_in": nrm(ks[3], (L, D_MODEL, IN_WIDTH), D_MODEL ** -0.5),
        "q_lora_g": gain(ks[4], (L, Q_LORA)),
        "kv_lora_g": gain(ks[5], (L, KV_LORA)),
        "w_uq": nrm(ks[6], (L, Q_LORA, MLA_HEADS * QK_HEAD), Q_LORA ** -0.5),
        "w_ukv": nrm(ks[7], (L, KV_LORA, MLA_HEADS * (QK_NOPE + V_HEAD)), KV_LORA ** -0.5),
        "q_head_g": gain(ks[8], (L, QK_HEAD)),
        "k_head_g": gain(ks[9], (L, QK_HEAD)),
        "w_o_mla": nrm(ks[10], (L, MLA_WIDTH, D_MODEL), MLA_WIDTH ** -0.5),
        "pool_w": nrm(ks[11], (L, POOL_GROUPS, POOL_GC, POOL_GC), POOL_GC ** -0.5),
        "pool_scale": gain(ks[12], (L, POOL_WIDTH)),
        "w_pool_out": nrm(ks[13], (L, POOL_WIDTH, D_MODEL), POOL_WIDTH ** -0.5),
        "gate_bias": nrm(ks[14], (L, N_BRANCHES, D_MODEL), 0.1),
        "w_out": nrm(ks[15], (L, D_MODEL, D_MODEL), D_MODEL ** -0.5),
        "norm2_g": gain(ks[16], (L, D_MODEL)),
        "peer_wq": nrm(ks[17], (L, D_MODEL, PEER_HEADS * PEER_KEY_DIM), D_MODEL ** -0.5),
        "peer_keys": nrm(ks[18], (L, PEER_HEADS, 2, N_KEYS, PEER_HALF), PEER_HALF ** -0.5),
        "peer_u": nrm(ks[19], (L, N_EXPERTS, D_MODEL), D_MODEL ** -0.5),
        "peer_v": nrm(ks[20], (L, N_EXPERTS, D_MODEL), PEER_HEADS ** -0.5),
    }


def reference(x, positions, norm1_g, w_in, q_lora_g, kv_lora_g, w_uq, w_ukv, q_head_g, k_head_g,
              w_o_mla, pool_w, pool_scale, w_pool_out, gate_bias, w_out, norm2_g,
              peer_wq, peer_keys, peer_u, peer_v):
    B, S, D = x.shape
    cos, sin = rope_cos_sin(positions)
    cuts = list(np.cumsum(IN_SPLITS)[:-1])
    for l in range(DEPTH):
        xn = rms_norm(x, norm1_g[l])
        c_q, c_kv, k_pe, p, g_logits = jnp.split(xn @ w_in[l], cuts, axis=-1)
        a = mla_branch(c_q, c_kv, k_pe, cos, sin, q_lora_g[l], kv_lora_g[l], w_uq[l], w_ukv[l],
                       q_head_g[l], k_head_g[l], w_o_mla[l])
        b = pool_branch(p, pool_w[l], pool_scale[l], w_pool_out[l])
        gates = jax.nn.sigmoid((g_logits.reshape(B, S, N_BRANCHES, D) + gate_bias[l]).astype(jnp.float32)).astype(x.dtype)
        mix = gates[:, :, 0] * a + gates[:, :, 1] * b
        x = x + mix @ w_out[l]
        x = x + peer_ffn(rms_norm(x, norm2_g[l]), peer_wq[l], peer_keys[l], peer_u[l], peer_v[l])
    return x
```

```python
import functools
import math

import jax
import jax.numpy as jnp
import numpy as np
from jax import lax
from jax.experimental import pallas as pl
from jax.experimental.pallas import tpu as pltpu

D_MODEL = 1024
MLA_HEADS = 8
Q_LORA = 384
KV_LORA = 256
QK_NOPE = 64
QK_ROPE = 32
QK_HEAD = QK_NOPE + QK_ROPE
V_HEAD = 64
MLA_WIDTH = MLA_HEADS * V_HEAD
ROPE_BASE = 10000.0
POOL_WINDOWS = (2, 4, 8, 16)
POOL_WIDTH = 512
POOL_GC = 128
PEER_HEADS = 8
N_KEYS = 128
N_EXPERTS = N_KEYS * N_KEYS
PEER_HALF = 128
PEER_TOPK = 16
RMS_EPS = 1e-6

LANES = 128
HEAD_PAD = LANES
IN_PAD = Q_LORA + KV_LORA + LANES + POOL_WIDTH + 2 * D_MODEL
VMEM_LIMIT = 48 * 1024 * 1024

CDT = jnp.bfloat16
F32 = jnp.float32
NEG_INF = float("-inf")

_CAND_ROWS = 16 + 8 * 7 + 8
_FIDX_INVALID = 1 << 20


def _cparams(sem, vmem=VMEM_LIMIT):
    return pltpu.CompilerParams(dimension_semantics=sem, vmem_limit_bytes=vmem)


def _rms(x, g):
    return x * lax.rsqrt(jnp.mean(x * x, axis=-1, keepdims=True) + RMS_EPS) * g


def _dot(a, b):
    return jnp.dot(a, b, preferred_element_type=F32)


def _dot_nt(a, b):
    return lax.dot_general(a, b, (((1,), (1,)), ((), ())), preferred_element_type=F32)


def _rope_kernel(pos_ref, inv_ref, c_ref, s1_ref, s2_ref):
    ang = pos_ref[...] * inv_ref[...]
    lane = lax.broadcasted_iota(jnp.int32, ang.shape, 1)
    cos = jnp.cos(ang)
    sin = jnp.sin(ang)
    half = QK_ROPE // 2
    c_ref[...] = jnp.where(lane < QK_NOPE, 1.0, jnp.where(lane < QK_HEAD, cos, 0.0))
    s1_ref[...] = jnp.where((lane >= QK_NOPE) & (lane < QK_NOPE + half), -sin, 0.0)
    s2_ref[...] = jnp.where((lane >= QK_NOPE + half) & (lane < QK_HEAD), sin, 0.0)


def _rope_tables(positions):
    T = positions.size
    tm = min(T, 2048)
    pos = jnp.broadcast_to(positions.reshape(T, 1).astype(F32), (T, LANES))
    inv = ROPE_BASE ** (-jnp.arange(0, QK_ROPE, 2, dtype=F32) / QK_ROPE)
    inv_pat = jnp.concatenate([jnp.zeros((QK_NOPE,), F32), inv, inv,
                               jnp.zeros((LANES - QK_HEAD,), F32)]).reshape(1, LANES)
    spec = pl.BlockSpec((tm, LANES), lambda i: (i, 0))
    return pl.pallas_call(
        _rope_kernel,
        grid=(T // tm,),
        in_specs=[spec, pl.BlockSpec((1, LANES), lambda i: (0, 0))],
        out_specs=[spec, spec, spec],
        out_shape=[jax.ShapeDtypeStruct((T, LANES), F32)] * 3,
        compiler_params=_cparams(("parallel",)),
        name="rope_tables",
    )(pos, inv_pat)


def _in_proj_kernel(x_ref, g1_ref, win_ref, qlg_ref, kvlg_ref, wuq_ref, wuk_ref, wuv_ref,
                    c_ref, s1_ref, s2_ref, qg_ref, kg_ref, bias_ref,
                    q_out, k_out, v_out, p_out, gate_out):
    xn = _rms(x_ref[...], g1_ref[...])
    y = _dot(xn.astype(CDT), win_ref[...])
    o = 0
    cq = y[:, o:o + Q_LORA]; o += Q_LORA
    ckv = y[:, o:o + KV_LORA]; o += KV_LORA
    kpe = y[:, o:o + LANES]; o += LANES
    p_out[...] = y[:, o:o + POOL_WIDTH]; o += POOL_WIDTH
    gate_out[...] = jax.nn.sigmoid(y[:, o:] + bias_ref[...]).astype(gate_out.dtype)

    cqn = _rms(cq, qlg_ref[...]).astype(CDT)
    ckvn = _rms(ckv, kvlg_ref[...]).astype(CDT)
    q_raw = _dot(cqn, wuq_ref[...])
    k_raw = _dot(ckvn, wuk_ref[...])
    v_out[...] = _dot(ckvn, wuv_ref[...]).astype(v_out.dtype)

    cpat = c_ref[...]
    s1pat = s1_ref[...]
    s2pat = s2_ref[...]
    half = QK_ROPE // 2

    def rope(xh):
        return (xh * cpat + pltpu.roll(xh, LANES - half, 1) * s1pat
                + pltpu.roll(xh, half, 1) * s2pat)

    def head_norm(xh, g):
        ms = jnp.sum(xh * xh, axis=-1, keepdims=True) * (1.0 / QK_HEAD)
        return xh * lax.rsqrt(ms + RMS_EPS) * g

    kpe_r = rope(kpe)
    qg = qg_ref[...]
    kg = kg_ref[...]
    for h in range(MLA_HEADS):
        sl = slice(h * HEAD_PAD, (h + 1) * HEAD_PAD)
        q_out[:, sl] = head_norm(rope(q_raw[:, sl]), qg).astype(q_out.dtype)
        k_out[:, sl] = head_norm(k_raw[:, sl] + kpe_r, kg).astype(k_out.dtype)


def _in_proj(x2d, lw, ropes, tm):
    T = x2d.shape[0]
    tm = min(tm, T)
    row = lambda w: pl.BlockSpec((tm, w), lambda i: (i, 0))
    full = lambda a: pl.BlockSpec(a.shape, lambda i: (0,) * a.ndim)
    consts = [lw["g1"], lw["w_in"], lw["qlg"], lw["kvlg"], lw["w_uq"], lw["w_uk"], lw["w_uv"]]
    tail = [lw["qg"], lw["kg"], lw["gate_bias"]]
    return pl.pallas_call(
        _in_proj_kernel,
        grid=(T // tm,),
        in_specs=[row(D_MODEL)] + [full(a) for a in consts] + [row(LANES)] * 3 + [full(a) for a in tail],
        out_specs=[row(MLA_HEADS * HEAD_PAD), row(MLA_HEADS * HEAD_PAD), row(MLA_WIDTH),
                   row(POOL_WIDTH), row(2 * D_MODEL)],
        out_shape=[jax.ShapeDtypeStruct((T, MLA_HEADS * HEAD_PAD), CDT),
                   jax.ShapeDtypeStruct((T, MLA_HEADS * HEAD_PAD), CDT),
                   jax.ShapeDtypeStruct((T, MLA_WIDTH), CDT),
                   jax.ShapeDtypeStruct((T, POOL_WIDTH), F32),
                   jax.ShapeDtypeStruct((T, 2 * D_MODEL), CDT)],
        compiler_params=_cparams(("parallel",)),
        name="in_proj",
    )(x2d, *consts, *ropes, *tail)


def _attn_kernel(q_ref, k_ref, v_ref, o_ref):
    outs = []
    for j in range(2):
        q = q_ref[:, j * HEAD_PAD:(j + 1) * HEAD_PAD]
        k = k_ref[:, j * HEAD_PAD:(j + 1) * HEAD_PAD]
        v = v_ref[:, j * V_HEAD:(j + 1) * V_HEAD]
        s = _dot_nt(q, k)
        m = jnp.max(s, axis=-1, keepdims=True)
        p = jnp.exp(s - m)
        l = jnp.sum(p, axis=-1, keepdims=True)
        outs.append(_dot(p.astype(CDT), v) / l)
    o_ref[...] = jnp.concatenate(outs, axis=-1).astype(o_ref.dtype)


def _attention(q, k, v, B, S, tq):
    tq = min(tq, S)
    nq = S // tq
    return pl.pallas_call(
        _attn_kernel,
        grid=(B, MLA_HEADS // 2, nq),
        in_specs=[pl.BlockSpec((tq, 2 * HEAD_PAD), lambda b, h, i: (b * nq + i, h)),
                  pl.BlockSpec((S, 2 * HEAD_PAD), lambda b, h, i: (b, h)),
                  pl.BlockSpec((S, 2 * V_HEAD), lambda b, h, i: (b, h))],
        out_specs=pl.BlockSpec((tq, 2 * V_HEAD), lambda b, h, i: (b * nq + i, h)),
        out_shape=jax.ShapeDtypeStruct((B * S, MLA_WIDTH), CDT),
        compiler_params=_cparams(("parallel", "parallel", "parallel")),
        name="attention",
    )(q, k, v)


def _pool_kernel(p_ref, pw_ref, ps_ref, y_ref):
    S = p_ref.shape[0]
    t = lax.broadcasted_iota(jnp.int32, (S, POOL_GC), 0)
    for g, w in enumerate(POOL_WINDOWS):
        half = w // 2
        sl = slice(g * POOL_GC, (g + 1) * POOL_GC)
        pg = p_ref[:, sl]
        acc = pg
        for d in range(-half, half):
            if d == 0:
                continue
            shifted = pltpu.roll(pg, (-d) % S, 0)
            valid = (t + d >= 0) & (t + d < S)
            acc = acc + jnp.where(valid, shifted, 0.0)
        count = (jnp.minimum(t + half, S) - jnp.maximum(t - half, 0)).astype(F32)
        mixed = acc / count - pg
        yg = _dot(mixed.astype(CDT), pw_ref[g]) * ps_ref[:, sl]
        y_ref[:, sl] = yg.astype(y_ref.dtype)


def _pool(p, lw, B, S):
    return pl.pallas_call(
        _pool_kernel,
        grid=(B,),
        in_specs=[pl.BlockSpec((S, POOL_WIDTH), lambda b: (b, 0)),
                  pl.BlockSpec(lw["pool_w"].shape, lambda b: (0, 0, 0)),
                  pl.BlockSpec((1, POOL_WIDTH), lambda b: (0, 0))],
        out_specs=pl.BlockSpec((S, POOL_WIDTH), lambda b: (b, 0)),
        out_shape=jax.ShapeDtypeStruct((B * S, POOL_WIDTH), CDT),
        compiler_params=_cparams(("parallel",)),
        name="pool",
    )(p, lw["pool_w"], lw["pool_scale"])


def _out_proj_kernel(x_ref, o_ref, y_ref, gate_ref, wo_ref, wpo_ref, wout_ref, g2_ref, wq_ref, keys_ref,
                     xnew_out, xnt_out, st_out):
    a = _dot(o_ref[...], wo_ref[...])
    bp = _dot(y_ref[...], wpo_ref[...])
    ga = gate_ref[:, :D_MODEL].astype(F32)
    gb = gate_ref[:, D_MODEL:].astype(F32)
    mix = ga * a + gb * bp
    xnew = x_ref[...] + _dot(mix.astype(CDT), wout_ref[...])
    xnew_out[...] = xnew
    xn = _rms(xnew, g2_ref[...])
    xnt_out[...] = xn.T.astype(xnt_out.dtype)
    qp = _dot(xn.astype(CDT), wq_ref[...])
    for hp in range(2 * PEER_HEADS):
        qh = qp[:, hp * PEER_HALF:(hp + 1) * PEER_HALF].astype(CDT)
        st_out[hp] = _dot_nt(keys_ref[hp], qh)


def _out_proj(x2d, o, y, gates, lw, tm):
    T = x2d.shape[0]
    tm = min(tm, T)
    row = lambda w: pl.BlockSpec((tm, w), lambda i: (i, 0))
    full = lambda a: pl.BlockSpec(a.shape, lambda i: (0,) * a.ndim)
    consts = [lw["w_o_mla"], lw["w_pool_out"], lw["w_out"], lw["g2"], lw["peer_wq"], lw["peer_keys"]]
    return pl.pallas_call(
        _out_proj_kernel,
        grid=(T // tm,),
        in_specs=[row(D_MODEL), row(MLA_WIDTH), row(POOL_WIDTH), row(2 * D_MODEL)] + [full(a) for a in consts],
        out_specs=[row(D_MODEL),
                   pl.BlockSpec((D_MODEL, tm), lambda i: (0, i)),
                   pl.BlockSpec((2 * PEER_HEADS, N_KEYS, tm), lambda i: (0, 0, i))],
        out_shape=[jax.ShapeDtypeStruct((T, D_MODEL), F32),
                   jax.ShapeDtypeStruct((D_MODEL, T), CDT),
                   jax.ShapeDtypeStruct((2 * PEER_HEADS, N_KEYS, T), F32)],
        compiler_params=_cparams(("parallel",)),
        name="out_proj",
    )(x2d, o, y, gates, *consts)


def _top16(s, key_iota, row16):
    work = s
    rank = jnp.full(s.shape, PEER_TOPK, jnp.int32)
    tops = jnp.zeros((PEER_TOPK, s.shape[1]), F32)
    for r in range(PEER_TOPK):
        m = jnp.max(work, axis=0, keepdims=True)
        first = jnp.min(jnp.where(work == m, key_iota, N_KEYS), axis=0, keepdims=True)
        sel = key_iota == first
        rank = jnp.where(sel, r, rank)
        work = jnp.where(sel, NEG_INF, work)
        tops = jnp.where(row16 == r, m, tops)
    return rank, tops


def _cand_grid(ta, tb, combine):
    pieces = [combine(ta[0:1], tb)]
    for r1 in range(1, 8):
        pieces.append(combine(ta[r1:r1 + 1], tb[0:8]))
    pieces.append(combine(ta[8:16], tb[0:1]))
    return jnp.concatenate(pieces, axis=0)


def _peer_topk_kernel(s_ref, fidx_ref, r2_out, e2_out, cnt_out, c_out, *, n_chunks):
    key_iota = lax.broadcasted_iota(jnp.int32, (N_KEYS, LANES), 0)
    row16 = lax.broadcasted_iota(jnp.int32, (PEER_TOPK, LANES), 0)
    fidx = fidx_ref[...]
    cand_valid = fidx < _FIDX_INVALID

    def body(it, carry):
        h = it // n_chunks
        off = pl.multiple_of((it % n_chunks) * LANES, LANES)
        s1 = s_ref[2 * h, :, pl.ds(off, LANES)]
        s2 = s_ref[2 * h + 1, :, pl.ds(off, LANES)]
        rank1, ta = _top16(s1, key_iota, row16)
        rank2, tb = _top16(s2, key_iota, row16)
        a0 = ta[0:1]
        b0 = tb[0:1]
        cand = jnp.where(cand_valid, _cand_grid(ta, tb, lambda a, b: a + b), NEG_INF)
        ecand = _cand_grid(jnp.exp(ta - a0), jnp.exp(tb - b0), lambda a, b: a * b)
        selected = jnp.zeros(cand.shape, F32)
        for _ in range(PEER_TOPK):
            m = jnp.max(cand, axis=0, keepdims=True)
            first = jnp.min(jnp.where(cand == m, fidx, _FIDX_INVALID), axis=0, keepdims=True)
            sel = fidx == first
            selected = jnp.where(sel, 1.0, selected)
            cand = jnp.where(sel, NEG_INF, cand)
        z = jnp.sum(selected * ecand, axis=0, keepdims=True)
        cntr = [jnp.sum(selected[0:16], axis=0, keepdims=True)]
        for r1 in range(1, 8):
            cntr.append(jnp.sum(selected[8 + 8 * r1:16 + 8 * r1], axis=0, keepdims=True))
        for r1 in range(8, 16):
            cntr.append(selected[64 + r1:65 + r1])
        cnt = jnp.zeros(s1.shape, F32)
        for r1 in range(PEER_TOPK):
            cnt = jnp.where(rank1 == r1, cntr[r1], cnt)
        r2_out[h, :, pl.ds(off, LANES)] = rank2.astype(F32)
        e2_out[h, :, pl.ds(off, LANES)] = jnp.exp(s2 - b0)
        cnt_out[h, :, pl.ds(off, LANES)] = cnt
        c_out[h, :, pl.ds(off, LANES)] = jnp.exp(s1 - a0) / z
        return carry

    lax.fori_loop(0, PEER_HEADS * n_chunks, body, 0)


def _cand_fidx():
    rows = []
    rows += [0 * 16 + r2 for r2 in range(16)]
    for r1 in range(1, 8):
        n = PEER_TOPK // (r1 + 1)
        rows += [r1 * 16 + r2 if r2 < n else _FIDX_INVALID + r1 * 16 + r2 for r2 in range(8)]
    rows += [r1 * 16 for r1 in range(8, 16)]
    assert len(rows) == _CAND_ROWS
    return jnp.broadcast_to(jnp.asarray(rows, jnp.int32)[:, None], (_CAND_ROWS, LANES))


def _peer_topk(st, tmk):
    T = st.shape[-1]
    tmk = min(tmk, T)
    spec = pl.BlockSpec((PEER_HEADS, N_KEYS, tmk), lambda i: (0, 0, i))
    shp = jax.ShapeDtypeStruct((PEER_HEADS, N_KEYS, T), F32)
    return pl.pallas_call(
        functools.partial(_peer_topk_kernel, n_chunks=tmk // LANES),
        grid=(T // tmk,),
        in_specs=[pl.BlockSpec((2 * PEER_HEADS, N_KEYS, tmk), lambda i: (0, 0, i)),
                  pl.BlockSpec((_CAND_ROWS, LANES), lambda i: (0, 0))],
        out_specs=[spec] * 4,
        out_shape=[shp] * 4,
        compiler_params=_cparams(("parallel",)),
        name="peer_topk",
    )(st, _cand_fidx())


def _peer_dense_kernel(x_ref, xnt_ref, r2_ref, e2_ref, cnt_ref, c_ref, u_ref, vt_ref, out_ref,
                       acc_ref, ht_ref, at_ref, *, eb, tc):
    e = pl.program_id(1)
    tm = xnt_ref.shape[1]

    @pl.when(e == 0)
    def _():
        acc_ref[...] = jnp.zeros_like(acc_ref)

    ht_ref[...] = _dot(u_ref[...], xnt_ref[...])
    for jb in range(eb // N_KEYS):
        j = e * (eb // N_KEYS) + jb
        rows = slice(jb * N_KEYS, (jb + 1) * N_KEYS)

        def chunk(ci, carry, j=j, rows=rows):
            cols = pl.ds(pl.multiple_of(ci * tc, tc), tc)
            g = jnp.zeros((N_KEYS, tc), F32)
            for h in range(PEER_HEADS):
                cnt = cnt_ref[h, pl.ds(j, 1), cols]
                coef = c_ref[h, pl.ds(j, 1), cols]
                g = g + jnp.where(r2_ref[h, :, cols] < cnt, e2_ref[h, :, cols] * coef, 0.0)
            at_ref[rows, cols] = (g * jax.nn.gelu(ht_ref[rows, cols])).astype(at_ref.dtype)
            return carry

        lax.fori_loop(0, tm // tc, chunk, 0)
    acc_ref[...] += _dot(vt_ref[...], at_ref[...])

    @pl.when(e == pl.num_programs(1) - 1)
    def _():
        out_ref[...] = x_ref[...] + acc_ref[...].T


def _peer_dense(x2d, xnt, r2, e2, cnt, c, u, vt, tm, eb, tc):
    T = x2d.shape[0]
    tm = min(tm, T)
    tc = min(tc, tm)
    tok3 = pl.BlockSpec((PEER_HEADS, N_KEYS, tm), lambda i, e: (0, 0, i))
    return pl.pallas_call(
        functools.partial(_peer_dense_kernel, eb=eb, tc=tc),
        grid=(T // tm, N_EXPERTS // eb),
        in_specs=[pl.BlockSpec((tm, D_MODEL), lambda i, e: (i, 0)),
                  pl.BlockSpec((D_MODEL, tm), lambda i, e: (0, i)),
                  tok3, tok3, tok3, tok3,
                  pl.BlockSpec((eb, D_MODEL), lambda i, e: (e, 0)),
                  pl.BlockSpec((D_MODEL, eb), lambda i, e: (0, e))],
        out_specs=pl.BlockSpec((tm, D_MODEL), lambda i, e: (i, 0)),
        out_shape=jax.ShapeDtypeStruct((T, D_MODEL), F32),
        scratch_shapes=[pltpu.VMEM((D_MODEL, tm), F32),
                        pltpu.VMEM((eb, tm), F32),
                        pltpu.VMEM((eb, tm), CDT)],
        compiler_params=_cparams(("parallel", "arbitrary")),
        name="peer_dense",
    )(x2d, xnt, r2, e2, cnt, c, u, vt)


def _pad_heads(w, width):
    r = w.shape[0]
    w = w.reshape(r, MLA_HEADS, width)
    w = jnp.pad(w, ((0, 0), (0, 0), (0, HEAD_PAD - width)))
    return w.reshape(r, MLA_HEADS * HEAD_PAD)


def _layer_weights(l, norm1_g, w_in, q_lora_g, kv_lora_g, w_uq, w_ukv, q_head_g, k_head_g, w_o_mla,
                   pool_w, pool_scale, w_pool_out, gate_bias, w_out, norm2_g, peer_wq, peer_keys,
                   peer_u, peer_v):
    wi = w_in[l]
    c0 = Q_LORA
    c1 = c0 + KV_LORA
    c2 = c1 + QK_ROPE
    c3 = c2 + POOL_WIDTH
    zeros = lambda n: jnp.zeros((D_MODEL, n), wi.dtype)
    w_in_pad = jnp.concatenate([wi[:, :c1], zeros(QK_NOPE), wi[:, c1:c2], zeros(LANES - QK_HEAD),
                                wi[:, c2:]], axis=1)
    wkv = w_ukv[l].reshape(KV_LORA, MLA_HEADS, QK_NOPE + V_HEAD)
    w_uk = _pad_heads(wkv[:, :, :QK_NOPE].reshape(KV_LORA, MLA_HEADS * QK_NOPE), QK_NOPE)
    w_uv = wkv[:, :, QK_NOPE:].reshape(KV_LORA, MLA_WIDTH)
    pad_g = lambda g: jnp.pad(g, (0, LANES - QK_HEAD)).reshape(1, LANES)
    return {
        "g1": norm1_g[l].reshape(1, D_MODEL),
        "w_in": w_in_pad.astype(CDT),
        "qlg": q_lora_g[l].reshape(1, Q_LORA),
        "kvlg": kv_lora_g[l].reshape(1, KV_LORA),
        "w_uq": _pad_heads(w_uq[l], QK_HEAD).astype(CDT),
        "w_uk": w_uk.astype(CDT),
        "w_uv": w_uv.astype(CDT),
        "qg": pad_g(q_head_g[l] * (1.0 / math.sqrt(QK_HEAD))),
        "kg": pad_g(k_head_g[l]),
        "gate_bias": gate_bias[l].reshape(1, 2 * D_MODEL),
        "w_o_mla": w_o_mla[l].astype(CDT),
        "pool_w": pool_w[l].astype(CDT),
        "pool_scale": pool_scale[l].reshape(1, POOL_WIDTH),
        "w_pool_out": w_pool_out[l].astype(CDT),
        "w_out": w_out[l].astype(CDT),
        "g2": norm2_g[l].reshape(1, D_MODEL),
        "peer_wq": peer_wq[l].astype(CDT),
        "peer_keys": peer_keys[l].reshape(2 * PEER_HEADS, N_KEYS, PEER_HALF).astype(CDT),
        "peer_u": peer_u[l].astype(CDT),
        "peer_vt": peer_v[l].T.astype(CDT),
    }


def kernel(x, positions, norm1_g, w_in, q_lora_g, kv_lora_g, w_uq, w_ukv, q_head_g, k_head_g, w_o_mla,
           pool_w, pool_scale, w_pool_out, gate_bias, w_out, norm2_g, peer_wq, peer_keys, peer_u, peer_v):
    B, S, D = x.shape
    assert D == D_MODEL and S % LANES == 0
    T = B * S
    depth = norm1_g.shape[0]
    ropes = _rope_tables(positions)
    x2d = x.reshape(T, D)
    for l in range(depth):
        lw = _layer_weights(l, norm1_g, w_in, q_lora_g, kv_lora_g, w_uq, w_ukv, q_head_g, k_head_g,
                            w_o_mla, pool_w, pool_scale, w_pool_out, gate_bias, w_out, norm2_g,
                            peer_wq, peer_keys, peer_u, peer_v)
        q, k, v, p, gates = _in_proj(x2d, lw, ropes, tm=256)
        o = _attention(q, k, v, B, S, tq=256)
        y = _pool(p, lw, B, S)
        x2d, xnt, st = _out_proj(x2d, o, y, gates, lw, tm=256)
        r2, e2, cnt, c = _peer_topk(st, tmk=512)
        x2d = _peer_dense(x2d, xnt, r2, e2, cnt, c, lw["peer_u"], lw["peer_vt"], tm=512, eb=512, tc=256)
    return x2d.reshape(B, S, D)
```

```python
import functools
import math

import jax
import jax.numpy as jnp
import numpy as np
from jax import lax
from jax.experimental import pallas as pl
from jax.experimental.pallas import tpu as pltpu

D_MODEL = 1024
MLA_HEADS = 8
Q_LORA = 384
KV_LORA = 256
QK_NOPE = 64
QK_ROPE = 32
QK_HEAD = QK_NOPE + QK_ROPE
V_HEAD = 64
MLA_WIDTH = MLA_HEADS * V_HEAD
ROPE_BASE = 10000.0
POOL_WINDOWS = (2, 4, 8, 16)
POOL_WIDTH = 512
POOL_GC = 128
PEER_HEADS = 8
N_KEYS = 128
N_EXPERTS = N_KEYS * N_KEYS
PEER_HALF = 128
PEER_TOPK = 16
RMS_EPS = 1e-6

LANES = 128
HEAD_PAD = LANES
IN_PAD = Q_LORA + KV_LORA + LANES + POOL_WIDTH + 2 * D_MODEL
VMEM_LIMIT = 48 * 1024 * 1024

CDT = jnp.bfloat16
GDT = jnp.bfloat16
F32 = jnp.float32
NEG_INF = float("-inf")

_CAND_ROWS = 16 + 8 * 7 + 8
_FIDX_INVALID = 1 << 20


def _cparams(sem, vmem=VMEM_LIMIT, flags=None):
    return pltpu.CompilerParams(dimension_semantics=sem, vmem_limit_bytes=vmem, flags=flags)


def _rms(x, g):
    return x * lax.rsqrt(jnp.mean(x * x, axis=-1, keepdims=True) + RMS_EPS) * g


def _dot(a, b):
    return jnp.dot(a, b, preferred_element_type=F32)


def _dot_nt(a, b):
    return lax.dot_general(a, b, (((1,), (1,)), ((), ())), preferred_element_type=F32)


def _rope_kernel(pos_ref, inv_ref, c_ref, s1_ref, s2_ref):
    ang = pos_ref[...] * inv_ref[...]
    lane = lax.broadcasted_iota(jnp.int32, ang.shape, 1)
    cos = jnp.cos(ang)
    sin = jnp.sin(ang)
    half = QK_ROPE // 2
    c_ref[...] = jnp.where(lane < QK_NOPE, 1.0, jnp.where(lane < QK_HEAD, cos, 0.0))
    s1_ref[...] = jnp.where((lane >= QK_NOPE) & (lane < QK_NOPE + half), -sin, 0.0)
    s2_ref[...] = jnp.where((lane >= QK_NOPE + half) & (lane < QK_HEAD), sin, 0.0)


def _rope_tables(positions):
    T = positions.size
    tm = min(T, 2048)
    pos = jnp.broadcast_to(positions.reshape(T, 1).astype(F32), (T, LANES))
    inv = ROPE_BASE ** (-jnp.arange(0, QK_ROPE, 2, dtype=F32) / QK_ROPE)
    inv_pat = jnp.concatenate([jnp.zeros((QK_NOPE,), F32), inv, inv,
                               jnp.zeros((LANES - QK_HEAD,), F32)]).reshape(1, LANES)
    spec = pl.BlockSpec((tm, LANES), lambda i: (i, 0))
    return pl.pallas_call(
        _rope_kernel,
        grid=(T // tm,),
        in_specs=[spec, pl.BlockSpec((1, LANES), lambda i: (0, 0))],
        out_specs=[spec, spec, spec],
        out_shape=[jax.ShapeDtypeStruct((T, LANES), F32)] * 3,
        compiler_params=_cparams(("parallel",)),
        name="rope_tables",
    )(pos, inv_pat)


def _in_proj_kernel(x_ref, g1_ref, win_ref, qlg_ref, kvlg_ref, wuq_ref, wuk_ref, wuv_ref,
                    c_ref, s1_ref, s2_ref, qg_ref, kg_ref, bias_ref,
                    q_out, k_out, v_out, p_out, gate_out):
    xn = _rms(x_ref[...], g1_ref[...])
    y = _dot(xn.astype(CDT), win_ref[...])
    o = 0
    cq = y[:, o:o + Q_LORA]; o += Q_LORA
    ckv = y[:, o:o + KV_LORA]; o += KV_LORA
    kpe = y[:, o:o + LANES]; o += LANES
    p_out[...] = y[:, o:o + POOL_WIDTH]; o += POOL_WIDTH
    gate_out[...] = jax.nn.sigmoid(y[:, o:] + bias_ref[...]).astype(gate_out.dtype)

    cqn = _rms(cq, qlg_ref[...]).astype(CDT)
    ckvn = _rms(ckv, kvlg_ref[...]).astype(CDT)
    q_raw = _dot(cqn, wuq_ref[...])
    k_raw = _dot(ckvn, wuk_ref[...])
    v_out[...] = _dot(ckvn, wuv_ref[...]).astype(v_out.dtype)

    cpat = c_ref[...]
    s1pat = s1_ref[...]
    s2pat = s2_ref[...]
    half = QK_ROPE // 2

    def rope(xh):
        return (xh * cpat + pltpu.roll(xh, LANES - half, 1) * s1pat
                + pltpu.roll(xh, half, 1) * s2pat)

    def head_norm(xh, g):
        ms = jnp.sum(xh * xh, axis=-1, keepdims=True) * (1.0 / QK_HEAD)
        return xh * lax.rsqrt(ms + RMS_EPS) * g

    kpe_r = rope(kpe)
    qg = qg_ref[...]
    kg = kg_ref[...]
    for h in range(MLA_HEADS):
        sl = slice(h * HEAD_PAD, (h + 1) * HEAD_PAD)
        q_out[:, sl] = head_norm(rope(q_raw[:, sl]), qg).astype(q_out.dtype)
        k_out[:, sl] = head_norm(k_raw[:, sl] + kpe_r, kg).astype(k_out.dtype)


def _in_proj(x2d, lw, ropes, tm):
    T = x2d.shape[0]
    tm = min(tm, T)
    row = lambda w: pl.BlockSpec((tm, w), lambda i: (i, 0))
    full = lambda a: pl.BlockSpec(a.shape, lambda i: (0,) * a.ndim)
    consts = [lw["g1"], lw["w_in"], lw["qlg"], lw["kvlg"], lw["w_uq"], lw["w_uk"], lw["w_uv"]]
    tail = [lw["qg"], lw["kg"], lw["gate_bias"]]
    return pl.pallas_call(
        _in_proj_kernel,
        grid=(T // tm,),
        in_specs=[row(D_MODEL)] + [full(a) for a in consts] + [row(LANES)] * 3 + [full(a) for a in tail],
        out_specs=[row(MLA_HEADS * HEAD_PAD), row(MLA_HEADS * HEAD_PAD), row(MLA_WIDTH),
                   row(POOL_WIDTH), row(2 * D_MODEL)],
        out_shape=[jax.ShapeDtypeStruct((T, MLA_HEADS * HEAD_PAD), CDT),
                   jax.ShapeDtypeStruct((T, MLA_HEADS * HEAD_PAD), CDT),
                   jax.ShapeDtypeStruct((T, MLA_WIDTH), CDT),
                   jax.ShapeDtypeStruct((T, POOL_WIDTH), F32),
                   jax.ShapeDtypeStruct((T, 2 * D_MODEL), CDT)],
        compiler_params=_cparams(("parallel",)),
        name="in_proj",
    )(x2d, *consts, *ropes, *tail)


def _attn_kernel(q_ref, k_ref, v_ref, o_ref):
    outs = []
    for j in range(2):
        q = q_ref[:, j * HEAD_PAD:(j + 1) * HEAD_PAD]
        k = k_ref[:, j * HEAD_PAD:(j + 1) * HEAD_PAD]
        v = v_ref[:, j * V_HEAD:(j + 1) * V_HEAD]
        s = _dot_nt(q, k)
        m = jnp.max(s, axis=-1, keepdims=True)
        p = jnp.exp(s - m)
        l = jnp.sum(p, axis=-1, keepdims=True)
        outs.append(_dot(p.astype(CDT), v) / l)
    o_ref[...] = jnp.concatenate(outs, axis=-1).astype(o_ref.dtype)


def _attention(q, k, v, B, S, tq):
    tq = min(tq, S)
    nq = S // tq
    return pl.pallas_call(
        _attn_kernel,
        grid=(B, MLA_HEADS // 2, nq),
        in_specs=[pl.BlockSpec((tq, 2 * HEAD_PAD), lambda b, h, i: (b * nq + i, h)),
                  pl.BlockSpec((S, 2 * HEAD_PAD), lambda b, h, i: (b, h)),
                  pl.BlockSpec((S, 2 * V_HEAD), lambda b, h, i: (b, h))],
        out_specs=pl.BlockSpec((tq, 2 * V_HEAD), lambda b, h, i: (b * nq + i, h)),
        out_shape=jax.ShapeDtypeStruct((B * S, MLA_WIDTH), CDT),
        compiler_params=_cparams(("parallel", "parallel", "parallel")),
        name="attention",
    )(q, k, v)


def _pool_kernel(p_ref, pw_ref, ps_ref, y_ref):
    S = p_ref.shape[0]
    t = lax.broadcasted_iota(jnp.int32, (S, POOL_GC), 0)
    for g, w in enumerate(POOL_WINDOWS):
        half = w // 2
        sl = slice(g * POOL_GC, (g + 1) * POOL_GC)
        pg = p_ref[:, sl]
        acc = pg
        for d in range(-half, half):
            if d == 0:
                continue
            shifted = pltpu.roll(pg, (-d) % S, 0)
            valid = (t + d >= 0) & (t + d < S)
            acc = acc + jnp.where(valid, shifted, 0.0)
        count = (jnp.minimum(t + half, S) - jnp.maximum(t - half, 0)).astype(F32)
        mixed = acc / count - pg
        yg = _dot(mixed.astype(CDT), pw_ref[g]) * ps_ref[:, sl]
        y_ref[:, sl] = yg.astype(y_ref.dtype)


def _pool(p, lw, B, S):
    return pl.pallas_call(
        _pool_kernel,
        grid=(B,),
        in_specs=[pl.BlockSpec((S, POOL_WIDTH), lambda b: (b, 0)),
                  pl.BlockSpec(lw["pool_w"].shape, lambda b: (0, 0, 0)),
                  pl.BlockSpec((1, POOL_WIDTH), lambda b: (0, 0))],
        out_specs=pl.BlockSpec((S, POOL_WIDTH), lambda b: (b, 0)),
        out_shape=jax.ShapeDtypeStruct((B * S, POOL_WIDTH), CDT),
        compiler_params=_cparams(("parallel",)),
        name="pool",
    )(p, lw["pool_w"], lw["pool_scale"])


def _out_proj_kernel(x_ref, o_ref, y_ref, gate_ref, wo_ref, wpo_ref, wout_ref, g2_ref, wq_ref, keys_ref,
                     xnew_out, xnt_out, st_out):
    a = _dot(o_ref[...], wo_ref[...])
    bp = _dot(y_ref[...], wpo_ref[...])
    ga = gate_ref[:, :D_MODEL].astype(F32)
    gb = gate_ref[:, D_MODEL:].astype(F32)
    mix = ga * a + gb * bp
    xnew = x_ref[...] + _dot(mix.astype(CDT), wout_ref[...])
    xnew_out[...] = xnew
    xn = _rms(xnew, g2_ref[...])
    xnt_out[...] = xn.T.astype(xnt_out.dtype)
    qp = _dot(xn.astype(CDT), wq_ref[...])
    for hp in range(2 * PEER_HEADS):
        qh = qp[:, hp * PEER_HALF:(hp + 1) * PEER_HALF].astype(CDT)
        st_out[hp] = _dot_nt(keys_ref[hp], qh)


def _out_proj(x2d, o, y, gates, lw, tm):
    T = x2d.shape[0]
    tm = min(tm, T)
    row = lambda w: pl.BlockSpec((tm, w), lambda i: (i, 0))
    full = lambda a: pl.BlockSpec(a.shape, lambda i: (0,) * a.ndim)
    consts = [lw["w_o_mla"], lw["w_pool_out"], lw["w_out"], lw["g2"], lw["peer_wq"], lw["peer_keys"]]
    return pl.pallas_call(
        _out_proj_kernel,
        grid=(T // tm,),
        in_specs=[row(D_MODEL), row(MLA_WIDTH), row(POOL_WIDTH), row(2 * D_MODEL)] + [full(a) for a in consts],
        out_specs=[row(D_MODEL),
                   pl.BlockSpec((D_MODEL, tm), lambda i: (0, i)),
                   pl.BlockSpec((2 * PEER_HEADS, N_KEYS, tm), lambda i: (0, 0, i))],
        out_shape=[jax.ShapeDtypeStruct((T, D_MODEL), F32),
                   jax.ShapeDtypeStruct((D_MODEL, T), CDT),
                   jax.ShapeDtypeStruct((2 * PEER_HEADS, N_KEYS, T), F32)],
        compiler_params=_cparams(("parallel",)),
        name="out_proj",
    )(x2d, o, y, gates, *consts)


def _top16(s, key_iota, row16):
    work = s
    rank = jnp.full(s.shape, PEER_TOPK, jnp.int32)
    tops = jnp.zeros((PEER_TOPK, s.shape[1]), F32)
    for r in range(PEER_TOPK):
        m = jnp.max(work, axis=0, keepdims=True)
        first = jnp.min(jnp.where(work == m, key_iota, N_KEYS), axis=0, keepdims=True)
        sel = key_iota == first
        rank = jnp.where(sel, r, rank)
        work = jnp.where(sel, NEG_INF, work)
        tops = jnp.where(row16 == r, m, tops)
    return rank, tops


def _cand_grid(ta, tb, combine):
    pieces = [combine(ta[0:1], tb)]
    for r1 in range(1, 8):
        pieces.append(combine(ta[r1:r1 + 1], tb[0:8]))
    pieces.append(combine(ta[8:16], tb[0:1]))
    return jnp.concatenate(pieces, axis=0)


def _peer_topk_kernel(s_ref, fidx_ref, r2_out, e2_out, cnt_out, c_out, *, n_chunks):
    key_iota = lax.broadcasted_iota(jnp.int32, (N_KEYS, LANES), 0)
    row16 = lax.broadcasted_iota(jnp.int32, (PEER_TOPK, LANES), 0)
    fidx = fidx_ref[...]
    cand_valid = fidx < _FIDX_INVALID

    def body(it, carry):
        h = it // n_chunks
        off = pl.multiple_of((it % n_chunks) * LANES, LANES)
        s1 = s_ref[2 * h, :, pl.ds(off, LANES)]
        s2 = s_ref[2 * h + 1, :, pl.ds(off, LANES)]
        rank1, ta = _top16(s1, key_iota, row16)
        rank2, tb = _top16(s2, key_iota, row16)
        a0 = ta[0:1]
        b0 = tb[0:1]
        cand = jnp.where(cand_valid, _cand_grid(ta, tb, lambda a, b: a + b), NEG_INF)
        ecand = _cand_grid(jnp.exp(ta - a0), jnp.exp(tb - b0), lambda a, b: a * b)
        selected = jnp.zeros(cand.shape, F32)
        for _ in range(PEER_TOPK):
            m = jnp.max(cand, axis=0, keepdims=True)
            first = jnp.min(jnp.where(cand == m, fidx, _FIDX_INVALID), axis=0, keepdims=True)
            sel = fidx == first
            selected = jnp.where(sel, 1.0, selected)
            cand = jnp.where(sel, NEG_INF, cand)
        z = jnp.sum(selected * ecand, axis=0, keepdims=True)
        cntr = [jnp.sum(selected[0:16], axis=0, keepdims=True)]
        for r1 in range(1, 8):
            cntr.append(jnp.sum(selected[8 + 8 * r1:16 + 8 * r1], axis=0, keepdims=True))
        for r1 in range(8, 16):
            cntr.append(selected[64 + r1:65 + r1])
        cnt = jnp.zeros(s1.shape, F32)
        for r1 in range(PEER_TOPK):
            cnt = jnp.where(rank1 == r1, cntr[r1], cnt)
        r2_out[h, :, pl.ds(off, LANES)] = rank2.astype(F32).astype(r2_out.dtype)
        e2_out[h, :, pl.ds(off, LANES)] = jnp.exp(s2 - b0).astype(e2_out.dtype)
        cnt_out[h, :, pl.ds(off, LANES)] = cnt.astype(cnt_out.dtype)
        c_out[h, :, pl.ds(off, LANES)] = (jnp.exp(s1 - a0) / z).astype(c_out.dtype)
        return carry

    lax.fori_loop(0, PEER_HEADS * n_chunks, body, 0)


def _cand_fidx():
    rows = []
    rows += [0 * 16 + r2 for r2 in range(16)]
    for r1 in range(1, 8):
        n = PEER_TOPK // (r1 + 1)
        rows += [r1 * 16 + r2 if r2 < n else _FIDX_INVALID + r1 * 16 + r2 for r2 in range(8)]
    rows += [r1 * 16 for r1 in range(8, 16)]
    assert len(rows) == _CAND_ROWS
    return jnp.broadcast_to(jnp.asarray(rows, jnp.int32)[:, None], (_CAND_ROWS, LANES))


def _peer_topk(st, tmk):
    T = st.shape[-1]
    tmk = min(tmk, T)
    spec = pl.BlockSpec((PEER_HEADS, N_KEYS, tmk), lambda i: (0, 0, i))
    shp = jax.ShapeDtypeStruct((PEER_HEADS, N_KEYS, T), GDT)
    shp32 = jax.ShapeDtypeStruct((PEER_HEADS, N_KEYS, T), F32)
    return pl.pallas_call(
        functools.partial(_peer_topk_kernel, n_chunks=tmk // LANES),
        grid=(T // tmk,),
        in_specs=[pl.BlockSpec((2 * PEER_HEADS, N_KEYS, tmk), lambda i: (0, 0, i)),
                  pl.BlockSpec((_CAND_ROWS, LANES), lambda i: (0, 0))],
        out_specs=[spec] * 4,
        out_shape=[shp, shp, shp32, shp32],
        compiler_params=_cparams(("parallel",)),
        name="peer_topk",
    )(st, _cand_fidx())


def _peer_dense_kernel(x_ref, xnt_ref, r2_ref, e2_ref, cnt_ref, c_ref, u_ref, vt_ref, out_ref,
                       acc_ref, ht_ref, at_ref, *, eb, tc):
    e = pl.program_id(1)
    tm = xnt_ref.shape[1]

    @pl.when(e == 0)
    def _():
        acc_ref[...] = jnp.zeros_like(acc_ref)

    ht_ref[...] = _dot(u_ref[...], xnt_ref[...])
    for jb in range(eb // N_KEYS):
        j = e * (eb // N_KEYS) + jb
        rows = slice(jb * N_KEYS, (jb + 1) * N_KEYS)

        def chunk(ci, carry, j=j, rows=rows):
            cols = pl.ds(pl.multiple_of(ci * tc, tc), tc)
            g = jnp.zeros((N_KEYS, tc), GDT)
            for h in range(PEER_HEADS):
                cnt = cnt_ref[h, pl.ds(j, 1), cols].astype(GDT)
                coef = c_ref[h, pl.ds(j, 1), cols].astype(GDT)
                g = g + jnp.where(r2_ref[h, :, cols] < cnt, e2_ref[h, :, cols] * coef, 0)
            at_ref[rows, cols] = (g * jax.nn.gelu(ht_ref[rows, cols]).astype(GDT)).astype(at_ref.dtype)
            return carry

        lax.fori_loop(0, tm // tc, chunk, 0)
    acc_ref[...] += _dot(vt_ref[...], at_ref[...])

    @pl.when(e == pl.num_programs(1) - 1)
    def _():
        out_ref[...] = x_ref[...] + acc_ref[...].T


def _peer_dense(x2d, xnt, r2, e2, cnt, c, u, vt, tm, eb, tc):
    T = x2d.shape[0]
    tm = min(tm, T)
    tc = min(tc, tm)
    tok3 = pl.BlockSpec((PEER_HEADS, N_KEYS, tm), lambda i, e: (0, 0, i))
    return pl.pallas_call(
        functools.partial(_peer_dense_kernel, eb=eb, tc=tc),
        grid=(T // tm, N_EXPERTS // eb),
        in_specs=[pl.BlockSpec((tm, D_MODEL), lambda i, e: (i, 0)),
                  pl.BlockSpec((D_MODEL, tm), lambda i, e: (0, i)),
                  tok3, tok3, tok3, tok3,
                  pl.BlockSpec((eb, D_MODEL), lambda i, e: (e, 0)),
                  pl.BlockSpec((D_MODEL, eb), lambda i, e: (0, e))],
        out_specs=pl.BlockSpec((tm, D_MODEL), lambda i, e: (i, 0)),
        out_shape=jax.ShapeDtypeStruct((T, D_MODEL), F32),
        scratch_shapes=[pltpu.VMEM((D_MODEL, tm), F32),
                        pltpu.VMEM((eb, tm), F32),
                        pltpu.VMEM((eb, tm), CDT)],
        compiler_params=_cparams(("parallel", "arbitrary")),
        name="peer_dense",
    )(x2d, xnt, r2, e2, cnt, c, u, vt)


def _pad_heads(w, width):
    r = w.shape[0]
    w = w.reshape(r, MLA_HEADS, width)
    w = jnp.pad(w, ((0, 0), (0, 0), (0, HEAD_PAD - width)))
    return w.reshape(r, MLA_HEADS * HEAD_PAD)


def _layer_weights(l, norm1_g, w_in, q_lora_g, kv_lora_g, w_uq, w_ukv, q_head_g, k_head_g, w_o_mla,
                   pool_w, pool_scale, w_pool_out, gate_bias, w_out, norm2_g, peer_wq, peer_keys,
                   peer_u, peer_v):
    wi = w_in[l]
    c0 = Q_LORA
    c1 = c0 + KV_LORA
    c2 = c1 + QK_ROPE
    c3 = c2 + POOL_WIDTH
    zeros = lambda n: jnp.zeros((D_MODEL, n), wi.dtype)
    w_in_pad = jnp.concatenate([wi[:, :c1], zeros(QK_NOPE), wi[:, c1:c2], zeros(LANES - QK_HEAD),
                                wi[:, c2:]], axis=1)
    wkv = w_ukv[l].reshape(KV_LORA, MLA_HEADS, QK_NOPE + V_HEAD)
    w_uk = _pad_heads(wkv[:, :, :QK_NOPE].reshape(KV_LORA, MLA_HEADS * QK_NOPE), QK_NOPE)
    w_uv = wkv[:, :, QK_NOPE:].reshape(KV_LORA, MLA_WIDTH)
    pad_g = lambda g: jnp.pad(g, (0, LANES - QK_HEAD)).reshape(1, LANES)
    return {
        "g1": norm1_g[l].reshape(1, D_MODEL),
        "w_in": w_in_pad.astype(CDT),
        "qlg": q_lora_g[l].reshape(1, Q_LORA),
        "kvlg": kv_lora_g[l].reshape(1, KV_LORA),
        "w_uq": _pad_heads(w_uq[l], QK_HEAD).astype(CDT),
        "w_uk": w_uk.astype(CDT),
        "w_uv": w_uv.astype(CDT),
        "qg": pad_g(q_head_g[l] * (1.0 / math.sqrt(QK_HEAD))),
        "kg": pad_g(k_head_g[l]),
        "gate_bias": gate_bias[l].reshape(1, 2 * D_MODEL),
        "w_o_mla": w_o_mla[l].astype(CDT),
        "pool_w": pool_w[l].astype(CDT),
        "pool_scale": pool_scale[l].reshape(1, POOL_WIDTH),
        "w_pool_out": w_pool_out[l].astype(CDT),
        "w_out": w_out[l].astype(CDT),
        "g2": norm2_g[l].reshape(1, D_MODEL),
        "peer_wq": peer_wq[l].astype(CDT),
        "peer_keys": peer_keys[l].reshape(2 * PEER_HEADS, N_KEYS, PEER_HALF).astype(CDT),
        "peer_u": peer_u[l].astype(CDT),
        "peer_vt": peer_v[l].T.astype(CDT),
    }


def kernel(x, positions, norm1_g, w_in, q_lora_g, kv_lora_g, w_uq, w_ukv, q_head_g, k_head_g, w_o_mla,
           pool_w, pool_scale, w_pool_out, gate_bias, w_out, norm2_g, peer_wq, peer_keys, peer_u, peer_v):
    B, S, D = x.shape
    assert D == D_MODEL and S % LANES == 0
    T = B * S
    depth = norm1_g.shape[0]
    ropes = _rope_tables(positions)
    x2d = x.reshape(T, D)
    for l in range(depth):
        lw = _layer_weights(l, norm1_g, w_in, q_lora_g, kv_lora_g, w_uq, w_ukv, q_head_g, k_head_g,
                            w_o_mla, pool_w, pool_scale, w_pool_out, gate_bias, w_out, norm2_g,
                            peer_wq, peer_keys, peer_u, peer_v)
        q, k, v, p, gates = _in_proj(x2d, lw, ropes, tm=256)
        o = _attention(q, k, v, B, S, tq=256)
        y = _pool(p, lw, B, S)
        x2d, xnt, st = _out_proj(x2d, o, y, gates, lw, tm=256)
        r2, e2, cnt, c = _peer_topk(st, tmk=512)
        x2d = _peer_dense(x2d, xnt, r2, e2, cnt, c, lw["peer_u"], lw["peer_vt"], tm=512, eb=512, tc=256)
    return x2d.reshape(B, S, D)
```

```python
import functools
import math

import jax
import jax.numpy as jnp
import numpy as np
from jax import lax
from jax.experimental import pallas as pl
from jax.experimental.pallas import tpu as pltpu

D_MODEL = 1024
MLA_HEADS = 8
Q_LORA = 384
KV_LORA = 256
QK_NOPE = 64
QK_ROPE = 32
QK_HEAD = QK_NOPE + QK_ROPE
V_HEAD = 64
MLA_WIDTH = MLA_HEADS * V_HEAD
ROPE_BASE = 10000.0
POOL_WINDOWS = (2, 4, 8, 16)
POOL_WIDTH = 512
POOL_GC = 128
PEER_HEADS = 8
N_KEYS = 128
N_EXPERTS = N_KEYS * N_KEYS
PEER_HALF = 128
PEER_TOPK = 16
RMS_EPS = 1e-6

LANES = 128
HEAD_PAD = LANES
IN_PAD = Q_LORA + KV_LORA + LANES + POOL_WIDTH + 2 * D_MODEL
VMEM_LIMIT = 48 * 1024 * 1024

CDT = jnp.bfloat16
GDT = jnp.bfloat16
F32 = jnp.float32
NEG_INF = float("-inf")
_FLT_MAX = float(np.finfo(np.float32).max)
_FLT_TOP_ULP = 2.0 ** 104

_CAND_ROWS = 16 + 8 * 7 + 8
_FIDX_INVALID = 1 << 20


def _cparams(sem, vmem=VMEM_LIMIT, flags=None):
    return pltpu.CompilerParams(dimension_semantics=sem, vmem_limit_bytes=vmem, flags=flags)


def _rms(x, g):
    return x * lax.rsqrt(jnp.mean(x * x, axis=-1, keepdims=True) + RMS_EPS) * g


def _dot(a, b):
    return jnp.dot(a, b, preferred_element_type=F32)


def _dot_nt(a, b):
    return lax.dot_general(a, b, (((1,), (1,)), ((), ())), preferred_element_type=F32)


def _rope_kernel(pos_ref, inv_ref, c_ref, s1_ref, s2_ref):
    ang = pos_ref[...] * inv_ref[...]
    lane = lax.broadcasted_iota(jnp.int32, ang.shape, 1)
    cos = jnp.cos(ang)
    sin = jnp.sin(ang)
    half = QK_ROPE // 2
    c_ref[...] = jnp.where(lane < QK_NOPE, 1.0, jnp.where(lane < QK_HEAD, cos, 0.0))
    s1_ref[...] = jnp.where((lane >= QK_NOPE) & (lane < QK_NOPE + half), -sin, 0.0)
    s2_ref[...] = jnp.where((lane >= QK_NOPE + half) & (lane < QK_HEAD), sin, 0.0)


def _rope_tables(positions):
    T = positions.size
    tm = min(T, 2048)
    pos = jnp.broadcast_to(positions.reshape(T, 1).astype(F32), (T, LANES))
    inv = ROPE_BASE ** (-jnp.arange(0, QK_ROPE, 2, dtype=F32) / QK_ROPE)
    inv_pat = jnp.concatenate([jnp.zeros((QK_NOPE,), F32), inv, inv,
                               jnp.zeros((LANES - QK_HEAD,), F32)]).reshape(1, LANES)
    spec = pl.BlockSpec((tm, LANES), lambda i: (i, 0))
    return pl.pallas_call(
        _rope_kernel,
        grid=(T // tm,),
        in_specs=[spec, pl.BlockSpec((1, LANES), lambda i: (0, 0))],
        out_specs=[spec, spec, spec],
        out_shape=[jax.ShapeDtypeStruct((T, LANES), F32)] * 3,
        compiler_params=_cparams(("parallel",)),
        name="rope_tables",
    )(pos, inv_pat)


def _in_proj_kernel(x_ref, g1_ref, win_ref, qlg_ref, kvlg_ref, wuq_ref, wuk_ref, wuv_ref,
                    c_ref, s1_ref, s2_ref, qg_ref, kg_ref, bias_ref,
                    q_out, k_out, v_out, p_out, gate_out):
    xn = _rms(x_ref[...], g1_ref[...])
    y = _dot(xn.astype(CDT), win_ref[...])
    o = 0
    cq = y[:, o:o + Q_LORA]; o += Q_LORA
    ckv = y[:, o:o + KV_LORA]; o += KV_LORA
    kpe = y[:, o:o + LANES]; o += LANES
    p_out[...] = y[:, o:o + POOL_WIDTH]; o += POOL_WIDTH
    gate_out[...] = jax.nn.sigmoid(y[:, o:] + bias_ref[...]).astype(gate_out.dtype)

    cqn = _rms(cq, qlg_ref[...]).astype(CDT)
    ckvn = _rms(ckv, kvlg_ref[...]).astype(CDT)
    q_raw = _dot(cqn, wuq_ref[...])
    k_raw = _dot(ckvn, wuk_ref[...])
    v_out[...] = _dot(ckvn, wuv_ref[...]).astype(v_out.dtype)

    cpat = c_ref[...]
    s1pat = s1_ref[...]
    s2pat = s2_ref[...]
    half = QK_ROPE // 2

    def rope(xh):
        return (xh * cpat + pltpu.roll(xh, LANES - half, 1) * s1pat
                + pltpu.roll(xh, half, 1) * s2pat)

    def head_norm(xh, g):
        ms = jnp.sum(xh * xh, axis=-1, keepdims=True) * (1.0 / QK_HEAD)
        return xh * lax.rsqrt(ms + RMS_EPS) * g

    kpe_r = rope(kpe)
    qg = qg_ref[...]
    kg = kg_ref[...]
    for h in range(MLA_HEADS):
        sl = slice(h * HEAD_PAD, (h + 1) * HEAD_PAD)
        q_out[:, sl] = head_norm(rope(q_raw[:, sl]), qg).astype(q_out.dtype)
        k_out[:, sl] = head_norm(k_raw[:, sl] + kpe_r, kg).astype(k_out.dtype)


def _in_proj(x2d, lw, ropes, tm):
    T = x2d.shape[0]
    tm = min(tm, T)
    row = lambda w: pl.BlockSpec((tm, w), lambda i: (i, 0))
    full = lambda a: pl.BlockSpec(a.shape, lambda i: (0,) * a.ndim)
    consts = [lw["g1"], lw["w_in"], lw["qlg"], lw["kvlg"], lw["w_uq"], lw["w_uk"], lw["w_uv"]]
    tail = [lw["qg"], lw["kg"], lw["gate_bias"]]
    return pl.pallas_call(
        _in_proj_kernel,
        grid=(T // tm,),
        in_specs=[row(D_MODEL)] + [full(a) for a in consts] + [row(LANES)] * 3 + [full(a) for a in tail],
        out_specs=[row(MLA_HEADS * HEAD_PAD), row(MLA_HEADS * HEAD_PAD), row(MLA_WIDTH),
                   row(POOL_WIDTH), row(2 * D_MODEL)],
        out_shape=[jax.ShapeDtypeStruct((T, MLA_HEADS * HEAD_PAD), CDT),
                   jax.ShapeDtypeStruct((T, MLA_HEADS * HEAD_PAD), CDT),
                   jax.ShapeDtypeStruct((T, MLA_WIDTH), CDT),
                   jax.ShapeDtypeStruct((T, POOL_WIDTH), F32),
                   jax.ShapeDtypeStruct((T, 2 * D_MODEL), CDT)],
        compiler_params=_cparams(("parallel",)),
        name="in_proj",
    )(x2d, *consts, *ropes, *tail)


def _attn_kernel(q_ref, k_ref, v_ref, o_ref):
    outs = []
    for j in range(2):
        q = q_ref[:, j * HEAD_PAD:(j + 1) * HEAD_PAD]
        k = k_ref[:, j * HEAD_PAD:(j + 1) * HEAD_PAD]
        v = v_ref[:, j * V_HEAD:(j + 1) * V_HEAD]
        s = _dot_nt(q, k)
        m = jnp.max(s, axis=-1, keepdims=True)
        p = jnp.exp(s - m)
        l = jnp.sum(p, axis=-1, keepdims=True)
        outs.append(_dot(p.astype(CDT), v) / l)
    o_ref[...] = jnp.concatenate(outs, axis=-1).astype(o_ref.dtype)


def _attention(q, k, v, B, S, tq):
    tq = min(tq, S)
    nq = S // tq
    return pl.pallas_call(
        _attn_kernel,
        grid=(B, MLA_HEADS // 2, nq),
        in_specs=[pl.BlockSpec((tq, 2 * HEAD_PAD), lambda b, h, i: (b * nq + i, h)),
                  pl.BlockSpec((S, 2 * HEAD_PAD), lambda b, h, i: (b, h)),
                  pl.BlockSpec((S, 2 * V_HEAD), lambda b, h, i: (b, h))],
        out_specs=pl.BlockSpec((tq, 2 * V_HEAD), lambda b, h, i: (b * nq + i, h)),
        out_shape=jax.ShapeDtypeStruct((B * S, MLA_WIDTH), CDT),
        compiler_params=_cparams(("parallel", "parallel", "parallel")),
        name="attention",
    )(q, k, v)


def _pool_kernel(p_ref, pw_ref, ps_ref, y_ref):
    S = p_ref.shape[0]
    t = lax.broadcasted_iota(jnp.int32, (S, POOL_GC), 0)
    for g, w in enumerate(POOL_WINDOWS):
        half = w // 2
        sl = slice(g * POOL_GC, (g + 1) * POOL_GC)
        pg = p_ref[:, sl]
        acc = pg
        for d in range(-half, half):
            if d == 0:
                continue
            shifted = pltpu.roll(pg, (-d) % S, 0)
            valid = (t + d >= 0) & (t + d < S)
            acc = acc + jnp.where(valid, shifted, 0.0)
        count = (jnp.minimum(t + half, S) - jnp.maximum(t - half, 0)).astype(F32)
        mixed = acc / count - pg
        yg = _dot(mixed.astype(CDT), pw_ref[g]) * ps_ref[:, sl]
        y_ref[:, sl] = yg.astype(y_ref.dtype)


def _pool(p, lw, B, S):
    return pl.pallas_call(
        _pool_kernel,
        grid=(B,),
        in_specs=[pl.BlockSpec((S, POOL_WIDTH), lambda b: (b, 0)),
                  pl.BlockSpec(lw["pool_w"].shape, lambda b: (0, 0, 0)),
                  pl.BlockSpec((1, POOL_WIDTH), lambda b: (0, 0))],
        out_specs=pl.BlockSpec((S, POOL_WIDTH), lambda b: (b, 0)),
        out_shape=jax.ShapeDtypeStruct((B * S, POOL_WIDTH), CDT),
        compiler_params=_cparams(("parallel",)),
        name="pool",
    )(p, lw["pool_w"], lw["pool_scale"])


def _out_proj_kernel(x_ref, o_ref, y_ref, gate_ref, wo_ref, wpo_ref, wout_ref, g2_ref, wq_ref, keys_ref,
                     xnew_out, xnt_out, st_out):
    a = _dot(o_ref[...], wo_ref[...])
    bp = _dot(y_ref[...], wpo_ref[...])
    ga = gate_ref[:, :D_MODEL].astype(F32)
    gb = gate_ref[:, D_MODEL:].astype(F32)
    mix = ga * a + gb * bp
    xnew = x_ref[...] + _dot(mix.astype(CDT), wout_ref[...])
    xnew_out[...] = xnew
    xn = _rms(xnew, g2_ref[...])
    xnt_out[...] = xn.T.astype(xnt_out.dtype)
    qp = _dot(xn.astype(CDT), wq_ref[...])
    for hp in range(2 * PEER_HEADS):
        qh = qp[:, hp * PEER_HALF:(hp + 1) * PEER_HALF].astype(CDT)
        st_out[hp] = _dot_nt(keys_ref[hp], qh)


def _out_proj(x2d, o, y, gates, lw, tm):
    T = x2d.shape[0]
    tm = min(tm, T)
    row = lambda w: pl.BlockSpec((tm, w), lambda i: (i, 0))
    full = lambda a: pl.BlockSpec(a.shape, lambda i: (0,) * a.ndim)
    consts = [lw["w_o_mla"], lw["w_pool_out"], lw["w_out"], lw["g2"], lw["peer_wq"], lw["peer_keys"]]
    return pl.pallas_call(
        _out_proj_kernel,
        grid=(T // tm,),
        in_specs=[row(D_MODEL), row(MLA_WIDTH), row(POOL_WIDTH), row(2 * D_MODEL)] + [full(a) for a in consts],
        out_specs=[row(D_MODEL),
                   pl.BlockSpec((D_MODEL, tm), lambda i: (0, i)),
                   pl.BlockSpec((2 * PEER_HEADS, N_KEYS, tm), lambda i: (0, 0, i))],
        out_shape=[jax.ShapeDtypeStruct((T, D_MODEL), F32),
                   jax.ShapeDtypeStruct((D_MODEL, T), CDT),
                   jax.ShapeDtypeStruct((2 * PEER_HEADS, N_KEYS, T), F32)],
        compiler_params=_cparams(("parallel",)),
        name="out_proj",
    )(x2d, o, y, gates, *consts)


def _top16(s, key_iota, row16, exact):
    work = s
    rank = jnp.full(s.shape, float(PEER_TOPK), F32)
    tops = jnp.zeros((PEER_TOPK, s.shape[1]), F32)
    for r in range(PEER_TOPK):
        m = jnp.max(work, axis=0, keepdims=True)
        sel = work == m
        if exact:
            first = jnp.min(jnp.where(sel, key_iota, float(N_KEYS)), axis=0, keepdims=True)
            sel = key_iota == first
            rank = jnp.where(sel, float(r), rank)
            work = jnp.where(sel, NEG_INF, work)
        else:
            work = jnp.where(sel, -_FLT_MAX + r * _FLT_TOP_ULP, work)
        tops = jnp.where(row16 == r, m, tops)
    if not exact:
        rank = jnp.minimum((work + _FLT_MAX) * (1.0 / _FLT_TOP_ULP), float(PEER_TOPK))
    return rank, tops


def _cand_grid(ta, tb, combine):
    pieces = [combine(ta[0:1], tb)]
    for r1 in range(1, 8):
        pieces.append(combine(ta[r1:r1 + 1], tb[0:8]))
    pieces.append(combine(ta[8:16], tb[0:1]))
    return jnp.concatenate(pieces, axis=0)


def _count_true(mask):
    return jnp.sum(jnp.where(mask, 1.0, 0.0), axis=0, keepdims=True)


def _route(s1, s2, key_iota, row16, fidx, exact):
    rank1, ta = _top16(s1, key_iota, row16, exact)
    rank2, tb = _top16(s2, key_iota, row16, exact)
    a0 = ta[0:1]
    b0 = tb[0:1]
    cand0 = jnp.where(fidx < _FIDX_INVALID, _cand_grid(ta, tb, lambda a, b: a + b), NEG_INF)
    ecand = _cand_grid(jnp.exp(ta - a0), jnp.exp(tb - b0), lambda a, b: a * b)
    cand = cand0
    if exact:
        selected = jnp.zeros(cand.shape, F32)
        for _ in range(PEER_TOPK):
            m = jnp.max(cand, axis=0, keepdims=True)
            first = jnp.min(jnp.where(cand == m, fidx, float(_FIDX_INVALID)), axis=0, keepdims=True)
            sel = fidx == first
            selected = jnp.where(sel, 1.0, selected)
            cand = jnp.where(sel, NEG_INF, cand)
        bad = jnp.zeros_like(a0)
    else:
        for _ in range(PEER_TOPK):
            m = jnp.max(cand, axis=0, keepdims=True)
            cand = jnp.where(cand == m, NEG_INF, cand)
        selected = jnp.where(cand0 >= m, 1.0, 0.0)
        k = float(PEER_TOPK)
        bad = jnp.where((_count_true(rank1 < k) != k) | (_count_true(rank2 < k) != k)
                        | (jnp.sum(selected, axis=0, keepdims=True) != k), 1.0, 0.0)
    z = jnp.sum(selected * ecand, axis=0, keepdims=True)
    cntr = [jnp.sum(selected[0:16], axis=0, keepdims=True)]
    for r1 in range(1, 8):
        cntr.append(jnp.sum(selected[8 + 8 * r1:16 + 8 * r1], axis=0, keepdims=True))
    for r1 in range(8, 16):
        cntr.append(selected[64 + r1:65 + r1])
    cnt = jnp.zeros(s1.shape, F32)
    for r1 in range(PEER_TOPK):
        cnt = jnp.where(rank1 == float(r1), cntr[r1], cnt)
    return rank2, jnp.exp(s2 - b0), cnt, jnp.exp(s1 - a0) / z, bad


def _peer_topk_kernel(s_ref, fidx_ref, r2_out, e2_out, cnt_out, c_out, *, n_chunks):
    key_iota = lax.broadcasted_iota(jnp.int32, (N_KEYS, LANES), 0).astype(F32)
    row16 = lax.broadcasted_iota(jnp.int32, (PEER_TOPK, LANES), 0)
    fidx = fidx_ref[...]

    def body(it, carry):
        h = it // n_chunks
        off = pl.multiple_of((it % n_chunks) * LANES, LANES)
        s1 = s_ref[2 * h, :, pl.ds(off, LANES)]
        s2 = s_ref[2 * h + 1, :, pl.ds(off, LANES)]

        def store(r2, e2, cnt, c):
            r2_out[h, :, pl.ds(off, LANES)] = r2.astype(r2_out.dtype)
            e2_out[h, :, pl.ds(off, LANES)] = e2.astype(e2_out.dtype)
            cnt_out[h, :, pl.ds(off, LANES)] = cnt.astype(cnt_out.dtype)
            c_out[h, :, pl.ds(off, LANES)] = c.astype(c_out.dtype)

        *fast, bad = _route(s1, s2, key_iota, row16, fidx, exact=False)
        store(*fast)

        @pl.when(jnp.max(bad) > 0.0)
        def _():
            *slow, _ = _route(s1, s2, key_iota, row16, fidx, exact=True)
            store(*slow)

        return carry

    lax.fori_loop(0, PEER_HEADS * n_chunks, body, 0)


def _cand_fidx():
    rows = []
    rows += [0 * 16 + r2 for r2 in range(16)]
    for r1 in range(1, 8):
        n = PEER_TOPK // (r1 + 1)
        rows += [r1 * 16 + r2 if r2 < n else _FIDX_INVALID + r1 * 16 + r2 for r2 in range(8)]
    rows += [r1 * 16 for r1 in range(8, 16)]
    assert len(rows) == _CAND_ROWS
    return jnp.broadcast_to(jnp.asarray(rows, F32)[:, None], (_CAND_ROWS, LANES))


def _peer_topk(st, tmk):
    T = st.shape[-1]
    tmk = min(tmk, T)
    spec = pl.BlockSpec((PEER_HEADS, N_KEYS, tmk), lambda i: (0, 0, i))
    shp = jax.ShapeDtypeStruct((PEER_HEADS, N_KEYS, T), GDT)
    shp32 = jax.ShapeDtypeStruct((PEER_HEADS, N_KEYS, T), F32)
    return pl.pallas_call(
        functools.partial(_peer_topk_kernel, n_chunks=tmk // LANES),
        grid=(T // tmk,),
        in_specs=[pl.BlockSpec((2 * PEER_HEADS, N_KEYS, tmk), lambda i: (0, 0, i)),
                  pl.BlockSpec((_CAND_ROWS, LANES), lambda i: (0, 0))],
        out_specs=[spec] * 4,
        out_shape=[shp, shp, shp32, shp32],
        compiler_params=_cparams(("parallel",)),
        name="peer_topk",
    )(st, _cand_fidx())


def _peer_dense_kernel(x_ref, xnt_ref, r2_ref, e2_ref, cnt_ref, c_ref, u_ref, vt_ref, out_ref,
                       acc_ref, ht_ref, at_ref, *, eb, tc):
    e = pl.program_id(1)
    tm = xnt_ref.shape[1]

    @pl.when(e == 0)
    def _():
        acc_ref[...] = jnp.zeros_like(acc_ref)

    ht_ref[...] = _dot(u_ref[...], xnt_ref[...])
    for jb in range(eb // N_KEYS):
        j = e * (eb // N_KEYS) + jb
        rows = slice(jb * N_KEYS, (jb + 1) * N_KEYS)

        def chunk(ci, carry, j=j, rows=rows):
            cols = pl.ds(pl.multiple_of(ci * tc, tc), tc)
            g = jnp.zeros((N_KEYS, tc), GDT)
            for h in range(PEER_HEADS):
                cnt = cnt_ref[h, pl.ds(j, 1), cols].astype(GDT)
                coef = c_ref[h, pl.ds(j, 1), cols].astype(GDT)
                g = g + jnp.where(r2_ref[h, :, cols] < cnt, e2_ref[h, :, cols] * coef, 0)
            at_ref[rows, cols] = (g * jax.nn.gelu(ht_ref[rows, cols]).astype(GDT)).astype(at_ref.dtype)
            return carry

        lax.fori_loop(0, tm // tc, chunk, 0)
    acc_ref[...] += _dot(vt_ref[...], at_ref[...])

    @pl.when(e == pl.num_programs(1) - 1)
    def _():
        out_ref[...] = x_ref[...] + acc_ref[...].T


def _peer_dense(x2d, xnt, r2, e2, cnt, c, u, vt, tm, eb, tc):
    T = x2d.shape[0]
    tm = min(tm, T)
    tc = min(tc, tm)
    tok3 = pl.BlockSpec((PEER_HEADS, N_KEYS, tm), lambda i, e: (0, 0, i))
    return pl.pallas_call(
        functools.partial(_peer_dense_kernel, eb=eb, tc=tc),
        grid=(T // tm, N_EXPERTS // eb),
        in_specs=[pl.BlockSpec((tm, D_MODEL), lambda i, e: (i, 0)),
                  pl.BlockSpec((D_MODEL, tm), lambda i, e: (0, i)),
                  tok3, tok3, tok3, tok3,
                  pl.BlockSpec((eb, D_MODEL), lambda i, e: (e, 0)),
                  pl.BlockSpec((D_MODEL, eb), lambda i, e: (0, e))],
        out_specs=pl.BlockSpec((tm, D_MODEL), lambda i, e: (i, 0)),
        out_shape=jax.ShapeDtypeStruct((T, D_MODEL), F32),
        scratch_shapes=[pltpu.VMEM((D_MODEL, tm), F32),
                        pltpu.VMEM((eb, tm), F32),
                        pltpu.VMEM((eb, tm), CDT)],
        compiler_params=_cparams(("parallel", "arbitrary")),
        name="peer_dense",
    )(x2d, xnt, r2, e2, cnt, c, u, vt)


def _pad_heads(w, width):
    r = w.shape[0]
    w = w.reshape(r, MLA_HEADS, width)
    w = jnp.pad(w, ((0, 0), (0, 0), (0, HEAD_PAD - width)))
    return w.reshape(r, MLA_HEADS * HEAD_PAD)


def _layer_weights(l, norm1_g, w_in, q_lora_g, kv_lora_g, w_uq, w_ukv, q_head_g, k_head_g, w_o_mla,
                   pool_w, pool_scale, w_pool_out, gate_bias, w_out, norm2_g, peer_wq, peer_keys,
                   peer_u, peer_v):
    wi = w_in[l]
    c0 = Q_LORA
    c1 = c0 + KV_LORA
    c2 = c1 + QK_ROPE
    c3 = c2 + POOL_WIDTH
    zeros = lambda n: jnp.zeros((D_MODEL, n), wi.dtype)
    w_in_pad = jnp.concatenate([wi[:, :c1], zeros(QK_NOPE), wi[:, c1:c2], zeros(LANES - QK_HEAD),
                                wi[:, c2:]], axis=1)
    wkv = w_ukv[l].reshape(KV_LORA, MLA_HEADS, QK_NOPE + V_HEAD)
    w_uk = _pad_heads(wkv[:, :, :QK_NOPE].reshape(KV_LORA, MLA_HEADS * QK_NOPE), QK_NOPE)
    w_uv = wkv[:, :, QK_NOPE:].reshape(KV_LORA, MLA_WIDTH)
    pad_g = lambda g: jnp.pad(g, (0, LANES - QK_HEAD)).reshape(1, LANES)
    return {
        "g1": norm1_g[l].reshape(1, D_MODEL),
        "w_in": w_in_pad.astype(CDT),
        "qlg": q_lora_g[l].reshape(1, Q_LORA),
        "kvlg": kv_lora_g[l].reshape(1, KV_LORA),
        "w_uq": _pad_heads(w_uq[l], QK_HEAD).astype(CDT),
        "w_uk": w_uk.astype(CDT),
        "w_uv": w_uv.astype(CDT),
        "qg": pad_g(q_head_g[l] * (1.0 / math.sqrt(QK_HEAD))),
        "kg": pad_g(k_head_g[l]),
        "gate_bias": gate_bias[l].reshape(1, 2 * D_MODEL),
        "w_o_mla": w_o_mla[l].astype(CDT),
        "pool_w": pool_w[l].astype(CDT),
        "pool_scale": pool_scale[l].reshape(1, POOL_WIDTH),
        "w_pool_out": w_pool_out[l].astype(CDT),
        "w_out": w_out[l].astype(CDT),
        "g2": norm2_g[l].reshape(1, D_MODEL),
        "peer_wq": peer_wq[l].astype(CDT),
        "peer_keys": peer_keys[l].reshape(2 * PEER_HEADS, N_KEYS, PEER_HALF).astype(CDT),
        "peer_u": peer_u[l].astype(CDT),
        "peer_vt": peer_v[l].T.astype(CDT),
    }


def kernel(x, positions, norm1_g, w_in, q_lora_g, kv_lora_g, w_uq, w_ukv, q_head_g, k_head_g, w_o_mla,
           pool_w, pool_scale, w_pool_out, gate_bias, w_out, norm2_g, peer_wq, peer_keys, peer_u, peer_v):
    B, S, D = x.shape
    assert D == D_MODEL and S % LANES == 0
    T = B * S
    depth = norm1_g.shape[0]
    ropes = _rope_tables(positions)
    x2d = x.reshape(T, D)
    for l in range(depth):
        lw = _layer_weights(l, norm1_g, w_in, q_lora_g, kv_lora_g, w_uq, w_ukv, q_head_g, k_head_g,
                            w_o_mla, pool_w, pool_scale, w_pool_out, gate_bias, w_out, norm2_g,
                            peer_wq, peer_keys, peer_u, peer_v)
        q, k, v, p, gates = _in_proj(x2d, lw, ropes, tm=256)
        o = _attention(q, k, v, B, S, tq=256)
        y = _pool(p, lw, B, S)
        x2d, xnt, st = _out_proj(x2d, o, y, gates, lw, tm=256)
        r2, e2, cnt, c = _peer_topk(st, tmk=512)
        x2d = _peer_dense(x2d, xnt, r2, e2, cnt, c, lw["peer_u"], lw["peer_vt"], tm=512, eb=512, tc=256)
    return x2d.reshape(B, S, D)
```

```python
import functools
import math

import jax
import jax.numpy as jnp
import numpy as np
from jax import lax
from jax.experimental import pallas as pl
from jax.experimental.pallas import tpu as pltpu

D_MODEL = 1024
MLA_HEADS = 8
Q_LORA = 384
KV_LORA = 256
QK_NOPE = 64
QK_ROPE = 32
QK_HEAD = QK_NOPE + QK_ROPE
V_HEAD = 64
MLA_WIDTH = MLA_HEADS * V_HEAD
ROPE_BASE = 10000.0
POOL_WINDOWS = (2, 4, 8, 16)
POOL_WIDTH = 512
POOL_GC = 128
PEER_HEADS = 8
N_KEYS = 128
N_EXPERTS = N_KEYS * N_KEYS
PEER_HALF = 128
PEER_TOPK = 16
RMS_EPS = 1e-6

LANES = 128
GATE_TILE_ROWS = 16
HEAD_PAD = LANES
IN_PAD = Q_LORA + KV_LORA + LANES + POOL_WIDTH + 2 * D_MODEL
VMEM_LIMIT = 48 * 1024 * 1024

CDT = jnp.bfloat16
GDT = jnp.bfloat16
F32 = jnp.float32
NEG_INF = float("-inf")
_FLT_MAX = float(np.finfo(np.float32).max)
_FLT_TOP_ULP = 2.0 ** 104

_CAND_ROWS = 16 + 8 * 7 + 8
_FIDX_INVALID = 1 << 20


def _cparams(sem, vmem=VMEM_LIMIT, flags=None):
    return pltpu.CompilerParams(dimension_semantics=sem, vmem_limit_bytes=vmem, flags=flags)


def _rms(x, g):
    return x * lax.rsqrt(jnp.mean(x * x, axis=-1, keepdims=True) + RMS_EPS) * g


def _dot(a, b):
    return jnp.dot(a, b, preferred_element_type=F32)


def _dot_nt(a, b):
    return lax.dot_general(a, b, (((1,), (1,)), ((), ())), preferred_element_type=F32)


def _rope_kernel(pos_ref, inv_ref, c_ref, s1_ref, s2_ref):
    ang = pos_ref[...] * inv_ref[...]
    lane = lax.broadcasted_iota(jnp.int32, ang.shape, 1)
    cos = jnp.cos(ang)
    sin = jnp.sin(ang)
    half = QK_ROPE // 2
    c_ref[...] = jnp.where(lane < QK_NOPE, 1.0, jnp.where(lane < QK_HEAD, cos, 0.0))
    s1_ref[...] = jnp.where((lane >= QK_NOPE) & (lane < QK_NOPE + half), -sin, 0.0)
    s2_ref[...] = jnp.where((lane >= QK_NOPE + half) & (lane < QK_HEAD), sin, 0.0)


def _rope_tables(positions):
    T = positions.size
    tm = min(T, 2048)
    pos = jnp.broadcast_to(positions.reshape(T, 1).astype(F32), (T, LANES))
    inv = ROPE_BASE ** (-jnp.arange(0, QK_ROPE, 2, dtype=F32) / QK_ROPE)
    inv_pat = jnp.concatenate([jnp.zeros((QK_NOPE,), F32), inv, inv,
                               jnp.zeros((LANES - QK_HEAD,), F32)]).reshape(1, LANES)
    spec = pl.BlockSpec((tm, LANES), lambda i: (i, 0))
    return pl.pallas_call(
        _rope_kernel,
        grid=(T // tm,),
        in_specs=[spec, pl.BlockSpec((1, LANES), lambda i: (0, 0))],
        out_specs=[spec, spec, spec],
        out_shape=[jax.ShapeDtypeStruct((T, LANES), F32)] * 3,
        compiler_params=_cparams(("parallel",)),
        name="rope_tables",
    )(pos, inv_pat)


def _in_proj_kernel(x_ref, g1_ref, win_ref, qlg_ref, kvlg_ref, wuq_ref, wuk_ref, wuv_ref,
                    c_ref, s1_ref, s2_ref, qg_ref, kg_ref, bias_ref,
                    q_out, k_out, v_out, p_out, gate_out):
    xn = _rms(x_ref[...], g1_ref[...])
    y = _dot(xn.astype(CDT), win_ref[...])
    o = 0
    cq = y[:, o:o + Q_LORA]; o += Q_LORA
    ckv = y[:, o:o + KV_LORA]; o += KV_LORA
    kpe = y[:, o:o + LANES]; o += LANES
    p_out[...] = y[:, o:o + POOL_WIDTH]; o += POOL_WIDTH
    gate_out[...] = jax.nn.sigmoid(y[:, o:] + bias_ref[...]).astype(gate_out.dtype)

    cqn = _rms(cq, qlg_ref[...]).astype(CDT)
    ckvn = _rms(ckv, kvlg_ref[...]).astype(CDT)
    q_raw = _dot(cqn, wuq_ref[...])
    k_raw = _dot(ckvn, wuk_ref[...])
    v_out[...] = _dot(ckvn, wuv_ref[...]).astype(v_out.dtype)

    cpat = c_ref[...]
    s1pat = s1_ref[...]
    s2pat = s2_ref[...]
    half = QK_ROPE // 2

    def rope(xh):
        return (xh * cpat + pltpu.roll(xh, LANES - half, 1) * s1pat
                + pltpu.roll(xh, half, 1) * s2pat)

    def head_norm(xh, g):
        ms = jnp.sum(xh * xh, axis=-1, keepdims=True) * (1.0 / QK_HEAD)
        return xh * lax.rsqrt(ms + RMS_EPS) * g

    kpe_r = rope(kpe)
    qg = qg_ref[...]
    kg = kg_ref[...]
    for h in range(MLA_HEADS):
        sl = slice(h * HEAD_PAD, (h + 1) * HEAD_PAD)
        q_out[:, sl] = head_norm(rope(q_raw[:, sl]), qg).astype(q_out.dtype)
        k_out[:, sl] = head_norm(k_raw[:, sl] + kpe_r, kg).astype(k_out.dtype)


def _in_proj(x2d, lw, ropes, tm):
    T = x2d.shape[0]
    tm = min(tm, T)
    row = lambda w: pl.BlockSpec((tm, w), lambda i: (i, 0))
    full = lambda a: pl.BlockSpec(a.shape, lambda i: (0,) * a.ndim)
    consts = [lw["g1"], lw["w_in"], lw["qlg"], lw["kvlg"], lw["w_uq"], lw["w_uk"], lw["w_uv"]]
    tail = [lw["qg"], lw["kg"], lw["gate_bias"]]
    return pl.pallas_call(
        _in_proj_kernel,
        grid=(T // tm,),
        in_specs=[row(D_MODEL)] + [full(a) for a in consts] + [row(LANES)] * 3 + [full(a) for a in tail],
        out_specs=[row(MLA_HEADS * HEAD_PAD), row(MLA_HEADS * HEAD_PAD), row(MLA_WIDTH),
                   row(POOL_WIDTH), row(2 * D_MODEL)],
        out_shape=[jax.ShapeDtypeStruct((T, MLA_HEADS * HEAD_PAD), CDT),
                   jax.ShapeDtypeStruct((T, MLA_HEADS * HEAD_PAD), CDT),
                   jax.ShapeDtypeStruct((T, MLA_WIDTH), CDT),
                   jax.ShapeDtypeStruct((T, POOL_WIDTH), F32),
                   jax.ShapeDtypeStruct((T, 2 * D_MODEL), CDT)],
        compiler_params=_cparams(("parallel",)),
        name="in_proj",
    )(x2d, *consts, *ropes, *tail)


def _attn_kernel(q_ref, k_ref, v_ref, o_ref):
    outs = []
    for j in range(2):
        q = q_ref[:, j * HEAD_PAD:(j + 1) * HEAD_PAD]
        k = k_ref[:, j * HEAD_PAD:(j + 1) * HEAD_PAD]
        v = v_ref[:, j * V_HEAD:(j + 1) * V_HEAD]
        s = _dot_nt(q, k)
        m = jnp.max(s, axis=-1, keepdims=True)
        p = jnp.exp(s - m)
        l = jnp.sum(p, axis=-1, keepdims=True)
        outs.append(_dot(p.astype(CDT), v) / l)
    o_ref[...] = jnp.concatenate(outs, axis=-1).astype(o_ref.dtype)


def _attention(q, k, v, B, S, tq):
    tq = min(tq, S)
    nq = S // tq
    return pl.pallas_call(
        _attn_kernel,
        grid=(B, MLA_HEADS // 2, nq),
        in_specs=[pl.BlockSpec((tq, 2 * HEAD_PAD), lambda b, h, i: (b * nq + i, h)),
                  pl.BlockSpec((S, 2 * HEAD_PAD), lambda b, h, i: (b, h)),
                  pl.BlockSpec((S, 2 * V_HEAD), lambda b, h, i: (b, h))],
        out_specs=pl.BlockSpec((tq, 2 * V_HEAD), lambda b, h, i: (b * nq + i, h)),
        out_shape=jax.ShapeDtypeStruct((B * S, MLA_WIDTH), CDT),
        compiler_params=_cparams(("parallel", "parallel", "parallel")),
        name="attention",
    )(q, k, v)


def _pool_kernel(p_ref, pw_ref, ps_ref, y_ref):
    S = p_ref.shape[0]
    t = lax.broadcasted_iota(jnp.int32, (S, POOL_GC), 0)
    for g, w in enumerate(POOL_WINDOWS):
        half = w // 2
        sl = slice(g * POOL_GC, (g + 1) * POOL_GC)
        pg = p_ref[:, sl]
        acc = pg
        for d in range(-half, half):
            if d == 0:
                continue
            shifted = pltpu.roll(pg, (-d) % S, 0)
            valid = (t + d >= 0) & (t + d < S)
            acc = acc + jnp.where(valid, shifted, 0.0)
        count = (jnp.minimum(t + half, S) - jnp.maximum(t - half, 0)).astype(F32)
        mixed = acc / count - pg
        yg = _dot(mixed.astype(CDT), pw_ref[g]) * ps_ref[:, sl]
        y_ref[:, sl] = yg.astype(y_ref.dtype)


def _pool(p, lw, B, S):
    return pl.pallas_call(
        _pool_kernel,
        grid=(B,),
        in_specs=[pl.BlockSpec((S, POOL_WIDTH), lambda b: (b, 0)),
                  pl.BlockSpec(lw["pool_w"].shape, lambda b: (0, 0, 0)),
                  pl.BlockSpec((1, POOL_WIDTH), lambda b: (0, 0))],
        out_specs=pl.BlockSpec((S, POOL_WIDTH), lambda b: (b, 0)),
        out_shape=jax.ShapeDtypeStruct((B * S, POOL_WIDTH), CDT),
        compiler_params=_cparams(("parallel",)),
        name="pool",
    )(p, lw["pool_w"], lw["pool_scale"])


def _out_proj_kernel(x_ref, o_ref, y_ref, gate_ref, wo_ref, wpo_ref, wout_ref, g2_ref, wq_ref, keys_ref,
                     xnew_out, xnt_out, st_out):
    a = _dot(o_ref[...], wo_ref[...])
    bp = _dot(y_ref[...], wpo_ref[...])
    ga = gate_ref[:, :D_MODEL].astype(F32)
    gb = gate_ref[:, D_MODEL:].astype(F32)
    mix = ga * a + gb * bp
    xnew = x_ref[...] + _dot(mix.astype(CDT), wout_ref[...])
    xnew_out[...] = xnew
    xn = _rms(xnew, g2_ref[...])
    xnt_out[...] = xn.T.astype(xnt_out.dtype)
    qp = _dot(xn.astype(CDT), wq_ref[...])
    for hp in range(2 * PEER_HEADS):
        qh = qp[:, hp * PEER_HALF:(hp + 1) * PEER_HALF].astype(CDT)
        st_out[hp] = _dot_nt(keys_ref[hp], qh)


def _out_proj(x2d, o, y, gates, lw, tm):
    T = x2d.shape[0]
    tm = min(tm, T)
    row = lambda w: pl.BlockSpec((tm, w), lambda i: (i, 0))
    full = lambda a: pl.BlockSpec(a.shape, lambda i: (0,) * a.ndim)
    consts = [lw["w_o_mla"], lw["w_pool_out"], lw["w_out"], lw["g2"], lw["peer_wq"], lw["peer_keys"]]
    return pl.pallas_call(
        _out_proj_kernel,
        grid=(T // tm,),
        in_specs=[row(D_MODEL), row(MLA_WIDTH), row(POOL_WIDTH), row(2 * D_MODEL)] + [full(a) for a in consts],
        out_specs=[row(D_MODEL),
                   pl.BlockSpec((D_MODEL, tm), lambda i: (0, i)),
                   pl.BlockSpec((2 * PEER_HEADS, N_KEYS, tm), lambda i: (0, 0, i))],
        out_shape=[jax.ShapeDtypeStruct((T, D_MODEL), F32),
                   jax.ShapeDtypeStruct((D_MODEL, T), CDT),
                   jax.ShapeDtypeStruct((2 * PEER_HEADS, N_KEYS, T), F32)],
        compiler_params=_cparams(("parallel",)),
        name="out_proj",
    )(x2d, o, y, gates, *consts)


def _top16(s, key_iota, row16, exact):
    work = s
    rank = jnp.full(s.shape, float(PEER_TOPK), F32)
    tops = jnp.zeros((PEER_TOPK, s.shape[1]), F32)
    for r in range(PEER_TOPK):
        m = jnp.max(work, axis=0, keepdims=True)
        sel = work == m
        if exact:
            first = jnp.min(jnp.where(sel, key_iota, float(N_KEYS)), axis=0, keepdims=True)
            sel = key_iota == first
            rank = jnp.where(sel, float(r), rank)
            work = jnp.where(sel, NEG_INF, work)
        else:
            work = jnp.where(sel, -_FLT_MAX + r * _FLT_TOP_ULP, work)
        tops = jnp.where(row16 == r, m, tops)
    if not exact:
        rank = jnp.minimum((work + _FLT_MAX) * (1.0 / _FLT_TOP_ULP), float(PEER_TOPK))
    return rank, tops


def _cand_grid(ta, tb, combine):
    pieces = [combine(ta[0:1], tb)]
    for r1 in range(1, 8):
        pieces.append(combine(ta[r1:r1 + 1], tb[0:8]))
    pieces.append(combine(ta[8:16], tb[0:1]))
    return jnp.concatenate(pieces, axis=0)


def _count_true(mask):
    return jnp.sum(jnp.where(mask, 1.0, 0.0), axis=0, keepdims=True)


def _route(s1, s2, key_iota, row16, fidx, exact):
    rank1, ta = _top16(s1, key_iota, row16, exact)
    rank2, tb = _top16(s2, key_iota, row16, exact)
    a0 = ta[0:1]
    b0 = tb[0:1]
    cand0 = jnp.where(fidx < _FIDX_INVALID, _cand_grid(ta, tb, lambda a, b: a + b), NEG_INF)
    ecand = _cand_grid(jnp.exp(ta - a0), jnp.exp(tb - b0), lambda a, b: a * b)
    cand = cand0
    if exact:
        selected = jnp.zeros(cand.shape, F32)
        for _ in range(PEER_TOPK):
            m = jnp.max(cand, axis=0, keepdims=True)
            first = jnp.min(jnp.where(cand == m, fidx, float(_FIDX_INVALID)), axis=0, keepdims=True)
            sel = fidx == first
            selected = jnp.where(sel, 1.0, selected)
            cand = jnp.where(sel, NEG_INF, cand)
        bad = jnp.zeros_like(a0)
    else:
        for _ in range(PEER_TOPK):
            m = jnp.max(cand, axis=0, keepdims=True)
            cand = jnp.where(cand == m, NEG_INF, cand)
        selected = jnp.where(cand0 >= m, 1.0, 0.0)
        k = float(PEER_TOPK)
        bad = jnp.where((_count_true(rank1 < k) != k) | (_count_true(rank2 < k) != k)
                        | (jnp.sum(selected, axis=0, keepdims=True) != k), 1.0, 0.0)
    z = jnp.sum(selected * ecand, axis=0, keepdims=True)
    cntr = [jnp.sum(selected[0:16], axis=0, keepdims=True)]
    for r1 in range(1, 8):
        cntr.append(jnp.sum(selected[8 + 8 * r1:16 + 8 * r1], axis=0, keepdims=True))
    for r1 in range(8, 16):
        cntr.append(selected[64 + r1:65 + r1])
    cnt = jnp.zeros(s1.shape, F32)
    for r1 in range(PEER_TOPK):
        cnt = jnp.where(rank1 == float(r1), cntr[r1], cnt)
    return rank2, jnp.exp(s2 - b0), cnt, jnp.exp(s1 - a0) / z, bad


def _peer_topk_kernel(s_ref, fidx_ref, r2_out, e2_out, cnt_out, c_out, *, n_chunks):
    key_iota = lax.broadcasted_iota(jnp.int32, (N_KEYS, LANES), 0).astype(F32)
    row16 = lax.broadcasted_iota(jnp.int32, (PEER_TOPK, LANES), 0)
    fidx = fidx_ref[...]

    def body(it, carry):
        h = it // n_chunks
        off = pl.multiple_of((it % n_chunks) * LANES, LANES)
        s1 = s_ref[2 * h, :, pl.ds(off, LANES)]
        s2 = s_ref[2 * h + 1, :, pl.ds(off, LANES)]

        def store(r2, e2, cnt, c):
            r2_out[h, :, pl.ds(off, LANES)] = r2.astype(r2_out.dtype)
            e2_out[h, :, pl.ds(off, LANES)] = e2.astype(e2_out.dtype)
            cnt_out[h, :, pl.ds(off, LANES)] = cnt.astype(cnt_out.dtype)
            c_out[h, :, pl.ds(off, LANES)] = c.astype(c_out.dtype)

        *fast, bad = _route(s1, s2, key_iota, row16, fidx, exact=False)
        store(*fast)

        @pl.when(jnp.max(bad) > 0.0)
        def _():
            *slow, _ = _route(s1, s2, key_iota, row16, fidx, exact=True)
            store(*slow)

        return carry

    lax.fori_loop(0, PEER_HEADS * n_chunks, body, 0)


def _cand_fidx():
    rows = []
    rows += [0 * 16 + r2 for r2 in range(16)]
    for r1 in range(1, 8):
        n = PEER_TOPK // (r1 + 1)
        rows += [r1 * 16 + r2 if r2 < n else _FIDX_INVALID + r1 * 16 + r2 for r2 in range(8)]
    rows += [r1 * 16 for r1 in range(8, 16)]
    assert len(rows) == _CAND_ROWS
    return jnp.broadcast_to(jnp.asarray(rows, F32)[:, None], (_CAND_ROWS, LANES))


def _peer_topk(st, tmk):
    T = st.shape[-1]
    tmk = min(tmk, T)
    spec = pl.BlockSpec((PEER_HEADS, N_KEYS, tmk), lambda i: (0, 0, i))
    shp = jax.ShapeDtypeStruct((PEER_HEADS, N_KEYS, T), GDT)
    shp32 = jax.ShapeDtypeStruct((PEER_HEADS, N_KEYS, T), F32)
    return pl.pallas_call(
        functools.partial(_peer_topk_kernel, n_chunks=tmk // LANES),
        grid=(T // tmk,),
        in_specs=[pl.BlockSpec((2 * PEER_HEADS, N_KEYS, tmk), lambda i: (0, 0, i)),
                  pl.BlockSpec((_CAND_ROWS, LANES), lambda i: (0, 0))],
        out_specs=[spec] * 4,
        out_shape=[shp, shp, shp32, shp32],
        compiler_params=_cparams(("parallel",)),
        name="peer_topk",
    )(st, _cand_fidx())


def _peer_dense_kernel(x_ref, xnt_ref, r2_ref, e2_ref, cnt_ref, c_ref, u_ref, vt_ref, out_ref,
                       acc_ref, ht_ref, at_ref, bc_ref, *, eb, tc, nsub):
    e = pl.program_id(1)
    tm = xnt_ref.shape[1]
    jpb = eb // N_KEYS
    sub = eb // nsub
    pk = bc_ref.shape[-2]

    @pl.when(e == 0)
    def _():
        acc_ref[...] = jnp.zeros_like(acc_ref)

    def fill(h, carry):
        for jb in range(jpb):
            j = e * jpb + jb
            bc_ref[0, h, jb] = jnp.broadcast_to(cnt_ref[h, pl.ds(j, 1), :], (pk, tm)).astype(GDT)
            bc_ref[1, h, jb] = jnp.broadcast_to(c_ref[h, pl.ds(j, 1), :], (pk, tm)).astype(GDT)
        return carry

    lax.fori_loop(0, PEER_HEADS, fill, 0)

    for sb in range(nsub):
        rows = slice(sb * sub, (sb + 1) * sub)
        ht_ref[rows, :] = _dot(u_ref[rows, :], xnt_ref[...])
    for sb in range(nsub):
        rows = slice(sb * sub, (sb + 1) * sub)
        for jb in range(sb * sub // N_KEYS, (sb + 1) * sub // N_KEYS):
            krows = slice(jb * N_KEYS, (jb + 1) * N_KEYS)
            for ci in range(tm // tc):
                cols = slice(ci * tc, (ci + 1) * tc)
                g = jnp.zeros((N_KEYS // pk, pk, tc), GDT)
                for h in range(PEER_HEADS):
                    r2 = r2_ref[h, :, cols].reshape(N_KEYS // pk, pk, tc)
                    e2 = e2_ref[h, :, cols].reshape(N_KEYS // pk, pk, tc)
                    g = g + jnp.where(r2 < bc_ref[0, h, jb, :, cols], e2 * bc_ref[1, h, jb, :, cols], 0)
                a = g.reshape(N_KEYS, tc) * jax.nn.gelu(ht_ref[krows, cols]).astype(GDT)
                at_ref[krows, cols] = a.astype(at_ref.dtype)
        acc_ref[...] += _dot(vt_ref[:, rows], at_ref[rows, :])

    @pl.when(e == pl.num_programs(1) - 1)
    def _():
        out_ref[...] = x_ref[...] + acc_ref[...].T


def _peer_dense(x2d, xnt, r2, e2, cnt, c, u, vt, tm, eb, tc):
    T = x2d.shape[0]
    tm = min(tm, T)
    tc = min(tc, tm)
    tok3 = pl.BlockSpec((PEER_HEADS, N_KEYS, tm), lambda i, e: (0, 0, i))
    return pl.pallas_call(
        functools.partial(_peer_dense_kernel, eb=eb, tc=tc, nsub=2),
        grid=(T // tm, N_EXPERTS // eb),
        in_specs=[pl.BlockSpec((tm, D_MODEL), lambda i, e: (i, 0)),
                  pl.BlockSpec((D_MODEL, tm), lambda i, e: (0, i)),
                  tok3, tok3, tok3, tok3,
                  pl.BlockSpec((eb, D_MODEL), lambda i, e: (e, 0)),
                  pl.BlockSpec((D_MODEL, eb), lambda i, e: (0, e))],
        out_specs=pl.BlockSpec((tm, D_MODEL), lambda i, e: (i, 0)),
        out_shape=jax.ShapeDtypeStruct((T, D_MODEL), F32),
        scratch_shapes=[pltpu.VMEM((D_MODEL, tm), F32),
                        pltpu.VMEM((eb, tm), F32),
                        pltpu.VMEM((eb, tm), CDT),
                        pltpu.VMEM((2, PEER_HEADS, eb // N_KEYS, GATE_TILE_ROWS, tm), GDT)],
        compiler_params=_cparams(("parallel", "arbitrary")),
        name="peer_dense",
    )(x2d, xnt, r2, e2, cnt, c, u, vt)


def _pad_heads(w, width):
    r = w.shape[0]
    w = w.reshape(r, MLA_HEADS, width)
    w = jnp.pad(w, ((0, 0), (0, 0), (0, HEAD_PAD - width)))
    return w.reshape(r, MLA_HEADS * HEAD_PAD)


def _layer_weights(l, norm1_g, w_in, q_lora_g, kv_lora_g, w_uq, w_ukv, q_head_g, k_head_g, w_o_mla,
                   pool_w, pool_scale, w_pool_out, gate_bias, w_out, norm2_g, peer_wq, peer_keys,
                   peer_u, peer_v):
    wi = w_in[l]
    c0 = Q_LORA
    c1 = c0 + KV_LORA
    c2 = c1 + QK_ROPE
    c3 = c2 + POOL_WIDTH
    zeros = lambda n: jnp.zeros((D_MODEL, n), wi.dtype)
    w_in_pad = jnp.concatenate([wi[:, :c1], zeros(QK_NOPE), wi[:, c1:c2], zeros(LANES - QK_HEAD),
                                wi[:, c2:]], axis=1)
    wkv = w_ukv[l].reshape(KV_LORA, MLA_HEADS, QK_NOPE + V_HEAD)
    w_uk = _pad_heads(wkv[:, :, :QK_NOPE].reshape(KV_LORA, MLA_HEADS * QK_NOPE), QK_NOPE)
    w_uv = wkv[:, :, QK_NOPE:].reshape(KV_LORA, MLA_WIDTH)
    pad_g = lambda g: jnp.pad(g, (0, LANES - QK_HEAD)).reshape(1, LANES)
    return {
        "g1": norm1_g[l].reshape(1, D_MODEL),
        "w_in": w_in_pad.astype(CDT),
        "qlg": q_lora_g[l].reshape(1, Q_LORA),
        "kvlg": kv_lora_g[l].reshape(1, KV_LORA),
        "w_uq": _pad_heads(w_uq[l], QK_HEAD).astype(CDT),
        "w_uk": w_uk.astype(CDT),
        "w_uv": w_uv.astype(CDT),
        "qg": pad_g(q_head_g[l] * (1.0 / math.sqrt(QK_HEAD))),
        "kg": pad_g(k_head_g[l]),
        "gate_bias": gate_bias[l].reshape(1, 2 * D_MODEL),
        "w_o_mla": w_o_mla[l].astype(CDT),
        "pool_w": pool_w[l].astype(CDT),
        "pool_scale": pool_scale[l].reshape(1, POOL_WIDTH),
        "w_pool_out": w_pool_out[l].astype(CDT),
        "w_out": w_out[l].astype(CDT),
        "g2": norm2_g[l].reshape(1, D_MODEL),
        "peer_wq": peer_wq[l].astype(CDT),
        "peer_keys": peer_keys[l].reshape(2 * PEER_HEADS, N_KEYS, PEER_HALF).astype(CDT),
        "peer_u": peer_u[l].astype(CDT),
        "peer_vt": peer_v[l].T.astype(CDT),
    }


def kernel(x, positions, norm1_g, w_in, q_lora_g, kv_lora_g, w_uq, w_ukv, q_head_g, k_head_g, w_o_mla,
           pool_w, pool_scale, w_pool_out, gate_bias, w_out, norm2_g, peer_wq, peer_keys, peer_u, peer_v):
    B, S, D = x.shape
    assert D == D_MODEL and S % LANES == 0
    T = B * S
    depth = norm1_g.shape[0]
    ropes = _rope_tables(positions)
    x2d = x.reshape(T, D)
    for l in range(depth):
        lw = _layer_weights(l, norm1_g, w_in, q_lora_g, kv_lora_g, w_uq, w_ukv, q_head_g, k_head_g,
                            w_o_mla, pool_w, pool_scale, w_pool_out, gate_bias, w_out, norm2_g,
                            peer_wq, peer_keys, peer_u, peer_v)
        q, k, v, p, gates = _in_proj(x2d, lw, ropes, tm=256)
        o = _attention(q, k, v, B, S, tq=256)
        y = _pool(p, lw, B, S)
        x2d, xnt, st = _out_proj(x2d, o, y, gates, lw, tm=256)
        r2, e2, cnt, c = _peer_topk(st, tmk=512)
        x2d = _peer_dense(x2d, xnt, r2, e2, cnt, c, lw["peer_u"], lw["peer_vt"], tm=512, eb=512, tc=256)
    return x2d.reshape(B, S, D)
```

```python
import functools
import math

import jax
import jax.numpy as jnp
import numpy as np
from jax import lax
from jax.experimental import pallas as pl
from jax.experimental.pallas import tpu as pltpu

D_MODEL = 1024
MLA_HEADS = 8
Q_LORA = 384
KV_LORA = 256
QK_NOPE = 64
QK_ROPE = 32
QK_HEAD = QK_NOPE + QK_ROPE
V_HEAD = 64
MLA_WIDTH = MLA_HEADS * V_HEAD
ROPE_BASE = 10000.0
POOL_WINDOWS = (2, 4, 8, 16)
POOL_WIDTH = 512
POOL_GC = 128
PEER_HEADS = 8
N_KEYS = 128
N_EXPERTS = N_KEYS * N_KEYS
PEER_HALF = 128
PEER_TOPK = 16
RMS_EPS = 1e-6

LANES = 128
GATE_TILE_ROWS = 16
HEAD_PAD = LANES
IN_PAD = Q_LORA + KV_LORA + LANES + POOL_WIDTH + 2 * D_MODEL
VMEM_LIMIT = 48 * 1024 * 1024

CDT = jnp.bfloat16
GDT = jnp.bfloat16
F32 = jnp.float32
NEG_INF = float("-inf")
_FLT_MAX = float(np.finfo(np.float32).max)
_FLT_TOP_ULP = 2.0 ** 104

_CAND_ROWS = 16 + 8 * 7 + 8
_FIDX_INVALID = 1 << 20


def _cparams(sem, vmem=VMEM_LIMIT, flags=None):
    return pltpu.CompilerParams(dimension_semantics=sem, vmem_limit_bytes=vmem, flags=flags)


def _rms(x, g):
    return x * lax.rsqrt(jnp.mean(x * x, axis=-1, keepdims=True) + RMS_EPS) * g


def _dot(a, b):
    return jnp.dot(a, b, preferred_element_type=F32)


def _dot_nt(a, b):
    return lax.dot_general(a, b, (((1,), (1,)), ((), ())), preferred_element_type=F32)


def _rope_kernel(pos_ref, inv_ref, c_ref, s1_ref, s2_ref):
    ang = pos_ref[...] * inv_ref[...]
    lane = lax.broadcasted_iota(jnp.int32, ang.shape, 1)
    cos = jnp.cos(ang)
    sin = jnp.sin(ang)
    half = QK_ROPE // 2
    c_ref[...] = jnp.where(lane < QK_NOPE, 1.0, jnp.where(lane < QK_HEAD, cos, 0.0))
    s1_ref[...] = jnp.where((lane >= QK_NOPE) & (lane < QK_NOPE + half), -sin, 0.0)
    s2_ref[...] = jnp.where((lane >= QK_NOPE + half) & (lane < QK_HEAD), sin, 0.0)


def _rope_tables(positions):
    T = positions.size
    tm = min(T, 2048)
    pos = jnp.broadcast_to(positions.reshape(T, 1).astype(F32), (T, LANES))
    inv = ROPE_BASE ** (-jnp.arange(0, QK_ROPE, 2, dtype=F32) / QK_ROPE)
    inv_pat = jnp.concatenate([jnp.zeros((QK_NOPE,), F32), inv, inv,
                               jnp.zeros((LANES - QK_HEAD,), F32)]).reshape(1, LANES)
    spec = pl.BlockSpec((tm, LANES), lambda i: (i, 0))
    return pl.pallas_call(
        _rope_kernel,
        grid=(T // tm,),
        in_specs=[spec, pl.BlockSpec((1, LANES), lambda i: (0, 0))],
        out_specs=[spec, spec, spec],
        out_shape=[jax.ShapeDtypeStruct((T, LANES), F32)] * 3,
        compiler_params=_cparams(("parallel",)),
        name="rope_tables",
    )(pos, inv_pat)


def _in_proj_kernel(x_ref, g1_ref, win_ref, qlg_ref, kvlg_ref, wuq_ref, wuk_ref, wuv_ref,
                    c_ref, s1_ref, s2_ref, qg_ref, kg_ref, bias_ref,
                    q_out, k_out, v_out, p_out, gate_out):
    xn = _rms(x_ref[...], g1_ref[...])
    y = _dot(xn.astype(CDT), win_ref[...])
    o = 0
    cq = y[:, o:o + Q_LORA]; o += Q_LORA
    ckv = y[:, o:o + KV_LORA]; o += KV_LORA
    kpe = y[:, o:o + LANES]; o += LANES
    p_out[...] = y[:, o:o + POOL_WIDTH]; o += POOL_WIDTH
    gate_out[...] = jax.nn.sigmoid(y[:, o:] + bias_ref[...]).astype(gate_out.dtype)

    cqn = _rms(cq, qlg_ref[...]).astype(CDT)
    ckvn = _rms(ckv, kvlg_ref[...]).astype(CDT)
    q_raw = _dot(cqn, wuq_ref[...])
    k_raw = _dot(ckvn, wuk_ref[...])
    v_out[...] = _dot(ckvn, wuv_ref[...]).astype(v_out.dtype)

    cpat = c_ref[...]
    s1pat = s1_ref[...]
    s2pat = s2_ref[...]
    half = QK_ROPE // 2

    def rope(xh):
        return (xh * cpat + pltpu.roll(xh, LANES - half, 1) * s1pat
                + pltpu.roll(xh, half, 1) * s2pat)

    def head_norm(xh, g):
        ms = jnp.sum(xh * xh, axis=-1, keepdims=True) * (1.0 / QK_HEAD)
        return xh * lax.rsqrt(ms + RMS_EPS) * g

    kpe_r = rope(kpe)
    qg = qg_ref[...]
    kg = kg_ref[...]
    for h in range(MLA_HEADS):
        sl = slice(h * HEAD_PAD, (h + 1) * HEAD_PAD)
        q_out[:, sl] = head_norm(rope(q_raw[:, sl]), qg).astype(q_out.dtype)
        k_out[:, sl] = head_norm(k_raw[:, sl] + kpe_r, kg).astype(k_out.dtype)


def _in_proj(x2d, lw, ropes, tm):
    T = x2d.shape[0]
    tm = min(tm, T)
    row = lambda w: pl.BlockSpec((tm, w), lambda i: (i, 0))
    full = lambda a: pl.BlockSpec(a.shape, lambda i: (0,) * a.ndim)
    consts = [lw["g1"], lw["w_in"], lw["qlg"], lw["kvlg"], lw["w_uq"], lw["w_uk"], lw["w_uv"]]
    tail = [lw["qg"], lw["kg"], lw["gate_bias"]]
    return pl.pallas_call(
        _in_proj_kernel,
        grid=(T // tm,),
        in_specs=[row(D_MODEL)] + [full(a) for a in consts] + [row(LANES)] * 3 + [full(a) for a in tail],
        out_specs=[row(MLA_HEADS * HEAD_PAD), row(MLA_HEADS * HEAD_PAD), row(MLA_WIDTH),
                   row(POOL_WIDTH), row(2 * D_MODEL)],
        out_shape=[jax.ShapeDtypeStruct((T, MLA_HEADS * HEAD_PAD), CDT),
                   jax.ShapeDtypeStruct((T, MLA_HEADS * HEAD_PAD), CDT),
                   jax.ShapeDtypeStruct((T, MLA_WIDTH), CDT),
                   jax.ShapeDtypeStruct((T, POOL_WIDTH), F32),
                   jax.ShapeDtypeStruct((T, 2 * D_MODEL), CDT)],
        compiler_params=_cparams(("parallel",)),
        name="in_proj",
    )(x2d, *consts, *ropes, *tail)


def _attn_kernel(q_ref, k_ref, v_ref, o_ref):
    outs = []
    for j in range(2):
        q = q_ref[:, j * HEAD_PAD:(j + 1) * HEAD_PAD]
        k = k_ref[:, j * HEAD_PAD:(j + 1) * HEAD_PAD]
        v = v_ref[:, j * V_HEAD:(j + 1) * V_HEAD]
        s = _dot_nt(q, k)
        m = jnp.max(s, axis=-1, keepdims=True)
        p = jnp.exp(s - m)
        l = jnp.sum(p, axis=-1, keepdims=True)
        outs.append(_dot(p.astype(CDT), v) / l)
    o_ref[...] = jnp.concatenate(outs, axis=-1).astype(o_ref.dtype)


def _attention(q, k, v, B, S, tq):
    tq = min(tq, S)
    nq = S // tq
    return pl.pallas_call(
        _attn_kernel,
        grid=(B, MLA_HEADS // 2, nq),
        in_specs=[pl.BlockSpec((tq, 2 * HEAD_PAD), lambda b, h, i: (b * nq + i, h)),
                  pl.BlockSpec((S, 2 * HEAD_PAD), lambda b, h, i: (b, h)),
                  pl.BlockSpec((S, 2 * V_HEAD), lambda b, h, i: (b, h))],
        out_specs=pl.BlockSpec((tq, 2 * V_HEAD), lambda b, h, i: (b * nq + i, h)),
        out_shape=jax.ShapeDtypeStruct((B * S, MLA_WIDTH), CDT),
        compiler_params=_cparams(("parallel", "parallel", "parallel")),
        name="attention",
    )(q, k, v)


def _pool_kernel(p_ref, pw_ref, ps_ref, y_ref):
    S = p_ref.shape[0]
    t = lax.broadcasted_iota(jnp.int32, (S, POOL_GC), 0)
    for g, w in enumerate(POOL_WINDOWS):
        half = w // 2
        sl = slice(g * POOL_GC, (g + 1) * POOL_GC)
        pg = p_ref[:, sl]
        acc = pg
        for d in range(-half, half):
            if d == 0:
                continue
            shifted = pltpu.roll(pg, (-d) % S, 0)
            valid = (t + d >= 0) & (t + d < S)
            acc = acc + jnp.where(valid, shifted, 0.0)
        count = (jnp.minimum(t + half, S) - jnp.maximum(t - half, 0)).astype(F32)
        mixed = acc / count - pg
        yg = _dot(mixed.astype(CDT), pw_ref[g]) * ps_ref[:, sl]
        y_ref[:, sl] = yg.astype(y_ref.dtype)


def _pool(p, lw, B, S):
    return pl.pallas_call(
        _pool_kernel,
        grid=(B,),
        in_specs=[pl.BlockSpec((S, POOL_WIDTH), lambda b: (b, 0)),
                  pl.BlockSpec(lw["pool_w"].shape, lambda b: (0, 0, 0)),
                  pl.BlockSpec((1, POOL_WIDTH), lambda b: (0, 0))],
        out_specs=pl.BlockSpec((S, POOL_WIDTH), lambda b: (b, 0)),
        out_shape=jax.ShapeDtypeStruct((B * S, POOL_WIDTH), CDT),
        compiler_params=_cparams(("parallel",)),
        name="pool",
    )(p, lw["pool_w"], lw["pool_scale"])


def _out_proj_kernel(x_ref, o_ref, y_ref, gate_ref, wo_ref, wpo_ref, wout_ref, g2_ref, wq_ref, keys_ref,
                     xnew_out, xnt_out, st_out):
    a = _dot(o_ref[...], wo_ref[...])
    bp = _dot(y_ref[...], wpo_ref[...])
    ga = gate_ref[:, :D_MODEL].astype(F32)
    gb = gate_ref[:, D_MODEL:].astype(F32)
    mix = ga * a + gb * bp
    xnew = x_ref[...] + _dot(mix.astype(CDT), wout_ref[...])
    xnew_out[...] = xnew
    xn = _rms(xnew, g2_ref[...])
    xnt_out[...] = xn.T.astype(xnt_out.dtype)
    qp = _dot(xn.astype(CDT), wq_ref[...])
    for hp in range(2 * PEER_HEADS):
        qh = qp[:, hp * PEER_HALF:(hp + 1) * PEER_HALF].astype(CDT)
        st_out[hp] = _dot_nt(keys_ref[hp], qh)


def _out_proj(x2d, o, y, gates, lw, tm):
    T = x2d.shape[0]
    tm = min(tm, T)
    row = lambda w: pl.BlockSpec((tm, w), lambda i: (i, 0))
    full = lambda a: pl.BlockSpec(a.shape, lambda i: (0,) * a.ndim)
    consts = [lw["w_o_mla"], lw["w_pool_out"], lw["w_out"], lw["g2"], lw["peer_wq"], lw["peer_keys"]]
    return pl.pallas_call(
        _out_proj_kernel,
        grid=(T // tm,),
        in_specs=[row(D_MODEL), row(MLA_WIDTH), row(POOL_WIDTH), row(2 * D_MODEL)] + [full(a) for a in consts],
        out_specs=[row(D_MODEL),
                   pl.BlockSpec((D_MODEL, tm), lambda i: (0, i)),
                   pl.BlockSpec((2 * PEER_HEADS, N_KEYS, tm), lambda i: (0, 0, i))],
        out_shape=[jax.ShapeDtypeStruct((T, D_MODEL), F32),
                   jax.ShapeDtypeStruct((D_MODEL, T), CDT),
                   jax.ShapeDtypeStruct((2 * PEER_HEADS, N_KEYS, T), F32)],
        compiler_params=_cparams(("parallel",)),
        name="out_proj",
    )(x2d, o, y, gates, *consts)


def _top16(s, key_iota, row16, exact):
    work = s
    rank = jnp.full(s.shape, float(PEER_TOPK), F32)
    tops = jnp.zeros((PEER_TOPK, s.shape[1]), F32)
    for r in range(PEER_TOPK):
        m = jnp.max(work, axis=0, keepdims=True)
        sel = work == m
        if exact:
            first = jnp.min(jnp.where(sel, key_iota, float(N_KEYS)), axis=0, keepdims=True)
            sel = key_iota == first
            rank = jnp.where(sel, float(r), rank)
            work = jnp.where(sel, NEG_INF, work)
        else:
            work = jnp.where(sel, -_FLT_MAX + r * _FLT_TOP_ULP, work)
        tops = jnp.where(row16 == r, m, tops)
    if not exact:
        rank = jnp.minimum((work + _FLT_MAX) * (1.0 / _FLT_TOP_ULP), float(PEER_TOPK))
    return rank, tops


def _cand_grid(ta, tb, combine):
    pieces = [combine(ta[0:1], tb)]
    for r1 in range(1, 8):
        pieces.append(combine(ta[r1:r1 + 1], tb[0:8]))
    pieces.append(combine(ta[8:16], tb[0:1]))
    return jnp.concatenate(pieces, axis=0)


def _count_true(mask):
    return jnp.sum(jnp.where(mask, 1.0, 0.0), axis=0, keepdims=True)


def _route(s1, s2, key_iota, row16, fidx, exact):
    rank1, ta = _top16(s1, key_iota, row16, exact)
    rank2, tb = _top16(s2, key_iota, row16, exact)
    a0 = ta[0:1]
    b0 = tb[0:1]
    cand0 = jnp.where(fidx < _FIDX_INVALID, _cand_grid(ta, tb, lambda a, b: a + b), NEG_INF)
    ecand = _cand_grid(jnp.exp(ta - a0), jnp.exp(tb - b0), lambda a, b: a * b)
    cand = cand0
    if exact:
        selected = jnp.zeros(cand.shape, F32)
        for _ in range(PEER_TOPK):
            m = jnp.max(cand, axis=0, keepdims=True)
            first = jnp.min(jnp.where(cand == m, fidx, float(_FIDX_INVALID)), axis=0, keepdims=True)
            sel = fidx == first
            selected = jnp.where(sel, 1.0, selected)
            cand = jnp.where(sel, NEG_INF, cand)
        bad = jnp.zeros_like(a0)
    else:
        for _ in range(PEER_TOPK):
            m = jnp.max(cand, axis=0, keepdims=True)
            cand = jnp.where(cand == m, NEG_INF, cand)
        selected = jnp.where(cand0 >= m, 1.0, 0.0)
        k = float(PEER_TOPK)
        bad = jnp.where((_count_true(rank1 < k) != k) | (_count_true(rank2 < k) != k)
                        | (jnp.sum(selected, axis=0, keepdims=True) != k), 1.0, 0.0)
    z = jnp.sum(selected * ecand, axis=0, keepdims=True)
    cntr = [jnp.sum(selected[0:16], axis=0, keepdims=True)]
    for r1 in range(1, 8):
        cntr.append(jnp.sum(selected[8 + 8 * r1:16 + 8 * r1], axis=0, keepdims=True))
    for r1 in range(8, 16):
        cntr.append(selected[64 + r1:65 + r1])
    cnt = jnp.zeros(s1.shape, F32)
    for r1 in range(PEER_TOPK):
        cnt = jnp.where(rank1 == float(r1), cntr[r1], cnt)
    return rank2, jnp.exp(s2 - b0), cnt, jnp.exp(s1 - a0) / z, bad


def _peer_topk_kernel(s_ref, fidx_ref, r2_out, e2_out, cnt_out, c_out, *, n_chunks):
    key_iota = lax.broadcasted_iota(jnp.int32, (N_KEYS, LANES), 0).astype(F32)
    row16 = lax.broadcasted_iota(jnp.int32, (PEER_TOPK, LANES), 0)
    fidx = fidx_ref[...]

    def body(it, carry):
        h = it // n_chunks
        off = pl.multiple_of((it % n_chunks) * LANES, LANES)
        s1 = s_ref[2 * h, :, pl.ds(off, LANES)]
        s2 = s_ref[2 * h + 1, :, pl.ds(off, LANES)]

        def store(r2, e2, cnt, c):
            r2_out[h, :, pl.ds(off, LANES)] = r2.astype(r2_out.dtype)
            e2_out[h, :, pl.ds(off, LANES)] = e2.astype(e2_out.dtype)
            cnt_out[h, :, pl.ds(off, LANES)] = cnt.astype(cnt_out.dtype)
            c_out[h, :, pl.ds(off, LANES)] = c.astype(c_out.dtype)

        *fast, bad = _route(s1, s2, key_iota, row16, fidx, exact=False)
        store(*fast)

        @pl.when(jnp.max(bad) > 0.0)
        def _():
            *slow, _ = _route(s1, s2, key_iota, row16, fidx, exact=True)
            store(*slow)

        return carry

    lax.fori_loop(0, PEER_HEADS * n_chunks, body, 0)


def _cand_fidx():
    rows = []
    rows += [0 * 16 + r2 for r2 in range(16)]
    for r1 in range(1, 8):
        n = PEER_TOPK // (r1 + 1)
        rows += [r1 * 16 + r2 if r2 < n else _FIDX_INVALID + r1 * 16 + r2 for r2 in range(8)]
    rows += [r1 * 16 for r1 in range(8, 16)]
    assert len(rows) == _CAND_ROWS
    return jnp.broadcast_to(jnp.asarray(rows, F32)[:, None], (_CAND_ROWS, LANES))


def _peer_topk(st, tmk):
    T = st.shape[-1]
    tmk = min(tmk, T)
    spec = pl.BlockSpec((PEER_HEADS, N_KEYS, tmk), lambda i: (0, 0, i))
    shp = jax.ShapeDtypeStruct((PEER_HEADS, N_KEYS, T), GDT)
    shp32 = jax.ShapeDtypeStruct((PEER_HEADS, N_KEYS, T), F32)
    return pl.pallas_call(
        functools.partial(_peer_topk_kernel, n_chunks=tmk // LANES),
        grid=(T // tmk,),
        in_specs=[pl.BlockSpec((2 * PEER_HEADS, N_KEYS, tmk), lambda i: (0, 0, i)),
                  pl.BlockSpec((_CAND_ROWS, LANES), lambda i: (0, 0))],
        out_specs=[spec] * 4,
        out_shape=[shp, shp, shp32, shp32],
        compiler_params=_cparams(("parallel",)),
        name="peer_topk",
    )(st, _cand_fidx())


def _peer_gate_kernel(xnt_ref, r2_ref, e2_ref, cnt_ref, c_ref, u_ref, at_ref, ht_ref, bc_ref, *, eb, tc):
    e = pl.program_id(1)
    tm = xnt_ref.shape[1]
    jpb = eb // N_KEYS
    pk = bc_ref.shape[-2]

    def fill(h, carry):
        for jb in range(jpb):
            j = e * jpb + jb
            bc_ref[0, h, jb] = jnp.broadcast_to(cnt_ref[h, pl.ds(j, 1), :], (pk, tm)).astype(GDT)
            bc_ref[1, h, jb] = jnp.broadcast_to(c_ref[h, pl.ds(j, 1), :], (pk, tm)).astype(GDT)
        return carry

    lax.fori_loop(0, PEER_HEADS, fill, 0)

    pieces = [(slice(jb * N_KEYS, (jb + 1) * N_KEYS), slice(ci * tc, (ci + 1) * tc), jb)
              for jb in range(jpb) for ci in range(tm // tc)]
    for krows, cols, _ in pieces:
        ht_ref[krows, cols] = _dot(u_ref[krows, :], xnt_ref[:, cols])
    for krows, cols, jb in pieces:
        g = jnp.zeros((N_KEYS // pk, pk, tc), GDT)
        for h in range(PEER_HEADS):
            r2 = r2_ref[h, :, cols].reshape(N_KEYS // pk, pk, tc)
            e2 = e2_ref[h, :, cols].reshape(N_KEYS // pk, pk, tc)
            g = g + jnp.where(r2 < bc_ref[0, h, jb, :, cols], e2 * bc_ref[1, h, jb, :, cols], 0)
        a = g.reshape(N_KEYS, tc) * jax.nn.gelu(ht_ref[krows, cols]).astype(GDT)
        at_ref[krows, cols] = a.astype(at_ref.dtype)


def _peer_gate(xnt, r2, e2, cnt, c, u, tm, eb, tc):
    T = xnt.shape[1]
    tm = min(tm, T)
    tc = min(tc, tm)
    tok3 = pl.BlockSpec((PEER_HEADS, N_KEYS, tm), lambda i, e: (0, 0, i))
    return pl.pallas_call(
        functools.partial(_peer_gate_kernel, eb=eb, tc=tc),
        grid=(T // tm, N_EXPERTS // eb),
        in_specs=[pl.BlockSpec((D_MODEL, tm), lambda i, e: (0, i)),
                  tok3, tok3, tok3, tok3,
                  pl.BlockSpec((eb, D_MODEL), lambda i, e: (e, 0))],
        out_specs=pl.BlockSpec((eb, tm), lambda i, e: (e, i)),
        out_shape=jax.ShapeDtypeStruct((N_EXPERTS, T), CDT),
        scratch_shapes=[pltpu.VMEM((eb, tm), F32),
                        pltpu.VMEM((2, PEER_HEADS, eb // N_KEYS, GATE_TILE_ROWS, tm), GDT)],
        compiler_params=_cparams(("parallel", "arbitrary")),
        name="peer_gate",
    )(xnt, r2, e2, cnt, c, u)


def _peer_out_kernel(x_ref, vt_ref, at_ref, out_ref, acc_ref):
    k = pl.program_id(1)

    @pl.when(k == 0)
    def _():
        acc_ref[...] = jnp.zeros_like(acc_ref)

    acc_ref[...] += _dot(vt_ref[...], at_ref[...])

    @pl.when(k == pl.num_programs(1) - 1)
    def _():
        out_ref[...] = x_ref[...] + acc_ref[...].T


def _peer_out(x2d, vt, at, tn, tk):
    T = x2d.shape[0]
    tn = min(tn, T)
    return pl.pallas_call(
        _peer_out_kernel,
        grid=(T // tn, N_EXPERTS // tk),
        in_specs=[pl.BlockSpec((tn, D_MODEL), lambda i, k: (i, 0)),
                  pl.BlockSpec((D_MODEL, tk), lambda i, k: (0, k)),
                  pl.BlockSpec((tk, tn), lambda i, k: (k, i))],
        out_specs=pl.BlockSpec((tn, D_MODEL), lambda i, k: (i, 0)),
        out_shape=jax.ShapeDtypeStruct((T, D_MODEL), F32),
        scratch_shapes=[pltpu.VMEM((D_MODEL, tn), F32)],
        compiler_params=_cparams(("parallel", "arbitrary")),
        name="peer_out",
    )(x2d, vt, at)


def _pad_heads(w, width):
    r = w.shape[0]
    w = w.reshape(r, MLA_HEADS, width)
    w = jnp.pad(w, ((0, 0), (0, 0), (0, HEAD_PAD - width)))
    return w.reshape(r, MLA_HEADS * HEAD_PAD)


def _layer_weights(l, norm1_g, w_in, q_lora_g, kv_lora_g, w_uq, w_ukv, q_head_g, k_head_g, w_o_mla,
                   pool_w, pool_scale, w_pool_out, gate_bias, w_out, norm2_g, peer_wq, peer_keys,
                   peer_u, peer_v):
    wi = w_in[l]
    c0 = Q_LORA
    c1 = c0 + KV_LORA
    c2 = c1 + QK_ROPE
    c3 = c2 + POOL_WIDTH
    zeros = lambda n: jnp.zeros((D_MODEL, n), wi.dtype)
    w_in_pad = jnp.concatenate([wi[:, :c1], zeros(QK_NOPE), wi[:, c1:c2], zeros(LANES - QK_HEAD),
                                wi[:, c2:]], axis=1)
    wkv = w_ukv[l].reshape(KV_LORA, MLA_HEADS, QK_NOPE + V_HEAD)
    w_uk = _pad_heads(wkv[:, :, :QK_NOPE].reshape(KV_LORA, MLA_HEADS * QK_NOPE), QK_NOPE)
    w_uv = wkv[:, :, QK_NOPE:].reshape(KV_LORA, MLA_WIDTH)
    pad_g = lambda g: jnp.pad(g, (0, LANES - QK_HEAD)).reshape(1, LANES)
    return {
        "g1": norm1_g[l].reshape(1, D_MODEL),
        "w_in": w_in_pad.astype(CDT),
        "qlg": q_lora_g[l].reshape(1, Q_LORA),
        "kvlg": kv_lora_g[l].reshape(1, KV_LORA),
        "w_uq": _pad_heads(w_uq[l], QK_HEAD).astype(CDT),
        "w_uk": w_uk.astype(CDT),
        "w_uv": w_uv.astype(CDT),
        "qg": pad_g(q_head_g[l] * (1.0 / math.sqrt(QK_HEAD))),
        "kg": pad_g(k_head_g[l]),
        "gate_bias": gate_bias[l].reshape(1, 2 * D_MODEL),
        "w_o_mla": w_o_mla[l].astype(CDT),
        "pool_w": pool_w[l].astype(CDT),
        "pool_scale": pool_scale[l].reshape(1, POOL_WIDTH),
        "w_pool_out": w_pool_out[l].astype(CDT),
        "w_out": w_out[l].astype(CDT),
        "g2": norm2_g[l].reshape(1, D_MODEL),
        "peer_wq": peer_wq[l].astype(CDT),
        "peer_keys": peer_keys[l].reshape(2 * PEER_HEADS, N_KEYS, PEER_HALF).astype(CDT),
        "peer_u": peer_u[l].astype(CDT),
        "peer_vt": peer_v[l].T.astype(CDT),
    }


def kernel(x, positions, norm1_g, w_in, q_lora_g, kv_lora_g, w_uq, w_ukv, q_head_g, k_head_g, w_o_mla,
           pool_w, pool_scale, w_pool_out, gate_bias, w_out, norm2_g, peer_wq, peer_keys, peer_u, peer_v):
    B, S, D = x.shape
    assert D == D_MODEL and S % LANES == 0
    T = B * S
    depth = norm1_g.shape[0]
    ropes = _rope_tables(positions)
    x2d = x.reshape(T, D)
    for l in range(depth):
        lw = _layer_weights(l, norm1_g, w_in, q_lora_g, kv_lora_g, w_uq, w_ukv, q_head_g, k_head_g,
                            w_o_mla, pool_w, pool_scale, w_pool_out, gate_bias, w_out, norm2_g,
                            peer_wq, peer_keys, peer_u, peer_v)
        q, k, v, p, gates = _in_proj(x2d, lw, ropes, tm=256)
        o = _attention(q, k, v, B, S, tq=256)
        y = _pool(p, lw, B, S)
        x2d, xnt, st = _out_proj(x2d, o, y, gates, lw, tm=256)
        r2, e2, cnt, c = _peer_topk(st, tmk=512)
        at = _peer_gate(xnt, r2, e2, cnt, c, lw["peer_u"], tm=512, eb=512, tc=256)
        x2d = _peer_out(x2d, lw["peer_vt"], at, tn=1024, tk=2048)
    return x2d.reshape(B, S, D)
```

```python
import functools
import math

import jax
import jax.numpy as jnp
import numpy as np
from jax import lax
from jax.experimental import pallas as pl
from jax.experimental.pallas import tpu as pltpu

D_MODEL = 1024
MLA_HEADS = 8
Q_LORA = 384
KV_LORA = 256
QK_NOPE = 64
QK_ROPE = 32
QK_HEAD = QK_NOPE + QK_ROPE
V_HEAD = 64
MLA_WIDTH = MLA_HEADS * V_HEAD
ROPE_BASE = 10000.0
POOL_WINDOWS = (2, 4, 8, 16)
POOL_WIDTH = 512
POOL_GC = 128
PEER_HEADS = 8
N_KEYS = 128
N_EXPERTS = N_KEYS * N_KEYS
PEER_HALF = 128
PEER_TOPK = 16
RMS_EPS = 1e-6

LANES = 128
GATE_TILE_ROWS = 16
HEAD_PAD = LANES
IN_PAD = Q_LORA + KV_LORA + LANES + POOL_WIDTH + 2 * D_MODEL
VMEM_LIMIT = 48 * 1024 * 1024

CDT = jnp.bfloat16
GDT = jnp.bfloat16
F32 = jnp.float32
NEG_INF = float("-inf")
_FLT_MAX = float(np.finfo(np.float32).max)
_FLT_TOP_ULP = 2.0 ** 104

_CAND_ROWS = 16 + 8 * 7 + 8
_FIDX_INVALID = 1 << 20


def _cparams(sem, vmem=VMEM_LIMIT, flags=None):
    return pltpu.CompilerParams(dimension_semantics=sem, vmem_limit_bytes=vmem, flags=flags)


def _rms(x, g):
    return x * lax.rsqrt(jnp.mean(x * x, axis=-1, keepdims=True) + RMS_EPS) * g


def _dot(a, b):
    return jnp.dot(a, b, preferred_element_type=F32)


def _dot_nt(a, b):
    return lax.dot_general(a, b, (((1,), (1,)), ((), ())), preferred_element_type=F32)


def _rope_kernel(pos_ref, inv_ref, c_ref, s1_ref, s2_ref):
    ang = pos_ref[...] * inv_ref[...]
    lane = lax.broadcasted_iota(jnp.int32, ang.shape, 1)
    cos = jnp.cos(ang)
    sin = jnp.sin(ang)
    half = QK_ROPE // 2
    c_ref[...] = jnp.where(lane < QK_NOPE, 1.0, jnp.where(lane < QK_HEAD, cos, 0.0))
    s1_ref[...] = jnp.where((lane >= QK_NOPE) & (lane < QK_NOPE + half), -sin, 0.0)
    s2_ref[...] = jnp.where((lane >= QK_NOPE + half) & (lane < QK_HEAD), sin, 0.0)


def _rope_tables(positions):
    T = positions.size
    tm = min(T, 2048)
    pos = jnp.broadcast_to(positions.reshape(T, 1).astype(F32), (T, LANES))
    inv = ROPE_BASE ** (-jnp.arange(0, QK_ROPE, 2, dtype=F32) / QK_ROPE)
    inv_pat = jnp.concatenate([jnp.zeros((QK_NOPE,), F32), inv, inv,
                               jnp.zeros((LANES - QK_HEAD,), F32)]).reshape(1, LANES)
    spec = pl.BlockSpec((tm, LANES), lambda i: (i, 0))
    return pl.pallas_call(
        _rope_kernel,
        grid=(T // tm,),
        in_specs=[spec, pl.BlockSpec((1, LANES), lambda i: (0, 0))],
        out_specs=[spec, spec, spec],
        out_shape=[jax.ShapeDtypeStruct((T, LANES), F32)] * 3,
        compiler_params=_cparams(("parallel",)),
        name="rope_tables",
    )(pos, inv_pat)


def _in_proj_kernel(x_ref, g1_ref, win_ref, qlg_ref, kvlg_ref, wuq_ref, wuk_ref, wuv_ref,
                    c_ref, s1_ref, s2_ref, qg_ref, kg_ref, bias_ref,
                    q_out, k_out, v_out, p_out, gate_out):
    xn = _rms(x_ref[...], g1_ref[...])
    y = _dot(xn.astype(CDT), win_ref[...])
    o = 0
    cq = y[:, o:o + Q_LORA]; o += Q_LORA
    ckv = y[:, o:o + KV_LORA]; o += KV_LORA
    kpe = y[:, o:o + LANES]; o += LANES
    p_out[...] = y[:, o:o + POOL_WIDTH]; o += POOL_WIDTH
    gate_out[...] = jax.nn.sigmoid(y[:, o:] + bias_ref[...]).astype(gate_out.dtype)

    cqn = _rms(cq, qlg_ref[...]).astype(CDT)
    ckvn = _rms(ckv, kvlg_ref[...]).astype(CDT)
    q_raw = _dot(cqn, wuq_ref[...])
    k_raw = _dot(ckvn, wuk_ref[...])
    v_out[...] = _dot(ckvn, wuv_ref[...]).astype(v_out.dtype)

    cpat = c_ref[...]
    s1pat = s1_ref[...]
    s2pat = s2_ref[...]
    half = QK_ROPE // 2

    def rope(xh):
        return (xh * cpat + pltpu.roll(xh, LANES - half, 1) * s1pat
                + pltpu.roll(xh, half, 1) * s2pat)

    def head_norm(xh, g):
        ms = jnp.sum(xh * xh, axis=-1, keepdims=True) * (1.0 / QK_HEAD)
        return xh * lax.rsqrt(ms + RMS_EPS) * g

    kpe_r = rope(kpe)
    qg = qg_ref[...]
    kg = kg_ref[...]
    for h in range(MLA_HEADS):
        sl = slice(h * HEAD_PAD, (h + 1) * HEAD_PAD)
        q_out[:, sl] = head_norm(rope(q_raw[:, sl]), qg).astype(q_out.dtype)
        k_out[:, sl] = head_norm(k_raw[:, sl] + kpe_r, kg).astype(k_out.dtype)


def _in_proj(x2d, lw, ropes, tm):
    T = x2d.shape[0]
    tm = min(tm, T)
    row = lambda w: pl.BlockSpec((tm, w), lambda i: (i, 0))
    full = lambda a: pl.BlockSpec(a.shape, lambda i: (0,) * a.ndim)
    consts = [lw["g1"], lw["w_in"], lw["qlg"], lw["kvlg"], lw["w_uq"], lw["w_uk"], lw["w_uv"]]
    tail = [lw["qg"], lw["kg"], lw["gate_bias"]]
    return pl.pallas_call(
        _in_proj_kernel,
        grid=(T // tm,),
        in_specs=[row(D_MODEL)] + [full(a) for a in consts] + [row(LANES)] * 3 + [full(a) for a in tail],
        out_specs=[row(MLA_HEADS * HEAD_PAD), row(MLA_HEADS * HEAD_PAD), row(MLA_WIDTH),
                   row(POOL_WIDTH), row(2 * D_MODEL)],
        out_shape=[jax.ShapeDtypeStruct((T, MLA_HEADS * HEAD_PAD), CDT),
                   jax.ShapeDtypeStruct((T, MLA_HEADS * HEAD_PAD), CDT),
                   jax.ShapeDtypeStruct((T, MLA_WIDTH), CDT),
                   jax.ShapeDtypeStruct((T, POOL_WIDTH), F32),
                   jax.ShapeDtypeStruct((T, 2 * D_MODEL), CDT)],
        compiler_params=_cparams(("parallel",)),
        name="in_proj",
    )(x2d, *consts, *ropes, *tail)


def _attn_kernel(q_ref, k_ref, v_ref, o_ref):
    outs = []
    for j in range(2):
        q = q_ref[:, j * HEAD_PAD:(j + 1) * HEAD_PAD]
        k = k_ref[:, j * HEAD_PAD:(j + 1) * HEAD_PAD]
        v = v_ref[:, j * V_HEAD:(j + 1) * V_HEAD]
        s = _dot_nt(q, k)
        m = jnp.max(s, axis=-1, keepdims=True)
        p = jnp.exp(s - m)
        l = jnp.sum(p, axis=-1, keepdims=True)
        outs.append(_dot(p.astype(CDT), v) / l)
    o_ref[...] = jnp.concatenate(outs, axis=-1).astype(o_ref.dtype)


def _attention(q, k, v, B, S, tq):
    tq = min(tq, S)
    nq = S // tq
    return pl.pallas_call(
        _attn_kernel,
        grid=(B, MLA_HEADS // 2, nq),
        in_specs=[pl.BlockSpec((tq, 2 * HEAD_PAD), lambda b, h, i: (b * nq + i, h)),
                  pl.BlockSpec((S, 2 * HEAD_PAD), lambda b, h, i: (b, h)),
                  pl.BlockSpec((S, 2 * V_HEAD), lambda b, h, i: (b, h))],
        out_specs=pl.BlockSpec((tq, 2 * V_HEAD), lambda b, h, i: (b * nq + i, h)),
        out_shape=jax.ShapeDtypeStruct((B * S, MLA_WIDTH), CDT),
        compiler_params=_cparams(("parallel", "parallel", "parallel")),
        name="attention",
    )(q, k, v)


def _pool_kernel(p_ref, pw_ref, ps_ref, y_ref):
    S = p_ref.shape[0]
    t = lax.broadcasted_iota(jnp.int32, (S, POOL_GC), 0)
    for g, w in enumerate(POOL_WINDOWS):
        half = w // 2
        sl = slice(g * POOL_GC, (g + 1) * POOL_GC)
        pg = p_ref[:, sl]
        acc = pg
        for d in range(-half, half):
            if d == 0:
                continue
            shifted = pltpu.roll(pg, (-d) % S, 0)
            valid = (t + d >= 0) & (t + d < S)
            acc = acc + jnp.where(valid, shifted, 0.0)
        count = (jnp.minimum(t + half, S) - jnp.maximum(t - half, 0)).astype(F32)
        mixed = acc / count - pg
        yg = _dot(mixed.astype(CDT), pw_ref[g]) * ps_ref[:, sl]
        y_ref[:, sl] = yg.astype(y_ref.dtype)


def _pool(p, lw, B, S):
    return pl.pallas_call(
        _pool_kernel,
        grid=(B,),
        in_specs=[pl.BlockSpec((S, POOL_WIDTH), lambda b: (b, 0)),
                  pl.BlockSpec(lw["pool_w"].shape, lambda b: (0, 0, 0)),
                  pl.BlockSpec((1, POOL_WIDTH), lambda b: (0, 0))],
        out_specs=pl.BlockSpec((S, POOL_WIDTH), lambda b: (b, 0)),
        out_shape=jax.ShapeDtypeStruct((B * S, POOL_WIDTH), CDT),
        compiler_params=_cparams(("parallel",)),
        name="pool",
    )(p, lw["pool_w"], lw["pool_scale"])


def _out_proj_kernel(x_ref, o_ref, y_ref, gate_ref, wo_ref, wpo_ref, wout_ref, g2_ref, wq_ref, keys_ref,
                     xnew_out, xnt_out, st_out):
    a = _dot(o_ref[...], wo_ref[...])
    bp = _dot(y_ref[...], wpo_ref[...])
    ga = gate_ref[:, :D_MODEL].astype(F32)
    gb = gate_ref[:, D_MODEL:].astype(F32)
    mix = ga * a + gb * bp
    xnew = x_ref[...] + _dot(mix.astype(CDT), wout_ref[...])
    xnew_out[...] = xnew
    xn = _rms(xnew, g2_ref[...])
    xnt_out[...] = xn.T.astype(xnt_out.dtype)
    qp = _dot(xn.astype(CDT), wq_ref[...])
    for hp in range(2 * PEER_HEADS):
        qh = qp[:, hp * PEER_HALF:(hp + 1) * PEER_HALF].astype(CDT)
        st_out[hp] = _dot_nt(keys_ref[hp], qh)


def _out_proj(x2d, o, y, gates, lw, tm):
    T = x2d.shape[0]
    tm = min(tm, T)
    row = lambda w: pl.BlockSpec((tm, w), lambda i: (i, 0))
    full = lambda a: pl.BlockSpec(a.shape, lambda i: (0,) * a.ndim)
    consts = [lw["w_o_mla"], lw["w_pool_out"], lw["w_out"], lw["g2"], lw["peer_wq"], lw["peer_keys"]]
    return pl.pallas_call(
        _out_proj_kernel,
        grid=(T // tm,),
        in_specs=[row(D_MODEL), row(MLA_WIDTH), row(POOL_WIDTH), row(2 * D_MODEL)] + [full(a) for a in consts],
        out_specs=[row(D_MODEL),
                   pl.BlockSpec((D_MODEL, tm), lambda i: (0, i)),
                   pl.BlockSpec((2 * PEER_HEADS, N_KEYS, tm), lambda i: (0, 0, i))],
        out_shape=[jax.ShapeDtypeStruct((T, D_MODEL), F32),
                   jax.ShapeDtypeStruct((D_MODEL, T), CDT),
                   jax.ShapeDtypeStruct((2 * PEER_HEADS, N_KEYS, T), F32)],
        compiler_params=_cparams(("parallel",)),
        name="out_proj",
    )(x2d, o, y, gates, *consts)


def _top16(s, key_iota, row16, exact):
    work = s
    rank = jnp.full(s.shape, float(PEER_TOPK), F32)
    tops = jnp.zeros((PEER_TOPK, s.shape[1]), F32)
    for r in range(PEER_TOPK):
        m = jnp.max(work, axis=0, keepdims=True)
        sel = work == m
        if exact:
            first = jnp.min(jnp.where(sel, key_iota, float(N_KEYS)), axis=0, keepdims=True)
            sel = key_iota == first
            rank = jnp.where(sel, float(r), rank)
            work = jnp.where(sel, NEG_INF, work)
        else:
            work = jnp.where(sel, -_FLT_MAX + r * _FLT_TOP_ULP, work)
        tops = jnp.where(row16 == r, m, tops)
    if not exact:
        rank = jnp.minimum((work + _FLT_MAX) * (1.0 / _FLT_TOP_ULP), float(PEER_TOPK))
    return rank, tops


def _cand_grid(ta, tb, combine):
    pieces = [combine(ta[0:1], tb)]
    for r1 in range(1, 8):
        pieces.append(combine(ta[r1:r1 + 1], tb[0:8]))
    pieces.append(combine(ta[8:16], tb[0:1]))
    return jnp.concatenate(pieces, axis=0)


def _count_true(mask):
    return jnp.sum(jnp.where(mask, 1.0, 0.0), axis=0, keepdims=True)


def _route(s1, s2, key_iota, row16, fidx, exact):
    rank1, ta = _top16(s1, key_iota, row16, exact)
    rank2, tb = _top16(s2, key_iota, row16, exact)
    a0 = ta[0:1]
    b0 = tb[0:1]
    cand0 = jnp.where(fidx < _FIDX_INVALID, _cand_grid(ta, tb, lambda a, b: a + b), NEG_INF)
    ecand = _cand_grid(jnp.exp(ta - a0), jnp.exp(tb - b0), lambda a, b: a * b)
    cand = cand0
    if exact:
        selected = jnp.zeros(cand.shape, F32)
        for _ in range(PEER_TOPK):
            m = jnp.max(cand, axis=0, keepdims=True)
            first = jnp.min(jnp.where(cand == m, fidx, float(_FIDX_INVALID)), axis=0, keepdims=True)
            sel = fidx == first
            selected = jnp.where(sel, 1.0, selected)
            cand = jnp.where(sel, NEG_INF, cand)
        bad = jnp.zeros_like(a0)
    else:
        for _ in range(PEER_TOPK):
            m = jnp.max(cand, axis=0, keepdims=True)
            cand = jnp.where(cand == m, NEG_INF, cand)
        selected = jnp.where(cand0 >= m, 1.0, 0.0)
        k = float(PEER_TOPK)
        bad = jnp.where((_count_true(rank1 < k) != k) | (_count_true(rank2 < k) != k)
                        | (jnp.sum(selected, axis=0, keepdims=True) != k), 1.0, 0.0)
    z = jnp.sum(selected * ecand, axis=0, keepdims=True)
    cntr = [jnp.sum(selected[0:16], axis=0, keepdims=True)]
    for r1 in range(1, 8):
        cntr.append(jnp.sum(selected[8 + 8 * r1:16 + 8 * r1], axis=0, keepdims=True))
    for r1 in range(8, 16):
        cntr.append(selected[64 + r1:65 + r1])
    cnt = jnp.zeros(s1.shape, F32)
    for r1 in range(PEER_TOPK):
        cnt = jnp.where(rank1 == float(r1), cntr[r1], cnt)
    return rank2, jnp.exp(s2 - b0), cnt, jnp.exp(s1 - a0) / z, bad


def _peer_topk_kernel(s_ref, fidx_ref, r2_out, e2_out, cnt_out, c_out, *, n_chunks):
    key_iota = lax.broadcasted_iota(jnp.int32, (N_KEYS, LANES), 0).astype(F32)
    row16 = lax.broadcasted_iota(jnp.int32, (PEER_TOPK, LANES), 0)
    fidx = fidx_ref[...]

    def body(it, carry):
        h = it // n_chunks
        off = pl.multiple_of((it % n_chunks) * LANES, LANES)
        s1 = s_ref[2 * h, :, pl.ds(off, LANES)]
        s2 = s_ref[2 * h + 1, :, pl.ds(off, LANES)]

        def store(r2, e2, cnt, c):
            r2_out[h, :, pl.ds(off, LANES)] = r2.astype(r2_out.dtype)
            e2_out[h, :, pl.ds(off, LANES)] = e2.astype(e2_out.dtype)
            cnt_out[h, :, pl.ds(off, LANES)] = cnt.astype(cnt_out.dtype)
            c_out[h, :, pl.ds(off, LANES)] = c.astype(c_out.dtype)

        *fast, bad = _route(s1, s2, key_iota, row16, fidx, exact=False)
        store(*fast)

        @pl.when(jnp.max(bad) > 0.0)
        def _():
            *slow, _ = _route(s1, s2, key_iota, row16, fidx, exact=True)
            store(*slow)

        return carry

    lax.fori_loop(0, PEER_HEADS * n_chunks, body, 0)


def _cand_fidx():
    rows = []
    rows += [0 * 16 + r2 for r2 in range(16)]
    for r1 in range(1, 8):
        n = PEER_TOPK // (r1 + 1)
        rows += [r1 * 16 + r2 if r2 < n else _FIDX_INVALID + r1 * 16 + r2 for r2 in range(8)]
    rows += [r1 * 16 for r1 in range(8, 16)]
    assert len(rows) == _CAND_ROWS
    return jnp.broadcast_to(jnp.asarray(rows, F32)[:, None], (_CAND_ROWS, LANES))


def _peer_topk(st, tmk):
    T = st.shape[-1]
    tmk = min(tmk, T)
    spec = pl.BlockSpec((PEER_HEADS, N_KEYS, tmk), lambda i: (0, 0, i))
    shp = jax.ShapeDtypeStruct((PEER_HEADS, N_KEYS, T), GDT)
    shp32 = jax.ShapeDtypeStruct((PEER_HEADS, N_KEYS, T), F32)
    return pl.pallas_call(
        functools.partial(_peer_topk_kernel, n_chunks=tmk // LANES),
        grid=(T // tmk,),
        in_specs=[pl.BlockSpec((2 * PEER_HEADS, N_KEYS, tmk), lambda i: (0, 0, i)),
                  pl.BlockSpec((_CAND_ROWS, LANES), lambda i: (0, 0))],
        out_specs=[spec] * 4,
        out_shape=[shp, shp, shp32, shp32],
        compiler_params=_cparams(("parallel",)),
        name="peer_topk",
    )(st, _cand_fidx())


def _peer_dense_kernel(x_ref, xnt_ref, r2_ref, e2_ref, cnt_ref, c_ref, u_ref, vt_ref, out_ref,
                       acc_ref, ht_ref, at_ref, bc_ref, *, eb, tc, nsub):
    e = pl.program_id(1)
    tm = xnt_ref.shape[1]
    jpb = eb // N_KEYS
    sub = eb // nsub
    pk = bc_ref.shape[-2]

    @pl.when(e == 0)
    def _():
        acc_ref[...] = jnp.zeros_like(acc_ref)

    def fill(h, carry):
        for jb in range(jpb):
            j = e * jpb + jb
            bc_ref[0, h, jb] = jnp.broadcast_to(cnt_ref[h, pl.ds(j, 1), :], (pk, tm)).astype(GDT)
            bc_ref[1, h, jb] = jnp.broadcast_to(c_ref[h, pl.ds(j, 1), :], (pk, tm)).astype(GDT)
        return carry

    lax.fori_loop(0, PEER_HEADS, fill, 0)

    for sb in range(nsub):
        rows = slice(sb * sub, (sb + 1) * sub)
        ht_ref[rows, :] = _dot(u_ref[rows, :], xnt_ref[...])
    for sb in range(nsub):
        rows = slice(sb * sub, (sb + 1) * sub)
        for jb in range(sb * sub // N_KEYS, (sb + 1) * sub // N_KEYS):
            krows = slice(jb * N_KEYS, (jb + 1) * N_KEYS)
            for ci in range(tm // tc):
                cols = slice(ci * tc, (ci + 1) * tc)
                g = jnp.zeros((N_KEYS // pk, pk, tc), GDT)
                for h in range(PEER_HEADS):
                    r2 = r2_ref[h, :, cols].reshape(N_KEYS // pk, pk, tc)
                    e2 = e2_ref[h, :, cols].reshape(N_KEYS // pk, pk, tc)
                    g = g + jnp.where(r2 < bc_ref[0, h, jb, :, cols], e2 * bc_ref[1, h, jb, :, cols], 0)
                a = g.reshape(N_KEYS, tc) * jax.nn.gelu(ht_ref[krows, cols]).astype(GDT)
                at_ref[krows, cols] = a.astype(at_ref.dtype)
        acc_ref[...] += _dot(vt_ref[:, rows], at_ref[rows, :])

    @pl.when(e == pl.num_programs(1) - 1)
    def _():
        out_ref[...] = x_ref[...] + acc_ref[...].T


def _peer_dense(x2d, xnt, r2, e2, cnt, c, u, vt, tm, eb, tc):
    T = x2d.shape[0]
    tm = min(tm, T)
    tc = min(tc, tm)
    tok3 = pl.BlockSpec((PEER_HEADS, N_KEYS, tm), lambda i, e: (0, 0, i))
    return pl.pallas_call(
        functools.partial(_peer_dense_kernel, eb=eb, tc=tc, nsub=eb // 256),
        grid=(T // tm, N_EXPERTS // eb),
        in_specs=[pl.BlockSpec((tm, D_MODEL), lambda i, e: (i, 0)),
                  pl.BlockSpec((D_MODEL, tm), lambda i, e: (0, i)),
                  tok3, tok3, tok3, tok3,
                  pl.BlockSpec((eb, D_MODEL), lambda i, e: (e, 0)),
                  pl.BlockSpec((D_MODEL, eb), lambda i, e: (0, e))],
        out_specs=pl.BlockSpec((tm, D_MODEL), lambda i, e: (i, 0)),
        out_shape=jax.ShapeDtypeStruct((T, D_MODEL), F32),
        scratch_shapes=[pltpu.VMEM((D_MODEL, tm), F32),
                        pltpu.VMEM((eb, tm), F32),
                        pltpu.VMEM((eb, tm), CDT),
                        pltpu.VMEM((2, PEER_HEADS, eb // N_KEYS, GATE_TILE_ROWS, tm), GDT)],
        compiler_params=_cparams(("parallel", "arbitrary")),
        name="peer_dense",
    )(x2d, xnt, r2, e2, cnt, c, u, vt)


def _pad_heads(w, width):
    r = w.shape[0]
    w = w.reshape(r, MLA_HEADS, width)
    w = jnp.pad(w, ((0, 0), (0, 0), (0, HEAD_PAD - width)))
    return w.reshape(r, MLA_HEADS * HEAD_PAD)


def _layer_weights(l, norm1_g, w_in, q_lora_g, kv_lora_g, w_uq, w_ukv, q_head_g, k_head_g, w_o_mla,
                   pool_w, pool_scale, w_pool_out, gate_bias, w_out, norm2_g, peer_wq, peer_keys,
                   peer_u, peer_v):
    wi = w_in[l]
    c0 = Q_LORA
    c1 = c0 + KV_LORA
    c2 = c1 + QK_ROPE
    c3 = c2 + POOL_WIDTH
    zeros = lambda n: jnp.zeros((D_MODEL, n), wi.dtype)
    w_in_pad = jnp.concatenate([wi[:, :c1], zeros(QK_NOPE), wi[:, c1:c2], zeros(LANES - QK_HEAD),
                                wi[:, c2:]], axis=1)
    wkv = w_ukv[l].reshape(KV_LORA, MLA_HEADS, QK_NOPE + V_HEAD)
    w_uk = _pad_heads(wkv[:, :, :QK_NOPE].reshape(KV_LORA, MLA_HEADS * QK_NOPE), QK_NOPE)
    w_uv = wkv[:, :, QK_NOPE:].reshape(KV_LORA, MLA_WIDTH)
    pad_g = lambda g: jnp.pad(g, (0, LANES - QK_HEAD)).reshape(1, LANES)
    return {
        "g1": norm1_g[l].reshape(1, D_MODEL),
        "w_in": w_in_pad.astype(CDT),
        "qlg": q_lora_g[l].reshape(1, Q_LORA),
        "kvlg": kv_lora_g[l].reshape(1, KV_LORA),
        "w_uq": _pad_heads(w_uq[l], QK_HEAD).astype(CDT),
        "w_uk": w_uk.astype(CDT),
        "w_uv": w_uv.astype(CDT),
        "qg": pad_g(q_head_g[l] * (1.0 / math.sqrt(QK_HEAD))),
        "kg": pad_g(k_head_g[l]),
        "gate_bias": gate_bias[l].reshape(1, 2 * D_MODEL),
        "w_o_mla": w_o_mla[l].astype(CDT),
        "pool_w": pool_w[l].astype(CDT),
        "pool_scale": pool_scale[l].reshape(1, POOL_WIDTH),
        "w_pool_out": w_pool_out[l].astype(CDT),
        "w_out": w_out[l].astype(CDT),
        "g2": norm2_g[l].reshape(1, D_MODEL),
        "peer_wq": peer_wq[l].astype(CDT),
        "peer_keys": peer_keys[l].reshape(2 * PEER_HEADS, N_KEYS, PEER_HALF).astype(CDT),
        "peer_u": peer_u[l].astype(CDT),
        "peer_vt": peer_v[l].T.astype(CDT),
    }


def kernel(x, positions, norm1_g, w_in, q_lora_g, kv_lora_g, w_uq, w_ukv, q_head_g, k_head_g, w_o_mla,
           pool_w, pool_scale, w_pool_out, gate_bias, w_out, norm2_g, peer_wq, peer_keys, peer_u, peer_v):
    B, S, D = x.shape
    assert D == D_MODEL and S % LANES == 0
    T = B * S
    depth = norm1_g.shape[0]
    ropes = _rope_tables(positions)
    x2d = x.reshape(T, D)
    for l in range(depth):
        lw = _layer_weights(l, norm1_g, w_in, q_lora_g, kv_lora_g, w_uq, w_ukv, q_head_g, k_head_g,
                            w_o_mla, pool_w, pool_scale, w_pool_out, gate_bias, w_out, norm2_g,
                            peer_wq, peer_keys, peer_u, peer_v)
        q, k, v, p, gates = _in_proj(x2d, lw, ropes, tm=256)
        o = _attention(q, k, v, B, S, tq=256)
        y = _pool(p, lw, B, S)
        x2d, xnt, st = _out_proj(x2d, o, y, gates, lw, tm=256)
        r2, e2, cnt, c = _peer_topk(st, tmk=512)
        x2d = _peer_dense(x2d, xnt, r2, e2, cnt, c, lw["peer_u"], lw["peer_vt"], tm=512, eb=1024, tc=256)
    return x2d.reshape(B, S, D)
```

```python
import functools
import math

import jax
import jax.numpy as jnp
import numpy as np
from jax import lax
from jax.experimental import pallas as pl
from jax.experimental.pallas import tpu as pltpu

D_MODEL = 1024
MLA_HEADS = 8
Q_LORA = 384
KV_LORA = 256
QK_NOPE = 64
QK_ROPE = 32
QK_HEAD = QK_NOPE + QK_ROPE
V_HEAD = 64
MLA_WIDTH = MLA_HEADS * V_HEAD
ROPE_BASE = 10000.0
POOL_WINDOWS = (2, 4, 8, 16)
POOL_WIDTH = 512
POOL_GC = 128
PEER_HEADS = 8
N_KEYS = 128
N_EXPERTS = N_KEYS * N_KEYS
PEER_HALF = 128
PEER_TOPK = 16
RMS_EPS = 1e-6

LANES = 128
GATE_TILE_ROWS = 16
HEAD_PAD = LANES
IN_PAD = Q_LORA + KV_LORA + LANES + POOL_WIDTH + 2 * D_MODEL
VMEM_LIMIT = 48 * 1024 * 1024
PEER_DENSE_VMEM_LIMIT = 56 * 1024 * 1024

CDT = jnp.bfloat16
GDT = jnp.bfloat16
F32 = jnp.float32
NEG_INF = float("-inf")
_FLT_MAX = float(np.finfo(np.float32).max)
_FLT_TOP_ULP = 2.0 ** 104

_CAND_ROWS = 16 + 8 * 7 + 8
_FIDX_INVALID = 1 << 20


def _cparams(sem, vmem=VMEM_LIMIT, flags=None):
    return pltpu.CompilerParams(dimension_semantics=sem, vmem_limit_bytes=vmem, flags=flags)


def _rms(x, g):
    return x * lax.rsqrt(jnp.mean(x * x, axis=-1, keepdims=True) + RMS_EPS) * g


def _dot(a, b):
    return jnp.dot(a, b, preferred_element_type=F32)


def _dot_nt(a, b):
    return lax.dot_general(a, b, (((1,), (1,)), ((), ())), preferred_element_type=F32)


def _rope_kernel(pos_ref, inv_ref, c_ref, s1_ref, s2_ref):
    ang = pos_ref[...] * inv_ref[...]
    lane = lax.broadcasted_iota(jnp.int32, ang.shape, 1)
    cos = jnp.cos(ang)
    sin = jnp.sin(ang)
    half = QK_ROPE // 2
    c_ref[...] = jnp.where(lane < QK_NOPE, 1.0, jnp.where(lane < QK_HEAD, cos, 0.0))
    s1_ref[...] = jnp.where((lane >= QK_NOPE) & (lane < QK_NOPE + half), -sin, 0.0)
    s2_ref[...] = jnp.where((lane >= QK_NOPE + half) & (lane < QK_HEAD), sin, 0.0)


def _rope_tables(positions):
    T = positions.size
    tm = min(T, 2048)
    pos = jnp.broadcast_to(positions.reshape(T, 1).astype(F32), (T, LANES))
    inv = ROPE_BASE ** (-jnp.arange(0, QK_ROPE, 2, dtype=F32) / QK_ROPE)
    inv_pat = jnp.concatenate([jnp.zeros((QK_NOPE,), F32), inv, inv,
                               jnp.zeros((LANES - QK_HEAD,), F32)]).reshape(1, LANES)
    spec = pl.BlockSpec((tm, LANES), lambda i: (i, 0))
    return pl.pallas_call(
        _rope_kernel,
        grid=(T // tm,),
        in_specs=[spec, pl.BlockSpec((1, LANES), lambda i: (0, 0))],
        out_specs=[spec, spec, spec],
        out_shape=[jax.ShapeDtypeStruct((T, LANES), F32)] * 3,
        compiler_params=_cparams(("parallel",)),
        name="rope_tables",
    )(pos, inv_pat)


def _in_proj_kernel(x_ref, g1_ref, win_ref, qlg_ref, kvlg_ref, wuq_ref, wuk_ref, wuv_ref,
                    c_ref, s1_ref, s2_ref, qg_ref, kg_ref, bias_ref,
                    q_out, k_out, v_out, p_out, gate_out):
    xn = _rms(x_ref[...], g1_ref[...])
    y = _dot(xn.astype(CDT), win_ref[...])
    o = 0
    cq = y[:, o:o + Q_LORA]; o += Q_LORA
    ckv = y[:, o:o + KV_LORA]; o += KV_LORA
    kpe = y[:, o:o + LANES]; o += LANES
    p_out[...] = y[:, o:o + POOL_WIDTH]; o += POOL_WIDTH
    gate_out[...] = jax.nn.sigmoid(y[:, o:] + bias_ref[...]).astype(gate_out.dtype)

    cqn = _rms(cq, qlg_ref[...]).astype(CDT)
    ckvn = _rms(ckv, kvlg_ref[...]).astype(CDT)
    q_raw = _dot(cqn, wuq_ref[...])
    k_raw = _dot(ckvn, wuk_ref[...])
    v_out[...] = _dot(ckvn, wuv_ref[...]).astype(v_out.dtype)

    cpat = c_ref[...]
    s1pat = s1_ref[...]
    s2pat = s2_ref[...]
    half = QK_ROPE // 2

    def rope(xh):
        return (xh * cpat + pltpu.roll(xh, LANES - half, 1) * s1pat
                + pltpu.roll(xh, half, 1) * s2pat)

    def head_norm(xh, g):
        ms = jnp.sum(xh * xh, axis=-1, keepdims=True) * (1.0 / QK_HEAD)
        return xh * lax.rsqrt(ms + RMS_EPS) * g

    kpe_r = rope(kpe)
    qg = qg_ref[...]
    kg = kg_ref[...]
    for h in range(MLA_HEADS):
        sl = slice(h * HEAD_PAD, (h + 1) * HEAD_PAD)
        q_out[:, sl] = head_norm(rope(q_raw[:, sl]), qg).astype(q_out.dtype)
        k_out[:, sl] = head_norm(k_raw[:, sl] + kpe_r, kg).astype(k_out.dtype)


def _in_proj(x2d, lw, ropes, tm):
    T = x2d.shape[0]
    tm = min(tm, T)
    row = lambda w: pl.BlockSpec((tm, w), lambda i: (i, 0))
    full = lambda a: pl.BlockSpec(a.shape, lambda i: (0,) * a.ndim)
    consts = [lw["g1"], lw["w_in"], lw["qlg"], lw["kvlg"], lw["w_uq"], lw["w_uk"], lw["w_uv"]]
    tail = [lw["qg"], lw["kg"], lw["gate_bias"]]
    return pl.pallas_call(
        _in_proj_kernel,
        grid=(T // tm,),
        in_specs=[row(D_MODEL)] + [full(a) for a in consts] + [row(LANES)] * 3 + [full(a) for a in tail],
        out_specs=[row(MLA_HEADS * HEAD_PAD), row(MLA_HEADS * HEAD_PAD), row(MLA_WIDTH),
                   row(POOL_WIDTH), row(2 * D_MODEL)],
        out_shape=[jax.ShapeDtypeStruct((T, MLA_HEADS * HEAD_PAD), CDT),
                   jax.ShapeDtypeStruct((T, MLA_HEADS * HEAD_PAD), CDT),
                   jax.ShapeDtypeStruct((T, MLA_WIDTH), CDT),
                   jax.ShapeDtypeStruct((T, POOL_WIDTH), F32),
                   jax.ShapeDtypeStruct((T, 2 * D_MODEL), CDT)],
        compiler_params=_cparams(("parallel",)),
        name="in_proj",
    )(x2d, *consts, *ropes, *tail)


def _attn_kernel(q_ref, k_ref, v_ref, o_ref):
    outs = []
    for j in range(2):
        q = q_ref[:, j * HEAD_PAD:(j + 1) * HEAD_PAD]
        k = k_ref[:, j * HEAD_PAD:(j + 1) * HEAD_PAD]
        v = v_ref[:, j * V_HEAD:(j + 1) * V_HEAD]
        s = _dot_nt(q, k)
        m = jnp.max(s, axis=-1, keepdims=True)
        p = jnp.exp(s - m)
        l = jnp.sum(p, axis=-1, keepdims=True)
        outs.append(_dot(p.astype(CDT), v) / l)
    o_ref[...] = jnp.concatenate(outs, axis=-1).astype(o_ref.dtype)


def _attention(q, k, v, B, S, tq):
    tq = min(tq, S)
    nq = S // tq
    return pl.pallas_call(
        _attn_kernel,
        grid=(B, MLA_HEADS // 2, nq),
        in_specs=[pl.BlockSpec((tq, 2 * HEAD_PAD), lambda b, h, i: (b * nq + i, h)),
                  pl.BlockSpec((S, 2 * HEAD_PAD), lambda b, h, i: (b, h)),
                  pl.BlockSpec((S, 2 * V_HEAD), lambda b, h, i: (b, h))],
        out_specs=pl.BlockSpec((tq, 2 * V_HEAD), lambda b, h, i: (b * nq + i, h)),
        out_shape=jax.ShapeDtypeStruct((B * S, MLA_WIDTH), CDT),
        compiler_params=_cparams(("parallel", "parallel", "parallel")),
        name="attention",
    )(q, k, v)


def _pool_kernel(p_ref, pw_ref, ps_ref, y_ref):
    S = p_ref.shape[0]
    t = lax.broadcasted_iota(jnp.int32, (S, POOL_GC), 0)
    for g, w in enumerate(POOL_WINDOWS):
        half = w // 2
        sl = slice(g * POOL_GC, (g + 1) * POOL_GC)
        pg = p_ref[:, sl]
        acc = pg
        for d in range(-half, half):
            if d == 0:
                continue
            shifted = pltpu.roll(pg, (-d) % S, 0)
            valid = (t + d >= 0) & (t + d < S)
            acc = acc + jnp.where(valid, shifted, 0.0)
        count = (jnp.minimum(t + half, S) - jnp.maximum(t - half, 0)).astype(F32)
        mixed = acc / count - pg
        yg = _dot(mixed.astype(CDT), pw_ref[g]) * ps_ref[:, sl]
        y_ref[:, sl] = yg.astype(y_ref.dtype)


def _pool(p, lw, B, S):
    return pl.pallas_call(
        _pool_kernel,
        grid=(B,),
        in_specs=[pl.BlockSpec((S, POOL_WIDTH), lambda b: (b, 0)),
                  pl.BlockSpec(lw["pool_w"].shape, lambda b: (0, 0, 0)),
                  pl.BlockSpec((1, POOL_WIDTH), lambda b: (0, 0))],
        out_specs=pl.BlockSpec((S, POOL_WIDTH), lambda b: (b, 0)),
        out_shape=jax.ShapeDtypeStruct((B * S, POOL_WIDTH), CDT),
        compiler_params=_cparams(("parallel",)),
        name="pool",
    )(p, lw["pool_w"], lw["pool_scale"])


def _out_proj_kernel(x_ref, o_ref, y_ref, gate_ref, wo_ref, wpo_ref, wout_ref, g2_ref, wq_ref, keys_ref,
                     xnew_out, xnt_out, st_out):
    a = _dot(o_ref[...], wo_ref[...])
    bp = _dot(y_ref[...], wpo_ref[...])
    ga = gate_ref[:, :D_MODEL].astype(F32)
    gb = gate_ref[:, D_MODEL:].astype(F32)
    mix = ga * a + gb * bp
    xnew = x_ref[...] + _dot(mix.astype(CDT), wout_ref[...])
    xnew_out[...] = xnew
    xn = _rms(xnew, g2_ref[...])
    xnt_out[...] = xn.T.astype(xnt_out.dtype)
    qp = _dot(xn.astype(CDT), wq_ref[...])
    for hp in range(2 * PEER_HEADS):
        qh = qp[:, hp * PEER_HALF:(hp + 1) * PEER_HALF].astype(CDT)
        st_out[hp] = _dot_nt(keys_ref[hp], qh)


def _out_proj(x2d, o, y, gates, lw, tm):
    T = x2d.shape[0]
    tm = min(tm, T)
    row = lambda w: pl.BlockSpec((tm, w), lambda i: (i, 0))
    full = lambda a: pl.BlockSpec(a.shape, lambda i: (0,) * a.ndim)
    consts = [lw["w_o_mla"], lw["w_pool_out"], lw["w_out"], lw["g2"], lw["peer_wq"], lw["peer_keys"]]
    return pl.pallas_call(
        _out_proj_kernel,
        grid=(T // tm,),
        in_specs=[row(D_MODEL), row(MLA_WIDTH), row(POOL_WIDTH), row(2 * D_MODEL)] + [full(a) for a in consts],
        out_specs=[row(D_MODEL),
                   pl.BlockSpec((D_MODEL, tm), lambda i: (0, i)),
                   pl.BlockSpec((2 * PEER_HEADS, N_KEYS, tm), lambda i: (0, 0, i))],
        out_shape=[jax.ShapeDtypeStruct((T, D_MODEL), F32),
                   jax.ShapeDtypeStruct((D_MODEL, T), CDT),
                   jax.ShapeDtypeStruct((2 * PEER_HEADS, N_KEYS, T), F32)],
        compiler_params=_cparams(("parallel",)),
        name="out_proj",
    )(x2d, o, y, gates, *consts)


def _top16(s, key_iota, row16, exact):
    work = s
    rank = jnp.full(s.shape, float(PEER_TOPK), F32)
    tops = jnp.zeros((PEER_TOPK, s.shape[1]), F32)
    for r in range(PEER_TOPK):
        m = jnp.max(work, axis=0, keepdims=True)
        sel = work == m
        if exact:
            first = jnp.min(jnp.where(sel, key_iota, float(N_KEYS)), axis=0, keepdims=True)
            sel = key_iota == first
            rank = jnp.where(sel, float(r), rank)
            work = jnp.where(sel, NEG_INF, work)
        else:
            work = jnp.where(sel, -_FLT_MAX + r * _FLT_TOP_ULP, work)
        tops = jnp.where(row16 == r, m, tops)
    if not exact:
        rank = jnp.minimum((work + _FLT_MAX) * (1.0 / _FLT_TOP_ULP), float(PEER_TOPK))
    return rank, tops


def _cand_grid(ta, tb, combine):
    pieces = [combine(ta[0:1], tb)]
    for r1 in range(1, 8):
        pieces.append(combine(ta[r1:r1 + 1], tb[0:8]))
    pieces.append(combine(ta[8:16], tb[0:1]))
    return jnp.concatenate(pieces, axis=0)


def _count_true(mask):
    return jnp.sum(jnp.where(mask, 1.0, 0.0), axis=0, keepdims=True)


def _route(s1, s2, key_iota, row16, fidx, exact):
    rank1, ta = _top16(s1, key_iota, row16, exact)
    rank2, tb = _top16(s2, key_iota, row16, exact)
    a0 = ta[0:1]
    b0 = tb[0:1]
    cand0 = jnp.where(fidx < _FIDX_INVALID, _cand_grid(ta, tb, lambda a, b: a + b), NEG_INF)
    ecand = _cand_grid(jnp.exp(ta - a0), jnp.exp(tb - b0), lambda a, b: a * b)
    cand = cand0
    if exact:
        selected = jnp.zeros(cand.shape, F32)
        for _ in range(PEER_TOPK):
            m = jnp.max(cand, axis=0, keepdims=True)
            first = jnp.min(jnp.where(cand == m, fidx, float(_FIDX_INVALID)), axis=0, keepdims=True)
            sel = fidx == first
            selected = jnp.where(sel, 1.0, selected)
            cand = jnp.where(sel, NEG_INF, cand)
        bad = jnp.zeros_like(a0)
    else:
        for _ in range(PEER_TOPK):
            m = jnp.max(cand, axis=0, keepdims=True)
            cand = jnp.where(cand == m, NEG_INF, cand)
        selected = jnp.where(cand0 >= m, 1.0, 0.0)
        k = float(PEER_TOPK)
        bad = jnp.where((_count_true(rank1 < k) != k) | (_count_true(rank2 < k) != k)
                        | (jnp.sum(selected, axis=0, keepdims=True) != k), 1.0, 0.0)
    z = jnp.sum(selected * ecand, axis=0, keepdims=True)
    cntr = [jnp.sum(selected[0:16], axis=0, keepdims=True)]
    for r1 in range(1, 8):
        cntr.append(jnp.sum(selected[8 + 8 * r1:16 + 8 * r1], axis=0, keepdims=True))
    for r1 in range(8, 16):
        cntr.append(selected[64 + r1:65 + r1])
    cnt = jnp.zeros(s1.shape, F32)
    for r1 in range(PEER_TOPK):
        cnt = jnp.where(rank1 == float(r1), cntr[r1], cnt)
    return rank2, jnp.exp(s2 - b0), cnt, jnp.exp(s1 - a0) / z, bad


def _peer_topk_kernel(s_ref, fidx_ref, r2_out, e2_out, cnt_out, c_out, *, n_chunks):
    key_iota = lax.broadcasted_iota(jnp.int32, (N_KEYS, LANES), 0).astype(F32)
    row16 = lax.broadcasted_iota(jnp.int32, (PEER_TOPK, LANES), 0)
    fidx = fidx_ref[...]

    def body(it, carry):
        h = it // n_chunks
        off = pl.multiple_of((it % n_chunks) * LANES, LANES)
        s1 = s_ref[2 * h, :, pl.ds(off, LANES)]
        s2 = s_ref[2 * h + 1, :, pl.ds(off, LANES)]

        def store(r2, e2, cnt, c):
            r2_out[h, :, pl.ds(off, LANES)] = r2.astype(r2_out.dtype)
            e2_out[h, :, pl.ds(off, LANES)] = e2.astype(e2_out.dtype)
            cnt_out[h, :, pl.ds(off, LANES)] = cnt.astype(cnt_out.dtype)
            c_out[h, :, pl.ds(off, LANES)] = c.astype(c_out.dtype)

        *fast, bad = _route(s1, s2, key_iota, row16, fidx, exact=False)
        store(*fast)

        @pl.when(jnp.max(bad) > 0.0)
        def _():
            *slow, _ = _route(s1, s2, key_iota, row16, fidx, exact=True)
            store(*slow)

        return carry

    lax.fori_loop(0, PEER_HEADS * n_chunks, body, 0)


def _cand_fidx():
    rows = []
    rows += [0 * 16 + r2 for r2 in range(16)]
    for r1 in range(1, 8):
        n = PEER_TOPK // (r1 + 1)
        rows += [r1 * 16 + r2 if r2 < n else _FIDX_INVALID + r1 * 16 + r2 for r2 in range(8)]
    rows += [r1 * 16 for r1 in range(8, 16)]
    assert len(rows) == _CAND_ROWS
    return jnp.broadcast_to(jnp.asarray(rows, F32)[:, None], (_CAND_ROWS, LANES))


def _peer_topk(st, tmk):
    T = st.shape[-1]
    tmk = min(tmk, T)
    spec = pl.BlockSpec((PEER_HEADS, N_KEYS, tmk), lambda i: (0, 0, i))
    shp = jax.ShapeDtypeStruct((PEER_HEADS, N_KEYS, T), GDT)
    shp32 = jax.ShapeDtypeStruct((PEER_HEADS, N_KEYS, T), F32)
    return pl.pallas_call(
        functools.partial(_peer_topk_kernel, n_chunks=tmk // LANES),
        grid=(T // tmk,),
        in_specs=[pl.BlockSpec((2 * PEER_HEADS, N_KEYS, tmk), lambda i: (0, 0, i)),
                  pl.BlockSpec((_CAND_ROWS, LANES), lambda i: (0, 0))],
        out_specs=[spec] * 4,
        out_shape=[shp, shp, shp32, shp32],
        compiler_params=_cparams(("parallel",)),
        name="peer_topk",
    )(st, _cand_fidx())


def _peer_dense_kernel(x_ref, xnt_ref, r2_ref, e2_ref, cnt_ref, c_ref, u_ref, vt_ref, out_ref,
                       acc_ref, ht_ref, at_ref, bc_ref, *, eb, tc, nsub):
    e = pl.program_id(1)
    tm = xnt_ref.shape[1]
    jpb = eb // N_KEYS
    sub = eb // nsub
    pk = bc_ref.shape[-2]

    @pl.when(e == 0)
    def _():
        acc_ref[...] = jnp.zeros_like(acc_ref)

    def fill(h, carry):
        for jb in range(jpb):
            j = e * jpb + jb
            bc_ref[0, h, jb] = jnp.broadcast_to(cnt_ref[h, pl.ds(j, 1), :], (pk, tm)).astype(GDT)
            bc_ref[1, h, jb] = jnp.broadcast_to(c_ref[h, pl.ds(j, 1), :], (pk, tm)).astype(GDT)
        return carry

    lax.fori_loop(0, PEER_HEADS, fill, 0)

    for sb in range(nsub):
        rows = slice(sb * sub, (sb + 1) * sub)
        ht_ref[rows, :] = _dot(u_ref[rows, :], xnt_ref[...])
    for sb in range(nsub):
        rows = slice(sb * sub, (sb + 1) * sub)
        for jb in range(sb * sub // N_KEYS, (sb + 1) * sub // N_KEYS):
            krows = slice(jb * N_KEYS, (jb + 1) * N_KEYS)
            for ci in range(tm // tc):
                cols = slice(ci * tc, (ci + 1) * tc)
                g = jnp.zeros((N_KEYS // pk, pk, tc), GDT)
                for h in range(PEER_HEADS):
                    r2 = r2_ref[h, :, cols].reshape(N_KEYS // pk, pk, tc)
                    e2 = e2_ref[h, :, cols].reshape(N_KEYS // pk, pk, tc)
                    g = g + jnp.where(r2 < bc_ref[0, h, jb, :, cols], e2 * bc_ref[1, h, jb, :, cols], 0)
                a = g.reshape(N_KEYS, tc) * jax.nn.gelu(ht_ref[krows, cols]).astype(GDT)
                at_ref[krows, cols] = a.astype(at_ref.dtype)
        acc_ref[...] += _dot(vt_ref[:, rows], at_ref[rows, :])

    @pl.when(e == pl.num_programs(1) - 1)
    def _():
        out_ref[...] = x_ref[...] + acc_ref[...].T


def _peer_dense(x2d, xnt, r2, e2, cnt, c, u, vt, tm, eb, tc):
    T = x2d.shape[0]
    tm = min(tm, T)
    tc = min(tc, tm)
    tok3 = pl.BlockSpec((PEER_HEADS, N_KEYS, tm), lambda i, e: (0, 0, i))
    return pl.pallas_call(
        functools.partial(_peer_dense_kernel, eb=eb, tc=tc, nsub=eb // 256),
        grid=(T // tm, N_EXPERTS // eb),
        in_specs=[pl.BlockSpec((tm, D_MODEL), lambda i, e: (i, 0)),
                  pl.BlockSpec((D_MODEL, tm), lambda i, e: (0, i)),
                  tok3, tok3, tok3, tok3,
                  pl.BlockSpec((eb, D_MODEL), lambda i, e: (e, 0)),
                  pl.BlockSpec((D_MODEL, eb), lambda i, e: (0, e))],
        out_specs=pl.BlockSpec((tm, D_MODEL), lambda i, e: (i, 0)),
        out_shape=jax.ShapeDtypeStruct((T, D_MODEL), F32),
        scratch_shapes=[pltpu.VMEM((D_MODEL, tm), F32),
                        pltpu.VMEM((eb, tm), F32),
                        pltpu.VMEM((eb, tm), CDT),
                        pltpu.VMEM((2, PEER_HEADS, eb // N_KEYS, GATE_TILE_ROWS, tm), GDT)],
        compiler_params=_cparams(("parallel", "arbitrary"), vmem=PEER_DENSE_VMEM_LIMIT),
        name="peer_dense",
    )(x2d, xnt, r2, e2, cnt, c, u, vt)


def _pad_heads(w, width):
    r = w.shape[0]
    w = w.reshape(r, MLA_HEADS, width)
    w = jnp.pad(w, ((0, 0), (0, 0), (0, HEAD_PAD - width)))
    return w.reshape(r, MLA_HEADS * HEAD_PAD)


def _layer_weights(l, norm1_g, w_in, q_lora_g, kv_lora_g, w_uq, w_ukv, q_head_g, k_head_g, w_o_mla,
                   pool_w, pool_scale, w_pool_out, gate_bias, w_out, norm2_g, peer_wq, peer_keys,
                   peer_u, peer_v):
    wi = w_in[l]
    c0 = Q_LORA
    c1 = c0 + KV_LORA
    c2 = c1 + QK_ROPE
    c3 = c2 + POOL_WIDTH
    zeros = lambda n: jnp.zeros((D_MODEL, n), wi.dtype)
    w_in_pad = jnp.concatenate([wi[:, :c1], zeros(QK_NOPE), wi[:, c1:c2], zeros(LANES - QK_HEAD),
                                wi[:, c2:]], axis=1)
    wkv = w_ukv[l].reshape(KV_LORA, MLA_HEADS, QK_NOPE + V_HEAD)
    w_uk = _pad_heads(wkv[:, :, :QK_NOPE].reshape(KV_LORA, MLA_HEADS * QK_NOPE), QK_NOPE)
    w_uv = wkv[:, :, QK_NOPE:].reshape(KV_LORA, MLA_WIDTH)
    pad_g = lambda g: jnp.pad(g, (0, LANES - QK_HEAD)).reshape(1, LANES)
    return {
        "g1": norm1_g[l].reshape(1, D_MODEL),
        "w_in": w_in_pad.astype(CDT),
        "qlg": q_lora_g[l].reshape(1, Q_LORA),
        "kvlg": kv_lora_g[l].reshape(1, KV_LORA),
        "w_uq": _pad_heads(w_uq[l], QK_HEAD).astype(CDT),
        "w_uk": w_uk.astype(CDT),
        "w_uv": w_uv.astype(CDT),
        "qg": pad_g(q_head_g[l] * (1.0 / math.sqrt(QK_HEAD))),
        "kg": pad_g(k_head_g[l]),
        "gate_bias": gate_bias[l].reshape(1, 2 * D_MODEL),
        "w_o_mla": w_o_mla[l].astype(CDT),
        "pool_w": pool_w[l].astype(CDT),
        "pool_scale": pool_scale[l].reshape(1, POOL_WIDTH),
        "w_pool_out": w_pool_out[l].astype(CDT),
        "w_out": w_out[l].astype(CDT),
        "g2": norm2_g[l].reshape(1, D_MODEL),
        "peer_wq": peer_wq[l].astype(CDT),
        "peer_keys": peer_keys[l].reshape(2 * PEER_HEADS, N_KEYS, PEER_HALF).astype(CDT),
        "peer_u": peer_u[l].astype(CDT),
        "peer_vt": peer_v[l].T.astype(CDT),
    }


def kernel(x, positions, norm1_g, w_in, q_lora_g, kv_lora_g, w_uq, w_ukv, q_head_g, k_head_g, w_o_mla,
           pool_w, pool_scale, w_pool_out, gate_bias, w_out, norm2_g, peer_wq, peer_keys, peer_u, peer_v):
    B, S, D = x.shape
    assert D == D_MODEL and S % LANES == 0
    T = B * S
    depth = norm1_g.shape[0]
    ropes = _rope_tables(positions)
    x2d = x.reshape(T, D)
    for l in range(depth):
        lw = _layer_weights(l, norm1_g, w_in, q_lora_g, kv_lora_g, w_uq, w_ukv, q_head_g, k_head_g,
                            w_o_mla, pool_w, pool_scale, w_pool_out, gate_bias, w_out, norm2_g,
                            peer_wq, peer_keys, peer_u, peer_v)
        q, k, v, p, gates = _in_proj(x2d, lw, ropes, tm=256)
        o = _attention(q, k, v, B, S, tq=256)
        y = _pool(p, lw, B, S)
        x2d, xnt, st = _out_proj(x2d, o, y, gates, lw, tm=256)
        r2, e2, cnt, c = _peer_topk(st, tmk=512)
        x2d = _peer_dense(x2d, xnt, r2, e2, cnt, c, lw["peer_u"], lw["peer_vt"], tm=512, eb=2048, tc=256)
    return x2d.reshape(B, S, D)
```

```python
import functools
import math

import jax
import jax.numpy as jnp
import numpy as np
from jax import lax
from jax.experimental import pallas as pl
from jax.experimental.pallas import tpu as pltpu

D_MODEL = 1024
MLA_HEADS = 8
Q_LORA = 384
KV_LORA = 256
QK_NOPE = 64
QK_ROPE = 32
QK_HEAD = QK_NOPE + QK_ROPE
V_HEAD = 64
MLA_WIDTH = MLA_HEADS * V_HEAD
ROPE_BASE = 10000.0
POOL_WINDOWS = (2, 4, 8, 16)
POOL_WIDTH = 512
POOL_GC = 128
PEER_HEADS = 8
N_KEYS = 128
N_EXPERTS = N_KEYS * N_KEYS
PEER_HALF = 128
PEER_TOPK = 16
RMS_EPS = 1e-6

LANES = 128
GATE_TILE_ROWS = 16
ROUTE_CHUNK = 256
HEAD_PAD = LANES
IN_PAD = Q_LORA + KV_LORA + LANES + POOL_WIDTH + 2 * D_MODEL
VMEM_LIMIT = 48 * 1024 * 1024
PEER_DENSE_VMEM_LIMIT = 56 * 1024 * 1024

CDT = jnp.bfloat16
GDT = jnp.bfloat16
F32 = jnp.float32
NEG_INF = float("-inf")
_FLT_MAX = float(np.finfo(np.float32).max)
_FLT_TOP_ULP = 2.0 ** 104

_CAND_ROWS = 16 + 8 * 7 + 8
_FIDX_INVALID = 1 << 20


def _cparams(sem, vmem=VMEM_LIMIT, flags=None):
    return pltpu.CompilerParams(dimension_semantics=sem, vmem_limit_bytes=vmem, flags=flags)


def _rms(x, g):
    return x * lax.rsqrt(jnp.mean(x * x, axis=-1, keepdims=True) + RMS_EPS) * g


def _dot(a, b):
    return jnp.dot(a, b, preferred_element_type=F32)


def _dot_nt(a, b):
    return lax.dot_general(a, b, (((1,), (1,)), ((), ())), preferred_element_type=F32)


def _rope_kernel(pos_ref, inv_ref, c_ref, s1_ref, s2_ref):
    ang = pos_ref[...] * inv_ref[...]
    lane = lax.broadcasted_iota(jnp.int32, ang.shape, 1)
    cos = jnp.cos(ang)
    sin = jnp.sin(ang)
    half = QK_ROPE // 2
    c_ref[...] = jnp.where(lane < QK_NOPE, 1.0, jnp.where(lane < QK_HEAD, cos, 0.0))
    s1_ref[...] = jnp.where((lane >= QK_NOPE) & (lane < QK_NOPE + half), -sin, 0.0)
    s2_ref[...] = jnp.where((lane >= QK_NOPE + half) & (lane < QK_HEAD), sin, 0.0)


def _rope_tables(positions):
    T = positions.size
    tm = min(T, 2048)
    pos = jnp.broadcast_to(positions.reshape(T, 1).astype(F32), (T, LANES))
    inv = ROPE_BASE ** (-jnp.arange(0, QK_ROPE, 2, dtype=F32) / QK_ROPE)
    inv_pat = jnp.concatenate([jnp.zeros((QK_NOPE,), F32), inv, inv,
                               jnp.zeros((LANES - QK_HEAD,), F32)]).reshape(1, LANES)
    spec = pl.BlockSpec((tm, LANES), lambda i: (i, 0))
    return pl.pallas_call(
        _rope_kernel,
        grid=(T // tm,),
        in_specs=[spec, pl.BlockSpec((1, LANES), lambda i: (0, 0))],
        out_specs=[spec, spec, spec],
        out_shape=[jax.ShapeDtypeStruct((T, LANES), F32)] * 3,
        compiler_params=_cparams(("parallel",)),
        name="rope_tables",
    )(pos, inv_pat)


def _in_proj_kernel(x_ref, g1_ref, win_ref, qlg_ref, kvlg_ref, wuq_ref, wuk_ref, wuv_ref,
                    c_ref, s1_ref, s2_ref, qg_ref, kg_ref, bias_ref,
                    q_out, k_out, v_out, p_out, gate_out):
    xn = _rms(x_ref[...], g1_ref[...])
    y = _dot(xn.astype(CDT), win_ref[...])
    o = 0
    cq = y[:, o:o + Q_LORA]; o += Q_LORA
    ckv = y[:, o:o + KV_LORA]; o += KV_LORA
    kpe = y[:, o:o + LANES]; o += LANES
    p_out[...] = y[:, o:o + POOL_WIDTH]; o += POOL_WIDTH
    gate_out[...] = jax.nn.sigmoid(y[:, o:] + bias_ref[...]).astype(gate_out.dtype)

    cqn = _rms(cq, qlg_ref[...]).astype(CDT)
    ckvn = _rms(ckv, kvlg_ref[...]).astype(CDT)
    q_raw = _dot(cqn, wuq_ref[...])
    k_raw = _dot(ckvn, wuk_ref[...])
    v_out[...] = _dot(ckvn, wuv_ref[...]).astype(v_out.dtype)

    cpat = c_ref[...]
    s1pat = s1_ref[...]
    s2pat = s2_ref[...]
    half = QK_ROPE // 2

    def rope(xh):
        return (xh * cpat + pltpu.roll(xh, LANES - half, 1) * s1pat
                + pltpu.roll(xh, half, 1) * s2pat)

    def head_norm(xh, g):
        ms = jnp.sum(xh * xh, axis=-1, keepdims=True) * (1.0 / QK_HEAD)
        return xh * lax.rsqrt(ms + RMS_EPS) * g

    kpe_r = rope(kpe)
    qg = qg_ref[...]
    kg = kg_ref[...]
    for h in range(MLA_HEADS):
        sl = slice(h * HEAD_PAD, (h + 1) * HEAD_PAD)
        q_out[:, sl] = head_norm(rope(q_raw[:, sl]), qg).astype(q_out.dtype)
        k_out[:, sl] = head_norm(k_raw[:, sl] + kpe_r, kg).astype(k_out.dtype)


def _in_proj(x2d, lw, ropes, tm):
    T = x2d.shape[0]
    tm = min(tm, T)
    row = lambda w: pl.BlockSpec((tm, w), lambda i: (i, 0))
    full = lambda a: pl.BlockSpec(a.shape, lambda i: (0,) * a.ndim)
    consts = [lw["g1"], lw["w_in"], lw["qlg"], lw["kvlg"], lw["w_uq"], lw["w_uk"], lw["w_uv"]]
    tail = [lw["qg"], lw["kg"], lw["gate_bias"]]
    return pl.pallas_call(
        _in_proj_kernel,
        grid=(T // tm,),
        in_specs=[row(D_MODEL)] + [full(a) for a in consts] + [row(LANES)] * 3 + [full(a) for a in tail],
        out_specs=[row(MLA_HEADS * HEAD_PAD), row(MLA_HEADS * HEAD_PAD), row(MLA_WIDTH),
                   row(POOL_WIDTH), row(2 * D_MODEL)],
        out_shape=[jax.ShapeDtypeStruct((T, MLA_HEADS * HEAD_PAD), CDT),
                   jax.ShapeDtypeStruct((T, MLA_HEADS * HEAD_PAD), CDT),
                   jax.ShapeDtypeStruct((T, MLA_WIDTH), CDT),
                   jax.ShapeDtypeStruct((T, POOL_WIDTH), F32),
                   jax.ShapeDtypeStruct((T, 2 * D_MODEL), CDT)],
        compiler_params=_cparams(("parallel",)),
        name="in_proj",
    )(x2d, *consts, *ropes, *tail)


def _attn_kernel(q_ref, k_ref, v_ref, o_ref):
    outs = []
    for j in range(2):
        q = q_ref[:, j * HEAD_PAD:(j + 1) * HEAD_PAD]
        k = k_ref[:, j * HEAD_PAD:(j + 1) * HEAD_PAD]
        v = v_ref[:, j * V_HEAD:(j + 1) * V_HEAD]
        s = _dot_nt(q, k)
        m = jnp.max(s, axis=-1, keepdims=True)
        p = jnp.exp(s - m)
        l = jnp.sum(p, axis=-1, keepdims=True)
        outs.append(_dot(p.astype(CDT), v) / l)
    o_ref[...] = jnp.concatenate(outs, axis=-1).astype(o_ref.dtype)


def _attention(q, k, v, B, S, tq):
    tq = min(tq, S)
    nq = S // tq
    return pl.pallas_call(
        _attn_kernel,
        grid=(B, MLA_HEADS // 2, nq),
        in_specs=[pl.BlockSpec((tq, 2 * HEAD_PAD), lambda b, h, i: (b * nq + i, h)),
                  pl.BlockSpec((S, 2 * HEAD_PAD), lambda b, h, i: (b, h)),
                  pl.BlockSpec((S, 2 * V_HEAD), lambda b, h, i: (b, h))],
        out_specs=pl.BlockSpec((tq, 2 * V_HEAD), lambda b, h, i: (b * nq + i, h)),
        out_shape=jax.ShapeDtypeStruct((B * S, MLA_WIDTH), CDT),
        compiler_params=_cparams(("parallel", "parallel", "parallel")),
        name="attention",
    )(q, k, v)


def _pool_kernel(p_ref, pw_ref, ps_ref, y_ref):
    S = p_ref.shape[0]
    t = lax.broadcasted_iota(jnp.int32, (S, POOL_GC), 0)
    for g, w in enumerate(POOL_WINDOWS):
        half = w // 2
        sl = slice(g * POOL_GC, (g + 1) * POOL_GC)
        pg = p_ref[:, sl]
        acc = pg
        for d in range(-half, half):
            if d == 0:
                continue
            shifted = pltpu.roll(pg, (-d) % S, 0)
            valid = (t + d >= 0) & (t + d < S)
            acc = acc + jnp.where(valid, shifted, 0.0)
        count = (jnp.minimum(t + half, S) - jnp.maximum(t - half, 0)).astype(F32)
        mixed = acc / count - pg
        yg = _dot(mixed.astype(CDT), pw_ref[g]) * ps_ref[:, sl]
        y_ref[:, sl] = yg.astype(y_ref.dtype)


def _pool(p, lw, B, S):
    return pl.pallas_call(
        _pool_kernel,
        grid=(B,),
        in_specs=[pl.BlockSpec((S, POOL_WIDTH), lambda b: (b, 0)),
                  pl.BlockSpec(lw["pool_w"].shape, lambda b: (0, 0, 0)),
                  pl.BlockSpec((1, POOL_WIDTH), lambda b: (0, 0))],
        out_specs=pl.BlockSpec((S, POOL_WIDTH), lambda b: (b, 0)),
        out_shape=jax.ShapeDtypeStruct((B * S, POOL_WIDTH), CDT),
        compiler_params=_cparams(("parallel",)),
        name="pool",
    )(p, lw["pool_w"], lw["pool_scale"])


def _out_proj_kernel(x_ref, o_ref, y_ref, gate_ref, wo_ref, wpo_ref, wout_ref, g2_ref, wq_ref, keys_ref,
                     xnew_out, xnt_out, st_out):
    a = _dot(o_ref[...], wo_ref[...])
    bp = _dot(y_ref[...], wpo_ref[...])
    ga = gate_ref[:, :D_MODEL].astype(F32)
    gb = gate_ref[:, D_MODEL:].astype(F32)
    mix = ga * a + gb * bp
    xnew = x_ref[...] + _dot(mix.astype(CDT), wout_ref[...])
    xnew_out[...] = xnew
    xn = _rms(xnew, g2_ref[...])
    xnt_out[...] = xn.T.astype(xnt_out.dtype)
    qp = _dot(xn.astype(CDT), wq_ref[...])
    for hp in range(2 * PEER_HEADS):
        qh = qp[:, hp * PEER_HALF:(hp + 1) * PEER_HALF].astype(CDT)
        st_out[hp] = _dot_nt(keys_ref[hp], qh)


def _out_proj(x2d, o, y, gates, lw, tm):
    T = x2d.shape[0]
    tm = min(tm, T)
    row = lambda w: pl.BlockSpec((tm, w), lambda i: (i, 0))
    full = lambda a: pl.BlockSpec(a.shape, lambda i: (0,) * a.ndim)
    consts = [lw["w_o_mla"], lw["w_pool_out"], lw["w_out"], lw["g2"], lw["peer_wq"], lw["peer_keys"]]
    return pl.pallas_call(
        _out_proj_kernel,
        grid=(T // tm,),
        in_specs=[row(D_MODEL), row(MLA_WIDTH), row(POOL_WIDTH), row(2 * D_MODEL)] + [full(a) for a in consts],
        out_specs=[row(D_MODEL),
                   pl.BlockSpec((D_MODEL, tm), lambda i: (0, i)),
                   pl.BlockSpec((2 * PEER_HEADS, N_KEYS, tm), lambda i: (0, 0, i))],
        out_shape=[jax.ShapeDtypeStruct((T, D_MODEL), F32),
                   jax.ShapeDtypeStruct((D_MODEL, T), CDT),
                   jax.ShapeDtypeStruct((2 * PEER_HEADS, N_KEYS, T), F32)],
        compiler_params=_cparams(("parallel",)),
        name="out_proj",
    )(x2d, o, y, gates, *consts)


def _top16(s, key_iota, row16, exact):
    work = s
    rank = jnp.full(s.shape, float(PEER_TOPK), F32)
    tops = jnp.zeros((PEER_TOPK, s.shape[1]), F32)
    for r in range(PEER_TOPK):
        m = jnp.max(work, axis=0, keepdims=True)
        sel = work == m
        if exact:
            first = jnp.min(jnp.where(sel, key_iota, float(N_KEYS)), axis=0, keepdims=True)
            sel = key_iota == first
            rank = jnp.where(sel, float(r), rank)
            work = jnp.where(sel, NEG_INF, work)
        else:
            work = jnp.where(sel, -_FLT_MAX + r * _FLT_TOP_ULP, work)
        tops = jnp.where(row16 == r, m, tops)
    if not exact:
        rank = jnp.minimum((work + _FLT_MAX) * (1.0 / _FLT_TOP_ULP), float(PEER_TOPK))
    return rank, tops


def _cand_grid(ta, tb, combine):
    pieces = [combine(ta[0:1], tb)]
    for r1 in range(1, 8):
        pieces.append(combine(ta[r1:r1 + 1], tb[0:8]))
    pieces.append(combine(ta[8:16], tb[0:1]))
    return jnp.concatenate(pieces, axis=0)


def _count_true(mask):
    return jnp.sum(jnp.where(mask, 1.0, 0.0), axis=0, keepdims=True)


def _route(s1, s2, key_iota, row16, fidx, exact):
    rank1, ta = _top16(s1, key_iota, row16, exact)
    rank2, tb = _top16(s2, key_iota, row16, exact)
    a0 = ta[0:1]
    b0 = tb[0:1]
    cand0 = jnp.where(fidx < _FIDX_INVALID, _cand_grid(ta, tb, lambda a, b: a + b), NEG_INF)
    ecand = _cand_grid(jnp.exp(ta - a0), jnp.exp(tb - b0), lambda a, b: a * b)
    cand = cand0
    if exact:
        selected = jnp.zeros(cand.shape, F32)
        for _ in range(PEER_TOPK):
            m = jnp.max(cand, axis=0, keepdims=True)
            first = jnp.min(jnp.where(cand == m, fidx, float(_FIDX_INVALID)), axis=0, keepdims=True)
            sel = fidx == first
            selected = jnp.where(sel, 1.0, selected)
            cand = jnp.where(sel, NEG_INF, cand)
        bad = jnp.zeros_like(a0)
    else:
        for _ in range(PEER_TOPK):
            m = jnp.max(cand, axis=0, keepdims=True)
            cand = jnp.where(cand == m, NEG_INF, cand)
        selected = jnp.where(cand0 >= m, 1.0, 0.0)
        k = float(PEER_TOPK)
        bad = jnp.where((_count_true(rank1 < k) != k) | (_count_true(rank2 < k) != k)
                        | (jnp.sum(selected, axis=0, keepdims=True) != k), 1.0, 0.0)
    z = jnp.sum(selected * ecand, axis=0, keepdims=True)
    cntr = [jnp.sum(selected[0:16], axis=0, keepdims=True)]
    for r1 in range(1, 8):
        cntr.append(jnp.sum(selected[8 + 8 * r1:16 + 8 * r1], axis=0, keepdims=True))
    for r1 in range(8, 16):
        cntr.append(selected[64 + r1:65 + r1])
    cnt = jnp.zeros(s1.shape, F32)
    for r1 in range(PEER_TOPK):
        cnt = jnp.where(rank1 == float(r1), cntr[r1], cnt)
    return rank2, jnp.exp(s2 - b0), cnt, jnp.exp(s1 - a0) / z, bad


def _peer_topk_kernel(s_ref, fidx_ref, r2_out, e2_out, cnt_out, c_out, *, n_chunks):
    key_iota = lax.broadcasted_iota(jnp.int32, (N_KEYS, ROUTE_CHUNK), 0).astype(F32)
    row16 = lax.broadcasted_iota(jnp.int32, (PEER_TOPK, ROUTE_CHUNK), 0)
    fidx = fidx_ref[...]

    def body(it, carry):
        h = it // n_chunks
        cols = pl.ds(pl.multiple_of((it % n_chunks) * ROUTE_CHUNK, ROUTE_CHUNK), ROUTE_CHUNK)
        s1 = s_ref[2 * h, :, cols]
        s2 = s_ref[2 * h + 1, :, cols]

        def store(r2, e2, cnt, c):
            r2_out[h, :, cols] = r2.astype(r2_out.dtype)
            e2_out[h, :, cols] = e2.astype(e2_out.dtype)
            cnt_out[h, :, cols] = cnt.astype(cnt_out.dtype)
            c_out[h, :, cols] = c.astype(c_out.dtype)

        *fast, bad = _route(s1, s2, key_iota, row16, fidx, exact=False)
        store(*fast)

        @pl.when(jnp.max(bad) > 0.0)
        def _():
            *slow, _ = _route(s1, s2, key_iota, row16, fidx, exact=True)
            store(*slow)

        return carry

    lax.fori_loop(0, PEER_HEADS * n_chunks, body, 0)


def _cand_fidx():
    rows = []
    rows += [0 * 16 + r2 for r2 in range(16)]
    for r1 in range(1, 8):
        n = PEER_TOPK // (r1 + 1)
        rows += [r1 * 16 + r2 if r2 < n else _FIDX_INVALID + r1 * 16 + r2 for r2 in range(8)]
    rows += [r1 * 16 for r1 in range(8, 16)]
    assert len(rows) == _CAND_ROWS
    return jnp.broadcast_to(jnp.asarray(rows, F32)[:, None], (_CAND_ROWS, ROUTE_CHUNK))


def _peer_topk(st, tmk):
    T = st.shape[-1]
    tmk = min(tmk, T)
    spec = pl.BlockSpec((PEER_HEADS, N_KEYS, tmk), lambda i: (0, 0, i))
    shp = jax.ShapeDtypeStruct((PEER_HEADS, N_KEYS, T), GDT)
    shp32 = jax.ShapeDtypeStruct((PEER_HEADS, N_KEYS, T), F32)
    return pl.pallas_call(
        functools.partial(_peer_topk_kernel, n_chunks=tmk // ROUTE_CHUNK),
        grid=(T // tmk,),
        in_specs=[pl.BlockSpec((2 * PEER_HEADS, N_KEYS, tmk), lambda i: (0, 0, i)),
                  pl.BlockSpec((_CAND_ROWS, ROUTE_CHUNK), lambda i: (0, 0))],
        out_specs=[spec] * 4,
        out_shape=[shp, shp, shp32, shp32],
        compiler_params=_cparams(("parallel",)),
        name="peer_topk",
    )(st, _cand_fidx())


def _peer_dense_kernel(x_ref, xnt_ref, r2_ref, e2_ref, cnt_ref, c_ref, u_ref, vt_ref, out_ref,
                       acc_ref, ht_ref, at_ref, bc_ref, *, eb, tc, nsub):
    e = pl.program_id(1)
    tm = xnt_ref.shape[1]
    jpb = eb // N_KEYS
    sub = eb // nsub
    pk = bc_ref.shape[-2]

    @pl.when(e == 0)
    def _():
        acc_ref[...] = jnp.zeros_like(acc_ref)

    def fill(h, carry):
        for jb in range(jpb):
            j = e * jpb + jb
            bc_ref[0, h, jb] = jnp.broadcast_to(cnt_ref[h, pl.ds(j, 1), :], (pk, tm)).astype(GDT)
            bc_ref[1, h, jb] = jnp.broadcast_to(c_ref[h, pl.ds(j, 1), :], (pk, tm)).astype(GDT)
        return carry

    lax.fori_loop(0, PEER_HEADS, fill, 0)

    for sb in range(nsub):
        rows = slice(sb * sub, (sb + 1) * sub)
        ht_ref[rows, :] = _dot(u_ref[rows, :], xnt_ref[...])
    for sb in range(nsub):
        rows = slice(sb * sub, (sb + 1) * sub)
        for jb in range(sb * sub // N_KEYS, (sb + 1) * sub // N_KEYS):
            krows = slice(jb * N_KEYS, (jb + 1) * N_KEYS)
            for ci in range(tm // tc):
                cols = slice(ci * tc, (ci + 1) * tc)
                g = jnp.zeros((N_KEYS // pk, pk, tc), GDT)
                for h in range(PEER_HEADS):
                    r2 = r2_ref[h, :, cols].reshape(N_KEYS // pk, pk, tc)
                    e2 = e2_ref[h, :, cols].reshape(N_KEYS // pk, pk, tc)
                    g = g + jnp.where(r2 < bc_ref[0, h, jb, :, cols], e2 * bc_ref[1, h, jb, :, cols], 0)
                a = g.reshape(N_KEYS, tc) * jax.nn.gelu(ht_ref[krows, cols]).astype(GDT)
                at_ref[krows, cols] = a.astype(at_ref.dtype)
        acc_ref[...] += _dot(vt_ref[:, rows], at_ref[rows, :])

    @pl.when(e == pl.num_programs(1) - 1)
    def _():
        out_ref[...] = x_ref[...] + acc_ref[...].T


def _peer_dense(x2d, xnt, r2, e2, cnt, c, u, vt, tm, eb, tc):
    T = x2d.shape[0]
    tm = min(tm, T)
    tc = min(tc, tm)
    tok3 = pl.BlockSpec((PEER_HEADS, N_KEYS, tm), lambda i, e: (0, 0, i))
    return pl.pallas_call(
        functools.partial(_peer_dense_kernel, eb=eb, tc=tc, nsub=eb // 256),
        grid=(T // tm, N_EXPERTS // eb),
        in_specs=[pl.BlockSpec((tm, D_MODEL), lambda i, e: (i, 0)),
                  pl.BlockSpec((D_MODEL, tm), lambda i, e: (0, i)),
                  tok3, tok3, tok3, tok3,
                  pl.BlockSpec((eb, D_MODEL), lambda i, e: (e, 0)),
                  pl.BlockSpec((D_MODEL, eb), lambda i, e: (0, e))],
        out_specs=pl.BlockSpec((tm, D_MODEL), lambda i, e: (i, 0)),
        out_shape=jax.ShapeDtypeStruct((T, D_MODEL), F32),
        scratch_shapes=[pltpu.VMEM((D_MODEL, tm), F32),
                        pltpu.VMEM((eb, tm), F32),
                        pltpu.VMEM((eb, tm), CDT),
                        pltpu.VMEM((2, PEER_HEADS, eb // N_KEYS, GATE_TILE_ROWS, tm), GDT)],
        compiler_params=_cparams(("parallel", "arbitrary"), vmem=PEER_DENSE_VMEM_LIMIT),
        name="peer_dense",
    )(x2d, xnt, r2, e2, cnt, c, u, vt)


def _pad_heads(w, width):
    r = w.shape[0]
    w = w.reshape(r, MLA_HEADS, width)
    w = jnp.pad(w, ((0, 0), (0, 0), (0, HEAD_PAD - width)))
    return w.reshape(r, MLA_HEADS * HEAD_PAD)


def _layer_weights(l, norm1_g, w_in, q_lora_g, kv_lora_g, w_uq, w_ukv, q_head_g, k_head_g, w_o_mla,
                   pool_w, pool_scale, w_pool_out, gate_bias, w_out, norm2_g, peer_wq, peer_keys,
                   peer_u, peer_v):
    wi = w_in[l]
    c0 = Q_LORA
    c1 = c0 + KV_LORA
    c2 = c1 + QK_ROPE
    c3 = c2 + POOL_WIDTH
    zeros = lambda n: jnp.zeros((D_MODEL, n), wi.dtype)
    w_in_pad = jnp.concatenate([wi[:, :c1], zeros(QK_NOPE), wi[:, c1:c2], zeros(LANES - QK_HEAD),
                                wi[:, c2:]], axis=1)
    wkv = w_ukv[l].reshape(KV_LORA, MLA_HEADS, QK_NOPE + V_HEAD)
    w_uk = _pad_heads(wkv[:, :, :QK_NOPE].reshape(KV_LORA, MLA_HEADS * QK_NOPE), QK_NOPE)
    w_uv = wkv[:, :, QK_NOPE:].reshape(KV_LORA, MLA_WIDTH)
    pad_g = lambda g: jnp.pad(g, (0, LANES - QK_HEAD)).reshape(1, LANES)
    return {
        "g1": norm1_g[l].reshape(1, D_MODEL),
        "w_in": w_in_pad.astype(CDT),
        "qlg": q_lora_g[l].reshape(1, Q_LORA),
        "kvlg": kv_lora_g[l].reshape(1, KV_LORA),
        "w_uq": _pad_heads(w_uq[l], QK_HEAD).astype(CDT),
        "w_uk": w_uk.astype(CDT),
        "w_uv": w_uv.astype(CDT),
        "qg": pad_g(q_head_g[l] * (1.0 / math.sqrt(QK_HEAD))),
        "kg": pad_g(k_head_g[l]),
        "gate_bias": gate_bias[l].reshape(1, 2 * D_MODEL),
        "w_o_mla": w_o_mla[l].astype(CDT),
        "pool_w": pool_w[l].astype(CDT),
        "pool_scale": pool_scale[l].reshape(1, POOL_WIDTH),
        "w_pool_out": w_pool_out[l].astype(CDT),
        "w_out": w_out[l].astype(CDT),
        "g2": norm2_g[l].reshape(1, D_MODEL),
        "peer_wq": peer_wq[l].astype(CDT),
        "peer_keys": peer_keys[l].reshape(2 * PEER_HEADS, N_KEYS, PEER_HALF).astype(CDT),
        "peer_u": peer_u[l].astype(CDT),
        "peer_vt": peer_v[l].T.astype(CDT),
    }


def kernel(x, positions, norm1_g, w_in, q_lora_g, kv_lora_g, w_uq, w_ukv, q_head_g, k_head_g, w_o_mla,
           pool_w, pool_scale, w_pool_out, gate_bias, w_out, norm2_g, peer_wq, peer_keys, peer_u, peer_v):
    B, S, D = x.shape
    assert D == D_MODEL and S % LANES == 0
    T = B * S
    depth = norm1_g.shape[0]
    ropes = _rope_tables(positions)
    x2d = x.reshape(T, D)
    for l in range(depth):
        lw = _layer_weights(l, norm1_g, w_in, q_lora_g, kv_lora_g, w_uq, w_ukv, q_head_g, k_head_g,
                            w_o_mla, pool_w, pool_scale, w_pool_out, gate_bias, w_out, norm2_g,
                            peer_wq, peer_keys, peer_u, peer_v)
        q, k, v, p, gates = _in_proj(x2d, lw, ropes, tm=256)
        o = _attention(q, k, v, B, S, tq=256)
        y = _pool(p, lw, B, S)
        x2d, xnt, st = _out_proj(x2d, o, y, gates, lw, tm=256)
        r2, e2, cnt, c = _peer_topk(st, tmk=512)
        x2d = _peer_dense(x2d, xnt, r2, e2, cnt, c, lw["peer_u"], lw["peer_vt"], tm=512, eb=2048, tc=256)
    return x2d.reshape(B, S, D)
```

```python
import functools
import math

import jax
import jax.numpy as jnp
import numpy as np
from jax import lax
from jax.experimental import pallas as pl
from jax.experimental.pallas import tpu as pltpu

D_MODEL = 1024
MLA_HEADS = 8
Q_LORA = 384
KV_LORA = 256
QK_NOPE = 64
QK_ROPE = 32
QK_HEAD = QK_NOPE + QK_ROPE
V_HEAD = 64
MLA_WIDTH = MLA_HEADS * V_HEAD
ROPE_BASE = 10000.0
POOL_WINDOWS = (2, 4, 8, 16)
POOL_WIDTH = 512
POOL_GC = 128
PEER_HEADS = 8
N_KEYS = 128
N_EXPERTS = N_KEYS * N_KEYS
PEER_HALF = 128
PEER_TOPK = 16
RMS_EPS = 1e-6

LANES = 128
GATE_TILE_ROWS = 16
ROUTE_CHUNK = 256
HEAD_PAD = LANES
IN_PAD = Q_LORA + KV_LORA + LANES + POOL_WIDTH + 2 * D_MODEL
VMEM_LIMIT = 48 * 1024 * 1024
PEER_DENSE_VMEM_LIMIT = 56 * 1024 * 1024

CDT = jnp.bfloat16
GDT = jnp.bfloat16
F32 = jnp.float32
NEG_INF = float("-inf")
_FLT_MAX = float(np.finfo(np.float32).max)
_FLT_TOP_ULP = 2.0 ** 104

_CAND_ROWS = 16 + 8 * 7 + 8
_FIDX_INVALID = 1 << 20


def _cparams(sem, vmem=VMEM_LIMIT, flags=None):
    return pltpu.CompilerParams(dimension_semantics=sem, vmem_limit_bytes=vmem, flags=flags)


def _rms(x, g):
    return x * lax.rsqrt(jnp.mean(x * x, axis=-1, keepdims=True) + RMS_EPS) * g


def _dot(a, b):
    return jnp.dot(a, b, preferred_element_type=F32)


def _dot_nt(a, b):
    return lax.dot_general(a, b, (((1,), (1,)), ((), ())), preferred_element_type=F32)


def _rope_kernel(pos_ref, inv_ref, c_ref, s1_ref, s2_ref):
    ang = pos_ref[...] * inv_ref[...]
    lane = lax.broadcasted_iota(jnp.int32, ang.shape, 1)
    cos = jnp.cos(ang)
    sin = jnp.sin(ang)
    half = QK_ROPE // 2
    c_ref[...] = jnp.where(lane < QK_NOPE, 1.0, jnp.where(lane < QK_HEAD, cos, 0.0))
    s1_ref[...] = jnp.where((lane >= QK_NOPE) & (lane < QK_NOPE + half), -sin, 0.0)
    s2_ref[...] = jnp.where((lane >= QK_NOPE + half) & (lane < QK_HEAD), sin, 0.0)


def _rope_tables(positions):
    T = positions.size
    tm = min(T, 2048)
    pos = jnp.broadcast_to(positions.reshape(T, 1).astype(F32), (T, LANES))
    inv = ROPE_BASE ** (-jnp.arange(0, QK_ROPE, 2, dtype=F32) / QK_ROPE)
    inv_pat = jnp.concatenate([jnp.zeros((QK_NOPE,), F32), inv, inv,
                               jnp.zeros((LANES - QK_HEAD,), F32)]).reshape(1, LANES)
    spec = pl.BlockSpec((tm, LANES), lambda i: (i, 0))
    return pl.pallas_call(
        _rope_kernel,
        grid=(T // tm,),
        in_specs=[spec, pl.BlockSpec((1, LANES), lambda i: (0, 0))],
        out_specs=[spec, spec, spec],
        out_shape=[jax.ShapeDtypeStruct((T, LANES), F32)] * 3,
        compiler_params=_cparams(("parallel",)),
        name="rope_tables",
    )(pos, inv_pat)


def _in_proj_kernel(x_ref, g1_ref, win_ref, qlg_ref, kvlg_ref, wuq_ref, wuk_ref, wuv_ref,
                    c_ref, s1_ref, s2_ref, qg_ref, kg_ref, bias_ref,
                    q_out, k_out, v_out, p_out, gate_out):
    xn = _rms(x_ref[...], g1_ref[...])
    y = _dot(xn.astype(CDT), win_ref[...])
    o = 0
    cq = y[:, o:o + Q_LORA]; o += Q_LORA
    ckv = y[:, o:o + KV_LORA]; o += KV_LORA
    kpe = y[:, o:o + LANES]; o += LANES
    p_out[...] = y[:, o:o + POOL_WIDTH]; o += POOL_WIDTH
    gate_out[...] = jax.nn.sigmoid(y[:, o:] + bias_ref[...]).astype(gate_out.dtype)

    cqn = _rms(cq, qlg_ref[...]).astype(CDT)
    ckvn = _rms(ckv, kvlg_ref[...]).astype(CDT)
    q_raw = _dot(cqn, wuq_ref[...])
    k_raw = _dot(ckvn, wuk_ref[...])
    v_out[...] = _dot(ckvn, wuv_ref[...]).astype(v_out.dtype)

    cpat = c_ref[...]
    s1pat = s1_ref[...]
    s2pat = s2_ref[...]
    half = QK_ROPE // 2

    def rope(xh):
        return (xh * cpat + pltpu.roll(xh, LANES - half, 1) * s1pat
                + pltpu.roll(xh, half, 1) * s2pat)

    def head_norm(xh, g):
        ms = jnp.sum(xh * xh, axis=-1, keepdims=True) * (1.0 / QK_HEAD)
        return xh * lax.rsqrt(ms + RMS_EPS) * g

    kpe_r = rope(kpe)
    qg = qg_ref[...]
    kg = kg_ref[...]
    for h in range(MLA_HEADS):
        sl = slice(h * HEAD_PAD, (h + 1) * HEAD_PAD)
        q_out[:, sl] = head_norm(rope(q_raw[:, sl]), qg).astype(q_out.dtype)
        k_out[:, sl] = head_norm(k_raw[:, sl] + kpe_r, kg).astype(k_out.dtype)


def _in_proj(x2d, lw, ropes, tm):
    T = x2d.shape[0]
    tm = min(tm, T)
    row = lambda w: pl.BlockSpec((tm, w), lambda i: (i, 0))
    full = lambda a: pl.BlockSpec(a.shape, lambda i: (0,) * a.ndim)
    consts = [lw["g1"], lw["w_in"], lw["qlg"], lw["kvlg"], lw["w_uq"], lw["w_uk"], lw["w_uv"]]
    tail = [lw["qg"], lw["kg"], lw["gate_bias"]]
    return pl.pallas_call(
        _in_proj_kernel,
        grid=(T // tm,),
        in_specs=[row(D_MODEL)] + [full(a) for a in consts] + [row(LANES)] * 3 + [full(a) for a in tail],
        out_specs=[row(MLA_HEADS * HEAD_PAD), row(MLA_HEADS * HEAD_PAD), row(MLA_WIDTH),
                   row(POOL_WIDTH), row(2 * D_MODEL)],
        out_shape=[jax.ShapeDtypeStruct((T, MLA_HEADS * HEAD_PAD), CDT),
                   jax.ShapeDtypeStruct((T, MLA_HEADS * HEAD_PAD), CDT),
                   jax.ShapeDtypeStruct((T, MLA_WIDTH), CDT),
                   jax.ShapeDtypeStruct((T, POOL_WIDTH), F32),
                   jax.ShapeDtypeStruct((T, 2 * D_MODEL), CDT)],
        compiler_params=_cparams(("parallel",)),
        name="in_proj",
    )(x2d, *consts, *ropes, *tail)


def _attn_kernel(q_ref, k_ref, v_ref, o_ref):
    outs = []
    for j in range(MLA_HEADS):
        q = q_ref[:, j * HEAD_PAD:(j + 1) * HEAD_PAD]
        k = k_ref[:, j * HEAD_PAD:(j + 1) * HEAD_PAD]
        v = v_ref[:, j * V_HEAD:(j + 1) * V_HEAD]
        s = _dot_nt(q, k)
        m = jnp.max(s, axis=-1, keepdims=True)
        p = jnp.exp(s - m)
        l = jnp.sum(p, axis=-1, keepdims=True)
        outs.append(_dot(p.astype(CDT), v) / l)
    o_ref[...] = jnp.concatenate(outs, axis=-1).astype(o_ref.dtype)


def _attention(q, k, v, B, S, tq):
    tq = min(tq, S)
    nq = S // tq
    return pl.pallas_call(
        _attn_kernel,
        grid=(B, nq),
        in_specs=[pl.BlockSpec((tq, MLA_HEADS * HEAD_PAD), lambda b, i: (b * nq + i, 0)),
                  pl.BlockSpec((S, MLA_HEADS * HEAD_PAD), lambda b, i: (b, 0)),
                  pl.BlockSpec((S, MLA_WIDTH), lambda b, i: (b, 0))],
        out_specs=pl.BlockSpec((tq, MLA_WIDTH), lambda b, i: (b * nq + i, 0)),
        out_shape=jax.ShapeDtypeStruct((B * S, MLA_WIDTH), CDT),
        compiler_params=_cparams(("parallel", "parallel")),
        name="attention",
    )(q, k, v)


def _pool_kernel(p_ref, pw_ref, ps_ref, y_ref):
    S = p_ref.shape[0]
    t = lax.broadcasted_iota(jnp.int32, (S, POOL_GC), 0)
    for g, w in enumerate(POOL_WINDOWS):
        half = w // 2
        sl = slice(g * POOL_GC, (g + 1) * POOL_GC)
        pg = p_ref[:, sl]
        acc = pg
        for d in range(-half, half):
            if d == 0:
                continue
            shifted = pltpu.roll(pg, (-d) % S, 0)
            valid = (t + d >= 0) & (t + d < S)
            acc = acc + jnp.where(valid, shifted, 0.0)
        count = (jnp.minimum(t + half, S) - jnp.maximum(t - half, 0)).astype(F32)
        mixed = acc / count - pg
        yg = _dot(mixed.astype(CDT), pw_ref[g]) * ps_ref[:, sl]
        y_ref[:, sl] = yg.astype(y_ref.dtype)


def _pool(p, lw, B, S):
    return pl.pallas_call(
        _pool_kernel,
        grid=(B,),
        in_specs=[pl.BlockSpec((S, POOL_WIDTH), lambda b: (b, 0)),
                  pl.BlockSpec(lw["pool_w"].shape, lambda b: (0, 0, 0)),
                  pl.BlockSpec((1, POOL_WIDTH), lambda b: (0, 0))],
        out_specs=pl.BlockSpec((S, POOL_WIDTH), lambda b: (b, 0)),
        out_shape=jax.ShapeDtypeStruct((B * S, POOL_WIDTH), CDT),
        compiler_params=_cparams(("parallel",)),
        name="pool",
    )(p, lw["pool_w"], lw["pool_scale"])


def _out_proj_kernel(x_ref, o_ref, y_ref, gate_ref, wo_ref, wpo_ref, wout_ref, g2_ref, wq_ref, keys_ref,
                     xnew_out, xnt_out, st_out):
    a = _dot(o_ref[...], wo_ref[...])
    bp = _dot(y_ref[...], wpo_ref[...])
    ga = gate_ref[:, :D_MODEL].astype(F32)
    gb = gate_ref[:, D_MODEL:].astype(F32)
    mix = ga * a + gb * bp
    xnew = x_ref[...] + _dot(mix.astype(CDT), wout_ref[...])
    xnew_out[...] = xnew
    xn = _rms(xnew, g2_ref[...])
    xnt_out[...] = xn.T.astype(xnt_out.dtype)
    qp = _dot(xn.astype(CDT), wq_ref[...])
    for hp in range(2 * PEER_HEADS):
        qh = qp[:, hp * PEER_HALF:(hp + 1) * PEER_HALF].astype(CDT)
        st_out[hp] = _dot_nt(keys_ref[hp], qh)


def _out_proj(x2d, o, y, gates, lw, tm):
    T = x2d.shape[0]
    tm = min(tm, T)
    row = lambda w: pl.BlockSpec((tm, w), lambda i: (i, 0))
    full = lambda a: pl.BlockSpec(a.shape, lambda i: (0,) * a.ndim)
    consts = [lw["w_o_mla"], lw["w_pool_out"], lw["w_out"], lw["g2"], lw["peer_wq"], lw["peer_keys"]]
    return pl.pallas_call(
        _out_proj_kernel,
        grid=(T // tm,),
        in_specs=[row(D_MODEL), row(MLA_WIDTH), row(POOL_WIDTH), row(2 * D_MODEL)] + [full(a) for a in consts],
        out_specs=[row(D_MODEL),
                   pl.BlockSpec((D_MODEL, tm), lambda i: (0, i)),
                   pl.BlockSpec((2 * PEER_HEADS, N_KEYS, tm), lambda i: (0, 0, i))],
        out_shape=[jax.ShapeDtypeStruct((T, D_MODEL), F32),
                   jax.ShapeDtypeStruct((D_MODEL, T), CDT),
                   jax.ShapeDtypeStruct((2 * PEER_HEADS, N_KEYS, T), F32)],
        compiler_params=_cparams(("parallel",)),
        name="out_proj",
    )(x2d, o, y, gates, *consts)


def _top16(s, key_iota, row16, exact):
    work = s
    rank = jnp.full(s.shape, float(PEER_TOPK), F32)
    tops = jnp.zeros((PEER_TOPK, s.shape[1]), F32)
    for r in range(PEER_TOPK):
        m = jnp.max(work, axis=0, keepdims=True)
        sel = work == m
        if exact:
            first = jnp.min(jnp.where(sel, key_iota, float(N_KEYS)), axis=0, keepdims=True)
            sel = key_iota == first
            rank = jnp.where(sel, float(r), rank)
            work = jnp.where(sel, NEG_INF, work)
        else:
            work = jnp.where(sel, -_FLT_MAX + r * _FLT_TOP_ULP, work)
        tops = jnp.where(row16 == r, m, tops)
    if not exact:
        rank = jnp.minimum((work + _FLT_MAX) * (1.0 / _FLT_TOP_ULP), float(PEER_TOPK))
    return rank, tops


def _cand_grid(ta, tb, combine):
    pieces = [combine(ta[0:1], tb)]
    for r1 in range(1, 8):
        pieces.append(combine(ta[r1:r1 + 1], tb[0:8]))
    pieces.append(combine(ta[8:16], tb[0:1]))
    return jnp.concatenate(pieces, axis=0)


def _count_true(mask):
    return jnp.sum(jnp.where(mask, 1.0, 0.0), axis=0, keepdims=True)


def _route(s1, s2, key_iota, row16, fidx, exact):
    rank1, ta = _top16(s1, key_iota, row16, exact)
    rank2, tb = _top16(s2, key_iota, row16, exact)
    a0 = ta[0:1]
    b0 = tb[0:1]
    cand0 = jnp.where(fidx < _FIDX_INVALID, _cand_grid(ta, tb, lambda a, b: a + b), NEG_INF)
    ecand = _cand_grid(jnp.exp(ta - a0), jnp.exp(tb - b0), lambda a, b: a * b)
    cand = cand0
    if exact:
        selected = jnp.zeros(cand.shape, F32)
        for _ in range(PEER_TOPK):
            m = jnp.max(cand, axis=0, keepdims=True)
            first = jnp.min(jnp.where(cand == m, fidx, float(_FIDX_INVALID)), axis=0, keepdims=True)
            sel = fidx == first
            selected = jnp.where(sel, 1.0, selected)
            cand = jnp.where(sel, NEG_INF, cand)
        bad = jnp.zeros_like(a0)
    else:
        for _ in range(PEER_TOPK):
            m = jnp.max(cand, axis=0, keepdims=True)
            cand = jnp.where(cand == m, NEG_INF, cand)
        selected = jnp.where(cand0 >= m, 1.0, 0.0)
        k = float(PEER_TOPK)
        bad = jnp.where((_count_true(rank1 < k) != k) | (_count_true(rank2 < k) != k)
                        | (jnp.sum(selected, axis=0, keepdims=True) != k), 1.0, 0.0)
    z = jnp.sum(selected * ecand, axis=0, keepdims=True)
    cntr = [jnp.sum(selected[0:16], axis=0, keepdims=True)]
    for r1 in range(1, 8):
        cntr.append(jnp.sum(selected[8 + 8 * r1:16 + 8 * r1], axis=0, keepdims=True))
    for r1 in range(8, 16):
        cntr.append(selected[64 + r1:65 + r1])
    cnt = jnp.zeros(s1.shape, F32)
    for r1 in range(PEER_TOPK):
        cnt = jnp.where(rank1 == float(r1), cntr[r1], cnt)
    return rank2, jnp.exp(s2 - b0), cnt, jnp.exp(s1 - a0) / z, bad


def _peer_topk_kernel(s_ref, fidx_ref, r2_out, e2_out, cnt_out, c_out, *, n_chunks):
    key_iota = lax.broadcasted_iota(jnp.int32, (N_KEYS, ROUTE_CHUNK), 0).astype(F32)
    row16 = lax.broadcasted_iota(jnp.int32, (PEER_TOPK, ROUTE_CHUNK), 0)
    fidx = fidx_ref[...]

    def body(it, carry):
        h = it // n_chunks
        cols = pl.ds(pl.multiple_of((it % n_chunks) * ROUTE_CHUNK, ROUTE_CHUNK), ROUTE_CHUNK)
        s1 = s_ref[2 * h, :, cols]
        s2 = s_ref[2 * h + 1, :, cols]

        def store(r2, e2, cnt, c):
            r2_out[h, :, cols] = r2.astype(r2_out.dtype)
            e2_out[h, :, cols] = e2.astype(e2_out.dtype)
            cnt_out[h, :, cols] = cnt.astype(cnt_out.dtype)
            c_out[h, :, cols] = c.astype(c_out.dtype)

        *fast, bad = _route(s1, s2, key_iota, row16, fidx, exact=False)
        store(*fast)

        @pl.when(jnp.max(bad) > 0.0)
        def _():
            *slow, _ = _route(s1, s2, key_iota, row16, fidx, exact=True)
            store(*slow)

        return carry

    lax.fori_loop(0, PEER_HEADS * n_chunks, body, 0)


def _cand_fidx():
    rows = []
    rows += [0 * 16 + r2 for r2 in range(16)]
    for r1 in range(1, 8):
        n = PEER_TOPK // (r1 + 1)
        rows += [r1 * 16 + r2 if r2 < n else _FIDX_INVALID + r1 * 16 + r2 for r2 in range(8)]
    rows += [r1 * 16 for r1 in range(8, 16)]
    assert len(rows) == _CAND_ROWS
    return jnp.broadcast_to(jnp.asarray(rows, F32)[:, None], (_CAND_ROWS, ROUTE_CHUNK))


def _peer_topk(st, tmk):
    T = st.shape[-1]
    tmk = min(tmk, T)
    spec = pl.BlockSpec((PEER_HEADS, N_KEYS, tmk), lambda i: (0, 0, i))
    shp = jax.ShapeDtypeStruct((PEER_HEADS, N_KEYS, T), GDT)
    shp32 = jax.ShapeDtypeStruct((PEER_HEADS, N_KEYS, T), F32)
    return pl.pallas_call(
        functools.partial(_peer_topk_kernel, n_chunks=tmk // ROUTE_CHUNK),
        grid=(T // tmk,),
        in_specs=[pl.BlockSpec((2 * PEER_HEADS, N_KEYS, tmk), lambda i: (0, 0, i)),
                  pl.BlockSpec((_CAND_ROWS, ROUTE_CHUNK), lambda i: (0, 0))],
        out_specs=[spec] * 4,
        out_shape=[shp, shp, shp32, shp32],
        compiler_params=_cparams(("parallel",)),
        name="peer_topk",
    )(st, _cand_fidx())


def _peer_dense_kernel(x_ref, xnt_ref, r2_ref, e2_ref, cnt_ref, c_ref, u_ref, vt_ref, out_ref,
                       acc_ref, ht_ref, at_ref, bc_ref, *, eb, tc, nsub):
    e = pl.program_id(1)
    tm = xnt_ref.shape[1]
    jpb = eb // N_KEYS
    sub = eb // nsub
    pk = bc_ref.shape[-2]

    @pl.when(e == 0)
    def _():
        acc_ref[...] = jnp.zeros_like(acc_ref)

    def fill(h, carry):
        for jb in range(jpb):
            j = e * jpb + jb
            bc_ref[0, h, jb] = jnp.broadcast_to(cnt_ref[h, pl.ds(j, 1), :], (pk, tm)).astype(GDT)
            bc_ref[1, h, jb] = jnp.broadcast_to(c_ref[h, pl.ds(j, 1), :], (pk, tm)).astype(GDT)
        return carry

    lax.fori_loop(0, PEER_HEADS, fill, 0)

    for sb in range(nsub):
        rows = slice(sb * sub, (sb + 1) * sub)
        ht_ref[rows, :] = _dot(u_ref[rows, :], xnt_ref[...])
    for sb in range(nsub):
        rows = slice(sb * sub, (sb + 1) * sub)
        for jb in range(sb * sub // N_KEYS, (sb + 1) * sub // N_KEYS):
            krows = slice(jb * N_KEYS, (jb + 1) * N_KEYS)
            for ci in range(tm // tc):
                cols = slice(ci * tc, (ci + 1) * tc)
                g = jnp.zeros((N_KEYS // pk, pk, tc), GDT)
                for h in range(PEER_HEADS):
                    r2 = r2_ref[h, :, cols].reshape(N_KEYS // pk, pk, tc)
                    e2 = e2_ref[h, :, cols].reshape(N_KEYS // pk, pk, tc)
                    g = g + jnp.where(r2 < bc_ref[0, h, jb, :, cols], e2 * bc_ref[1, h, jb, :, cols], 0)
                a = g.reshape(N_KEYS, tc) * jax.nn.gelu(ht_ref[krows, cols]).astype(GDT)
                at_ref[krows, cols] = a.astype(at_ref.dtype)
        acc_ref[...] += _dot(vt_ref[:, rows], at_ref[rows, :])

    @pl.when(e == pl.num_programs(1) - 1)
    def _():
        out_ref[...] = x_ref[...] + acc_ref[...].T


def _peer_dense(x2d, xnt, r2, e2, cnt, c, u_all, vt_all, layer, tm, eb, tc):
    T = x2d.shape[0]
    tm = min(tm, T)
    tc = min(tc, tm)
    tok3 = pl.BlockSpec((PEER_HEADS, N_KEYS, tm), lambda i, e: (0, 0, i))
    return pl.pallas_call(
        functools.partial(_peer_dense_kernel, eb=eb, tc=tc, nsub=eb // 256),
        grid=(T // tm, N_EXPERTS // eb),
        in_specs=[pl.BlockSpec((tm, D_MODEL), lambda i, e: (i, 0)),
                  pl.BlockSpec((D_MODEL, tm), lambda i, e: (0, i)),
                  tok3, tok3, tok3, tok3,
                  pl.BlockSpec((None, eb, D_MODEL), lambda i, e: (layer, e, 0)),
                  pl.BlockSpec((None, D_MODEL, eb), lambda i, e: (layer, 0, e))],
        out_specs=pl.BlockSpec((tm, D_MODEL), lambda i, e: (i, 0)),
        out_shape=jax.ShapeDtypeStruct((T, D_MODEL), F32),
        scratch_shapes=[pltpu.VMEM((D_MODEL, tm), F32),
                        pltpu.VMEM((eb, tm), F32),
                        pltpu.VMEM((eb, tm), CDT),
                        pltpu.VMEM((2, PEER_HEADS, eb // N_KEYS, GATE_TILE_ROWS, tm), GDT)],
        compiler_params=_cparams(("parallel", "arbitrary"), vmem=PEER_DENSE_VMEM_LIMIT),
        name="peer_dense",
    )(x2d, xnt, r2, e2, cnt, c, u_all, vt_all)


def _pad_heads(w, width):
    r = w.shape[0]
    w = w.reshape(r, MLA_HEADS, width)
    w = jnp.pad(w, ((0, 0), (0, 0), (0, HEAD_PAD - width)))
    return w.reshape(r, MLA_HEADS * HEAD_PAD)


def _layer_weights(l, norm1_g, w_in, q_lora_g, kv_lora_g, w_uq, w_ukv, q_head_g, k_head_g, w_o_mla,
                   pool_w, pool_scale, w_pool_out, gate_bias, w_out, norm2_g, peer_wq, peer_keys):
    wi = w_in[l]
    c0 = Q_LORA
    c1 = c0 + KV_LORA
    c2 = c1 + QK_ROPE
    c3 = c2 + POOL_WIDTH
    zeros = lambda n: jnp.zeros((D_MODEL, n), wi.dtype)
    w_in_pad = jnp.concatenate([wi[:, :c1], zeros(QK_NOPE), wi[:, c1:c2], zeros(LANES - QK_HEAD),
                                wi[:, c2:]], axis=1)
    wkv = w_ukv[l].reshape(KV_LORA, MLA_HEADS, QK_NOPE + V_HEAD)
    w_uk = _pad_heads(wkv[:, :, :QK_NOPE].reshape(KV_LORA, MLA_HEADS * QK_NOPE), QK_NOPE)
    w_uv = wkv[:, :, QK_NOPE:].reshape(KV_LORA, MLA_WIDTH)
    pad_g = lambda g: jnp.pad(g, (0, LANES - QK_HEAD)).reshape(1, LANES)
    return {
        "g1": norm1_g[l].reshape(1, D_MODEL),
        "w_in": w_in_pad.astype(CDT),
        "qlg": q_lora_g[l].reshape(1, Q_LORA),
        "kvlg": kv_lora_g[l].reshape(1, KV_LORA),
        "w_uq": _pad_heads(w_uq[l], QK_HEAD).astype(CDT),
        "w_uk": w_uk.astype(CDT),
        "w_uv": w_uv.astype(CDT),
        "qg": pad_g(q_head_g[l] * (1.0 / math.sqrt(QK_HEAD))),
        "kg": pad_g(k_head_g[l]),
        "gate_bias": gate_bias[l].reshape(1, 2 * D_MODEL),
        "w_o_mla": w_o_mla[l].astype(CDT),
        "pool_w": pool_w[l].astype(CDT),
        "pool_scale": pool_scale[l].reshape(1, POOL_WIDTH),
        "w_pool_out": w_pool_out[l].astype(CDT),
        "w_out": w_out[l].astype(CDT),
        "g2": norm2_g[l].reshape(1, D_MODEL),
        "peer_wq": peer_wq[l].astype(CDT),
        "peer_keys": peer_keys[l].reshape(2 * PEER_HEADS, N_KEYS, PEER_HALF).astype(CDT),
    }


def kernel(x, positions, norm1_g, w_in, q_lora_g, kv_lora_g, w_uq, w_ukv, q_head_g, k_head_g, w_o_mla,
           pool_w, pool_scale, w_pool_out, gate_bias, w_out, norm2_g, peer_wq, peer_keys, peer_u, peer_v):
    B, S, D = x.shape
    assert D == D_MODEL and S % LANES == 0
    T = B * S
    depth = norm1_g.shape[0]
    ropes = _rope_tables(positions)
    x2d = x.reshape(T, D)
    u_all = peer_u.astype(CDT)
    vt_all = jnp.swapaxes(peer_v, 1, 2).astype(CDT)
    for l in range(depth):
        lw = _layer_weights(l, norm1_g, w_in, q_lora_g, kv_lora_g, w_uq, w_ukv, q_head_g, k_head_g,
                            w_o_mla, pool_w, pool_scale, w_pool_out, gate_bias, w_out, norm2_g,
                            peer_wq, peer_keys)
        q, k, v, p, gates = _in_proj(x2d, lw, ropes, tm=256)
        o = _attention(q, k, v, B, S, tq=256)
        y = _pool(p, lw, B, S)
        x2d, xnt, st = _out_proj(x2d, o, y, gates, lw, tm=256)
        r2, e2, cnt, c = _peer_topk(st, tmk=512)
        x2d = _peer_dense(x2d, xnt, r2, e2, cnt, c, u_all, vt_all, l, tm=512, eb=2048, tc=256)
    return x2d.reshape(B, S, D)
```

```python
import functools
import math

import jax
import jax.numpy as jnp
import numpy as np
from jax import lax
from jax.experimental import pallas as pl
from jax.experimental.pallas import tpu as pltpu

D_MODEL = 1024
MLA_HEADS = 8
Q_LORA = 384
KV_LORA = 256
QK_NOPE = 64
QK_ROPE = 32
QK_HEAD = QK_NOPE + QK_ROPE
V_HEAD = 64
MLA_WIDTH = MLA_HEADS * V_HEAD
ROPE_BASE = 10000.0
POOL_WINDOWS = (2, 4, 8, 16)
POOL_WIDTH = 512
POOL_GC = 128
PEER_HEADS = 8
N_KEYS = 128
N_EXPERTS = N_KEYS * N_KEYS
PEER_HALF = 128
PEER_TOPK = 16
RMS_EPS = 1e-6

LANES = 128
GATE_TILE_ROWS = 16
ROUTE_CHUNK = 256
HEAD_PAD = LANES
IN_PAD = Q_LORA + KV_LORA + LANES + POOL_WIDTH + 2 * D_MODEL
VMEM_LIMIT = 48 * 1024 * 1024
PEER_DENSE_VMEM_LIMIT = 56 * 1024 * 1024

CDT = jnp.bfloat16
GDT = jnp.bfloat16
F32 = jnp.float32
NEG_INF = float("-inf")
_FLT_MAX = float(np.finfo(np.float32).max)
_FLT_TOP_ULP = 2.0 ** 104

_CAND_ROWS = 16 + 8 * 7 + 8
_FIDX_INVALID = 1 << 20


def _cparams(sem, vmem=VMEM_LIMIT, flags=None):
    return pltpu.CompilerParams(dimension_semantics=sem, vmem_limit_bytes=vmem, flags=flags)


def _rms(x, g):
    return x * lax.rsqrt(jnp.mean(x * x, axis=-1, keepdims=True) + RMS_EPS) * g


def _dot(a, b):
    return jnp.dot(a, b, preferred_element_type=F32)


def _dot_nt(a, b):
    return lax.dot_general(a, b, (((1,), (1,)), ((), ())), preferred_element_type=F32)


def _rope_kernel(pos_ref, inv_ref, c_ref, s1_ref, s2_ref):
    ang = pos_ref[...] * inv_ref[...]
    lane = lax.broadcasted_iota(jnp.int32, ang.shape, 1)
    cos = jnp.cos(ang)
    sin = jnp.sin(ang)
    half = QK_ROPE // 2
    c_ref[...] = jnp.where(lane < QK_NOPE, 1.0, jnp.where(lane < QK_HEAD, cos, 0.0))
    s1_ref[...] = jnp.where((lane >= QK_NOPE) & (lane < QK_NOPE + half), -sin, 0.0)
    s2_ref[...] = jnp.where((lane >= QK_NOPE + half) & (lane < QK_HEAD), sin, 0.0)


def _rope_tables(positions):
    T = positions.size
    tm = min(T, 2048)
    pos = jnp.broadcast_to(positions.reshape(T, 1).astype(F32), (T, LANES))
    inv = ROPE_BASE ** (-jnp.arange(0, QK_ROPE, 2, dtype=F32) / QK_ROPE)
    inv_pat = jnp.concatenate([jnp.zeros((QK_NOPE,), F32), inv, inv,
                               jnp.zeros((LANES - QK_HEAD,), F32)]).reshape(1, LANES)
    spec = pl.BlockSpec((tm, LANES), lambda i: (i, 0))
    return pl.pallas_call(
        _rope_kernel,
        grid=(T // tm,),
        in_specs=[spec, pl.BlockSpec((1, LANES), lambda i: (0, 0))],
        out_specs=[spec, spec, spec],
        out_shape=[jax.ShapeDtypeStruct((T, LANES), F32)] * 3,
        compiler_params=_cparams(("parallel",)),
        name="rope_tables",
    )(pos, inv_pat)


def _in_proj_kernel(x_ref, g1_ref, win_ref, qlg_ref, kvlg_ref, wuq_ref, wuk_ref, wuv_ref,
                    c_ref, s1_ref, s2_ref, qg_ref, kg_ref, bias_ref,
                    q_out, k_out, v_out, p_out, gate_out):
    xn = _rms(x_ref[...], g1_ref[...])
    y = _dot(xn.astype(CDT), win_ref[...])
    o = 0
    cq = y[:, o:o + Q_LORA]; o += Q_LORA
    ckv = y[:, o:o + KV_LORA]; o += KV_LORA
    kpe = y[:, o:o + LANES]; o += LANES
    p_out[...] = y[:, o:o + POOL_WIDTH]; o += POOL_WIDTH
    gate_out[...] = jax.nn.sigmoid(y[:, o:] + bias_ref[...]).astype(gate_out.dtype)

    cqn = _rms(cq, qlg_ref[...]).astype(CDT)
    ckvn = _rms(ckv, kvlg_ref[...]).astype(CDT)
    q_raw = _dot(cqn, wuq_ref[...])
    k_raw = _dot(ckvn, wuk_ref[...])
    v_out[...] = _dot(ckvn, wuv_ref[...]).astype(v_out.dtype)

    cpat = c_ref[...]
    s1pat = s1_ref[...]
    s2pat = s2_ref[...]
    half = QK_ROPE // 2

    def rope(xh):
        return (xh * cpat + pltpu.roll(xh, LANES - half, 1) * s1pat
                + pltpu.roll(xh, half, 1) * s2pat)

    def head_norm(xh, g):
        ms = jnp.sum(xh * xh, axis=-1, keepdims=True) * (1.0 / QK_HEAD)
        return xh * lax.rsqrt(ms + RMS_EPS) * g

    kpe_r = rope(kpe)
    qg = qg_ref[...]
    kg = kg_ref[...]
    for h in range(MLA_HEADS):
        sl = slice(h * HEAD_PAD, (h + 1) * HEAD_PAD)
        q_out[:, sl] = head_norm(rope(q_raw[:, sl]), qg).astype(q_out.dtype)
        k_out[:, sl] = head_norm(k_raw[:, sl] + kpe_r, kg).astype(k_out.dtype)


def _in_proj(x2d, lw, ropes, tm):
    T = x2d.shape[0]
    tm = min(tm, T)
    row = lambda w: pl.BlockSpec((tm, w), lambda i: (i, 0))
    full = lambda a: pl.BlockSpec(a.shape, lambda i: (0,) * a.ndim)
    consts = [lw["g1"], lw["w_in"], lw["qlg"], lw["kvlg"], lw["w_uq"], lw["w_uk"], lw["w_uv"]]
    tail = [lw["qg"], lw["kg"], lw["gate_bias"]]
    return pl.pallas_call(
        _in_proj_kernel,
        grid=(T // tm,),
        in_specs=[row(D_MODEL)] + [full(a) for a in consts] + [row(LANES)] * 3 + [full(a) for a in tail],
        out_specs=[row(MLA_HEADS * HEAD_PAD), row(MLA_HEADS * HEAD_PAD), row(MLA_WIDTH),
                   row(POOL_WIDTH), row(2 * D_MODEL)],
        out_shape=[jax.ShapeDtypeStruct((T, MLA_HEADS * HEAD_PAD), CDT),
                   jax.ShapeDtypeStruct((T, MLA_HEADS * HEAD_PAD), CDT),
                   jax.ShapeDtypeStruct((T, MLA_WIDTH), CDT),
                   jax.ShapeDtypeStruct((T, POOL_WIDTH), F32),
                   jax.ShapeDtypeStruct((T, 2 * D_MODEL), CDT)],
        compiler_params=_cparams(("parallel",)),
        name="in_proj",
    )(x2d, *consts, *ropes, *tail)


def _attn_kernel(q_ref, k_ref, v_ref, o_ref):
    outs = []
    for j in range(MLA_HEADS):
        q = q_ref[:, j * HEAD_PAD:(j + 1) * HEAD_PAD]
        k = k_ref[:, j * HEAD_PAD:(j + 1) * HEAD_PAD]
        v = v_ref[:, j * V_HEAD:(j + 1) * V_HEAD]
        s = _dot_nt(q, k)
        m = jnp.max(s, axis=-1, keepdims=True)
        p = jnp.exp(s - m)
        l = jnp.sum(p, axis=-1, keepdims=True)
        outs.append(_dot(p.astype(CDT), v) / l)
    o_ref[...] = jnp.concatenate(outs, axis=-1).astype(o_ref.dtype)


def _attention(q, k, v, B, S, tq):
    tq = min(tq, S)
    nq = S // tq
    return pl.pallas_call(
        _attn_kernel,
        grid=(B, nq),
        in_specs=[pl.BlockSpec((tq, MLA_HEADS * HEAD_PAD), lambda b, i: (b * nq + i, 0)),
                  pl.BlockSpec((S, MLA_HEADS * HEAD_PAD), lambda b, i: (b, 0)),
                  pl.BlockSpec((S, MLA_WIDTH), lambda b, i: (b, 0))],
        out_specs=pl.BlockSpec((tq, MLA_WIDTH), lambda b, i: (b * nq + i, 0)),
        out_shape=jax.ShapeDtypeStruct((B * S, MLA_WIDTH), CDT),
        compiler_params=_cparams(("parallel", "parallel")),
        name="attention",
    )(q, k, v)


def _pool_kernel(p_ref, pw_ref, ps_ref, y_ref):
    S = p_ref.shape[0]
    t = lax.broadcasted_iota(jnp.int32, (S, POOL_GC), 0)
    for g, w in enumerate(POOL_WINDOWS):
        half = w // 2
        sl = slice(g * POOL_GC, (g + 1) * POOL_GC)
        pg = p_ref[:, sl]
        acc = pg
        for d in range(-half, half):
            if d == 0:
                continue
            shifted = pltpu.roll(pg, (-d) % S, 0)
            valid = (t + d >= 0) & (t + d < S)
            acc = acc + jnp.where(valid, shifted, 0.0)
        count = (jnp.minimum(t + half, S) - jnp.maximum(t - half, 0)).astype(F32)
        mixed = acc / count - pg
        yg = _dot(mixed.astype(CDT), pw_ref[g]) * ps_ref[:, sl]
        y_ref[:, sl] = yg.astype(y_ref.dtype)


def _pool(p, lw, B, S):
    return pl.pallas_call(
        _pool_kernel,
        grid=(B,),
        in_specs=[pl.BlockSpec((S, POOL_WIDTH), lambda b: (b, 0)),
                  pl.BlockSpec(lw["pool_w"].shape, lambda b: (0, 0, 0)),
                  pl.BlockSpec((1, POOL_WIDTH), lambda b: (0, 0))],
        out_specs=pl.BlockSpec((S, POOL_WIDTH), lambda b: (b, 0)),
        out_shape=jax.ShapeDtypeStruct((B * S, POOL_WIDTH), CDT),
        compiler_params=_cparams(("parallel",)),
        name="pool",
    )(p, lw["pool_w"], lw["pool_scale"])


def _out_proj_kernel(x_ref, o_ref, y_ref, gate_ref, wo_ref, wpo_ref, wout_ref, g2_ref, wq_ref, keys_ref,
                     xnew_out, xnt_out, st_out):
    a = _dot(o_ref[...], wo_ref[...])
    bp = _dot(y_ref[...], wpo_ref[...])
    ga = gate_ref[:, :D_MODEL].astype(F32)
    gb = gate_ref[:, D_MODEL:].astype(F32)
    mix = ga * a + gb * bp
    xnew = x_ref[...] + _dot(mix.astype(CDT), wout_ref[...])
    xnew_out[...] = xnew
    xn = _rms(xnew, g2_ref[...])
    xnt_out[...] = xn.T.astype(xnt_out.dtype)
    qp = _dot(xn.astype(CDT), wq_ref[...])
    for hp in range(2 * PEER_HEADS):
        qh = qp[:, hp * PEER_HALF:(hp + 1) * PEER_HALF].astype(CDT)
        st_out[hp] = _dot_nt(keys_ref[hp], qh)


def _out_proj(x2d, o, y, gates, lw, tm):
    T = x2d.shape[0]
    tm = min(tm, T)
    row = lambda w: pl.BlockSpec((tm, w), lambda i: (i, 0))
    full = lambda a: pl.BlockSpec(a.shape, lambda i: (0,) * a.ndim)
    consts = [lw["w_o_mla"], lw["w_pool_out"], lw["w_out"], lw["g2"], lw["peer_wq"], lw["peer_keys"]]
    return pl.pallas_call(
        _out_proj_kernel,
        grid=(T // tm,),
        in_specs=[row(D_MODEL), row(MLA_WIDTH), row(POOL_WIDTH), row(2 * D_MODEL)] + [full(a) for a in consts],
        out_specs=[row(D_MODEL),
                   pl.BlockSpec((D_MODEL, tm), lambda i: (0, i)),
                   pl.BlockSpec((2 * PEER_HEADS, N_KEYS, tm), lambda i: (0, 0, i))],
        out_shape=[jax.ShapeDtypeStruct((T, D_MODEL), F32),
                   jax.ShapeDtypeStruct((D_MODEL, T), CDT),
                   jax.ShapeDtypeStruct((2 * PEER_HEADS, N_KEYS, T), F32)],
        compiler_params=_cparams(("parallel",)),
        name="out_proj",
    )(x2d, o, y, gates, *consts)


def _top16(s, key_iota, row16, exact):
    work = s
    rank = jnp.full(s.shape, float(PEER_TOPK), F32)
    tops = jnp.zeros((PEER_TOPK, s.shape[1]), F32)
    for r in range(PEER_TOPK):
        m = jnp.max(work, axis=0, keepdims=True)
        sel = work == m
        if exact:
            first = jnp.min(jnp.where(sel, key_iota, float(N_KEYS)), axis=0, keepdims=True)
            sel = key_iota == first
            rank = jnp.where(sel, float(r), rank)
            work = jnp.where(sel, NEG_INF, work)
        else:
            work = jnp.where(sel, -_FLT_MAX + r * _FLT_TOP_ULP, work)
        tops = jnp.where(row16 == r, m, tops)
    if not exact:
        rank = jnp.minimum((work + _FLT_MAX) * (1.0 / _FLT_TOP_ULP), float(PEER_TOPK))
    return rank, tops


def _cand_grid(ta, tb, combine):
    pieces = [combine(ta[0:1], tb)]
    for r1 in range(1, 8):
        pieces.append(combine(ta[r1:r1 + 1], tb[0:8]))
    pieces.append(combine(ta[8:16], tb[0:1]))
    return jnp.concatenate(pieces, axis=0)


def _count_true(mask):
    return jnp.sum(jnp.where(mask, 1.0, 0.0), axis=0, keepdims=True)


def _route(s1, s2, key_iota, row16, fidx, exact):
    rank1, ta = _top16(s1, key_iota, row16, exact)
    rank2, tb = _top16(s2, key_iota, row16, exact)
    a0 = ta[0:1]
    b0 = tb[0:1]
    cand0 = jnp.where(fidx < _FIDX_INVALID, _cand_grid(ta, tb, lambda a, b: a + b), NEG_INF)
    ecand = _cand_grid(jnp.exp(ta - a0), jnp.exp(tb - b0), lambda a, b: a * b)
    cand = cand0
    if exact:
        selected = jnp.zeros(cand.shape, F32)
        for _ in range(PEER_TOPK):
            m = jnp.max(cand, axis=0, keepdims=True)
            first = jnp.min(jnp.where(cand == m, fidx, float(_FIDX_INVALID)), axis=0, keepdims=True)
            sel = fidx == first
            selected = jnp.where(sel, 1.0, selected)
            cand = jnp.where(sel, NEG_INF, cand)
        bad = jnp.zeros_like(a0)
    else:
        for _ in range(PEER_TOPK):
            m = jnp.max(cand, axis=0, keepdims=True)
            cand = jnp.where(cand == m, NEG_INF, cand)
        selected = jnp.where(cand0 >= m, 1.0, 0.0)
        k = float(PEER_TOPK)
        bad = jnp.where((_count_true(rank1 < k) != k) | (_count_true(rank2 < k) != k)
                        | (jnp.sum(selected, axis=0, keepdims=True) != k), 1.0, 0.0)
    z = jnp.sum(selected * ecand, axis=0, keepdims=True)
    cntr = [jnp.sum(selected[0:16], axis=0, keepdims=True)]
    for r1 in range(1, 8):
        cntr.append(jnp.sum(selected[8 + 8 * r1:16 + 8 * r1], axis=0, keepdims=True))
    for r1 in range(8, 16):
        cntr.append(selected[64 + r1:65 + r1])
    cntr_rows = jnp.concatenate(cntr[0:8] + [selected[72:80]], axis=0)
    cnt = jnp.zeros(s1.shape, F32)
    for v in range(1, 5):
        ranks_with_v = jnp.sum(jnp.where(cntr_rows >= float(v), 1.0, 0.0), axis=0, keepdims=True)
        cnt = jnp.where(rank1 < ranks_with_v, float(v), cnt)
    for r1 in range(3):
        cnt = jnp.where(rank1 == float(r1), cntr[r1], cnt)
    return rank2, jnp.exp(s2 - b0), cnt, jnp.exp(s1 - a0) / z, bad


def _peer_topk_kernel(s_ref, fidx_ref, r2_out, e2_out, cnt_out, c_out, *, n_chunks):
    key_iota = lax.broadcasted_iota(jnp.int32, (N_KEYS, ROUTE_CHUNK), 0).astype(F32)
    row16 = lax.broadcasted_iota(jnp.int32, (PEER_TOPK, ROUTE_CHUNK), 0)
    fidx = fidx_ref[...]

    def body(it, carry):
        h = it // n_chunks
        cols = pl.ds(pl.multiple_of((it % n_chunks) * ROUTE_CHUNK, ROUTE_CHUNK), ROUTE_CHUNK)
        s1 = s_ref[2 * h, :, cols]
        s2 = s_ref[2 * h + 1, :, cols]

        def store(r2, e2, cnt, c):
            r2_out[h, :, cols] = r2.astype(r2_out.dtype)
            e2_out[h, :, cols] = e2.astype(e2_out.dtype)
            cnt_out[h, :, cols] = cnt.astype(cnt_out.dtype)
            c_out[h, :, cols] = c.astype(c_out.dtype)

        *fast, bad = _route(s1, s2, key_iota, row16, fidx, exact=False)
        store(*fast)

        @pl.when(jnp.max(bad) > 0.0)
        def _():
            *slow, _ = _route(s1, s2, key_iota, row16, fidx, exact=True)
            store(*slow)

        return carry

    lax.fori_loop(0, PEER_HEADS * n_chunks, body, 0)


def _cand_fidx():
    rows = []
    rows += [0 * 16 + r2 for r2 in range(16)]
    for r1 in range(1, 8):
        n = PEER_TOPK // (r1 + 1)
        rows += [r1 * 16 + r2 if r2 < n else _FIDX_INVALID + r1 * 16 + r2 for r2 in range(8)]
    rows += [r1 * 16 for r1 in range(8, 16)]
    assert len(rows) == _CAND_ROWS
    return jnp.broadcast_to(jnp.asarray(rows, F32)[:, None], (_CAND_ROWS, ROUTE_CHUNK))


def _peer_topk(st, tmk):
    T = st.shape[-1]
    tmk = min(tmk, T)
    spec = pl.BlockSpec((PEER_HEADS, N_KEYS, tmk), lambda i: (0, 0, i))
    shp = jax.ShapeDtypeStruct((PEER_HEADS, N_KEYS, T), GDT)
    shp32 = jax.ShapeDtypeStruct((PEER_HEADS, N_KEYS, T), F32)
    return pl.pallas_call(
        functools.partial(_peer_topk_kernel, n_chunks=tmk // ROUTE_CHUNK),
        grid=(T // tmk,),
        in_specs=[pl.BlockSpec((2 * PEER_HEADS, N_KEYS, tmk), lambda i: (0, 0, i)),
                  pl.BlockSpec((_CAND_ROWS, ROUTE_CHUNK), lambda i: (0, 0))],
        out_specs=[spec] * 4,
        out_shape=[shp, shp, shp32, shp32],
        compiler_params=_cparams(("parallel",)),
        name="peer_topk",
    )(st, _cand_fidx())


def _peer_dense_kernel(x_ref, xnt_ref, r2_ref, e2_ref, cnt_ref, c_ref, u_ref, vt_ref, out_ref,
                       acc_ref, ht_ref, at_ref, bc_ref, *, eb, tc, nsub):
    e = pl.program_id(1)
    tm = xnt_ref.shape[1]
    jpb = eb // N_KEYS
    sub = eb // nsub
    pk = bc_ref.shape[-2]

    @pl.when(e == 0)
    def _():
        acc_ref[...] = jnp.zeros_like(acc_ref)

    def fill(h, carry):
        for jb in range(jpb):
            j = e * jpb + jb
            bc_ref[0, h, jb] = jnp.broadcast_to(cnt_ref[h, pl.ds(j, 1), :], (pk, tm)).astype(GDT)
            bc_ref[1, h, jb] = jnp.broadcast_to(c_ref[h, pl.ds(j, 1), :], (pk, tm)).astype(GDT)
        return carry

    lax.fori_loop(0, PEER_HEADS, fill, 0)

    for sb in range(nsub):
        rows = slice(sb * sub, (sb + 1) * sub)
        ht_ref[rows, :] = _dot(u_ref[rows, :], xnt_ref[...])
    for sb in range(nsub):
        rows = slice(sb * sub, (sb + 1) * sub)
        for jb in range(sb * sub // N_KEYS, (sb + 1) * sub // N_KEYS):
            krows = slice(jb * N_KEYS, (jb + 1) * N_KEYS)
            for ci in range(tm // tc):
                cols = slice(ci * tc, (ci + 1) * tc)
                g = jnp.zeros((N_KEYS // pk, pk, tc), GDT)
                for h in range(PEER_HEADS):
                    r2 = r2_ref[h, :, cols].reshape(N_KEYS // pk, pk, tc)
                    e2 = e2_ref[h, :, cols].reshape(N_KEYS // pk, pk, tc)
                    g = g + jnp.where(r2 < bc_ref[0, h, jb, :, cols], e2 * bc_ref[1, h, jb, :, cols], 0)
                a = g.reshape(N_KEYS, tc) * jax.nn.gelu(ht_ref[krows, cols]).astype(GDT)
                at_ref[krows, cols] = a.astype(at_ref.dtype)
        acc_ref[...] += _dot(vt_ref[:, rows], at_ref[rows, :])

    @pl.when(e == pl.num_programs(1) - 1)
    def _():
        out_ref[...] = x_ref[...] + acc_ref[...].T


def _peer_dense(x2d, xnt, r2, e2, cnt, c, u_all, vt_all, layer, tm, eb, tc):
    T = x2d.shape[0]
    tm = min(tm, T)
    tc = min(tc, tm)
    tok3 = pl.BlockSpec((PEER_HEADS, N_KEYS, tm), lambda i, e: (0, 0, i))
    return pl.pallas_call(
        functools.partial(_peer_dense_kernel, eb=eb, tc=tc, nsub=eb // 256),
        grid=(T // tm, N_EXPERTS // eb),
        in_specs=[pl.BlockSpec((tm, D_MODEL), lambda i, e: (i, 0)),
                  pl.BlockSpec((D_MODEL, tm), lambda i, e: (0, i)),
                  tok3, tok3, tok3, tok3,
                  pl.BlockSpec((None, eb, D_MODEL), lambda i, e: (layer, e, 0)),
                  pl.BlockSpec((None, D_MODEL, eb), lambda i, e: (layer, 0, e))],
        out_specs=pl.BlockSpec((tm, D_MODEL), lambda i, e: (i, 0)),
        out_shape=jax.ShapeDtypeStruct((T, D_MODEL), F32),
        scratch_shapes=[pltpu.VMEM((D_MODEL, tm), F32),
                        pltpu.VMEM((eb, tm), F32),
                        pltpu.VMEM((eb, tm), CDT),
                        pltpu.VMEM((2, PEER_HEADS, eb // N_KEYS, GATE_TILE_ROWS, tm), GDT)],
        compiler_params=_cparams(("parallel", "arbitrary"), vmem=PEER_DENSE_VMEM_LIMIT),
        name="peer_dense",
    )(x2d, xnt, r2, e2, cnt, c, u_all, vt_all)


def _pad_heads(w, width):
    r = w.shape[0]
    w = w.reshape(r, MLA_HEADS, width)
    w = jnp.pad(w, ((0, 0), (0, 0), (0, HEAD_PAD - width)))
    return w.reshape(r, MLA_HEADS * HEAD_PAD)


def _layer_weights(l, norm1_g, w_in, q_lora_g, kv_lora_g, w_uq, w_ukv, q_head_g, k_head_g, w_o_mla,
                   pool_w, pool_scale, w_pool_out, gate_bias, w_out, norm2_g, peer_wq, peer_keys):
    wi = w_in[l]
    c0 = Q_LORA
    c1 = c0 + KV_LORA
    c2 = c1 + QK_ROPE
    c3 = c2 + POOL_WIDTH
    zeros = lambda n: jnp.zeros((D_MODEL, n), wi.dtype)
    w_in_pad = jnp.concatenate([wi[:, :c1], zeros(QK_NOPE), wi[:, c1:c2], zeros(LANES - QK_HEAD),
                                wi[:, c2:]], axis=1)
    wkv = w_ukv[l].reshape(KV_LORA, MLA_HEADS, QK_NOPE + V_HEAD)
    w_uk = _pad_heads(wkv[:, :, :QK_NOPE].reshape(KV_LORA, MLA_HEADS * QK_NOPE), QK_NOPE)
    w_uv = wkv[:, :, QK_NOPE:].reshape(KV_LORA, MLA_WIDTH)
    pad_g = lambda g: jnp.pad(g, (0, LANES - QK_HEAD)).reshape(1, LANES)
    return {
        "g1": norm1_g[l].reshape(1, D_MODEL),
        "w_in": w_in_pad.astype(CDT),
        "qlg": q_lora_g[l].reshape(1, Q_LORA),
        "kvlg": kv_lora_g[l].reshape(1, KV_LORA),
        "w_uq": _pad_heads(w_uq[l], QK_HEAD).astype(CDT),
        "w_uk": w_uk.astype(CDT),
        "w_uv": w_uv.astype(CDT),
        "qg": pad_g(q_head_g[l] * (1.0 / math.sqrt(QK_HEAD))),
        "kg": pad_g(k_head_g[l]),
        "gate_bias": gate_bias[l].reshape(1, 2 * D_MODEL),
        "w_o_mla": w_o_mla[l].astype(CDT),
        "pool_w": pool_w[l].astype(CDT),
        "pool_scale": pool_scale[l].reshape(1, POOL_WIDTH),
        "w_pool_out": w_pool_out[l].astype(CDT),
        "w_out": w_out[l].astype(CDT),
        "g2": norm2_g[l].reshape(1, D_MODEL),
        "peer_wq": peer_wq[l].astype(CDT),
        "peer_keys": peer_keys[l].reshape(2 * PEER_HEADS, N_KEYS, PEER_HALF).astype(CDT),
    }


def kernel(x, positions, norm1_g, w_in, q_lora_g, kv_lora_g, w_uq, w_ukv, q_head_g, k_head_g, w_o_mla,
           pool_w, pool_scale, w_pool_out, gate_bias, w_out, norm2_g, peer_wq, peer_keys, peer_u, peer_v):
    B, S, D = x.shape
    assert D == D_MODEL and S % LANES == 0
    T = B * S
    depth = norm1_g.shape[0]
    ropes = _rope_tables(positions)
    x2d = x.reshape(T, D)
    u_all = peer_u.astype(CDT)
    vt_all = jnp.swapaxes(peer_v, 1, 2).astype(CDT)
    for l in range(depth):
        lw = _layer_weights(l, norm1_g, w_in, q_lora_g, kv_lora_g, w_uq, w_ukv, q_head_g, k_head_g,
                            w_o_mla, pool_w, pool_scale, w_pool_out, gate_bias, w_out, norm2_g,
                            peer_wq, peer_keys)
        q, k, v, p, gates = _in_proj(x2d, lw, ropes, tm=256)
        o = _attention(q, k, v, B, S, tq=256)
        y = _pool(p, lw, B, S)
        x2d, xnt, st = _out_proj(x2d, o, y, gates, lw, tm=256)
        r2, e2, cnt, c = _peer_topk(st, tmk=512)
        x2d = _peer_dense(x2d, xnt, r2, e2, cnt, c, u_all, vt_all, l, tm=512, eb=2048, tc=256)
    return x2d.reshape(B, S, D)
```

```python
import functools
import math

import jax
import jax.numpy as jnp
import numpy as np
from jax import lax
from jax.experimental import pallas as pl
from jax.experimental.pallas import tpu as pltpu

D_MODEL = 1024
MLA_HEADS = 8
Q_LORA = 384
KV_LORA = 256
QK_NOPE = 64
QK_ROPE = 32
QK_HEAD = QK_NOPE + QK_ROPE
V_HEAD = 64
MLA_WIDTH = MLA_HEADS * V_HEAD
ROPE_BASE = 10000.0
POOL_WINDOWS = (2, 4, 8, 16)
POOL_WIDTH = 512
POOL_GC = 128
PEER_HEADS = 8
N_KEYS = 128
N_EXPERTS = N_KEYS * N_KEYS
PEER_HALF = 128
PEER_TOPK = 16
RMS_EPS = 1e-6

LANES = 128
GATE_TILE_ROWS = 16
ROW_SUBTILE = 256
ROUTE_CHUNK = 256
HEAD_PAD = LANES
IN_PAD = Q_LORA + KV_LORA + LANES + POOL_WIDTH + 2 * D_MODEL
VMEM_LIMIT = 48 * 1024 * 1024
PEER_DENSE_VMEM_LIMIT = 56 * 1024 * 1024

CDT = jnp.bfloat16
GDT = jnp.bfloat16
F32 = jnp.float32
NEG_INF = float("-inf")
_FLT_MAX = float(np.finfo(np.float32).max)
_FLT_TOP_ULP = 2.0 ** 104

_CAND_ROWS = 16 + 8 * 7 + 8
_FIDX_INVALID = 1 << 20


def _cparams(sem, vmem=VMEM_LIMIT, flags=None):
    return pltpu.CompilerParams(dimension_semantics=sem, vmem_limit_bytes=vmem, flags=flags)


def _rms(x, g):
    return x * lax.rsqrt(jnp.mean(x * x, axis=-1, keepdims=True) + RMS_EPS) * g


def _dot(a, b):
    return jnp.dot(a, b, preferred_element_type=F32)


def _dot_nt(a, b):
    return lax.dot_general(a, b, (((1,), (1,)), ((), ())), preferred_element_type=F32)


def _rope_kernel(pos_ref, inv_ref, c_ref, s1_ref, s2_ref):
    ang = pos_ref[...] * inv_ref[...]
    lane = lax.broadcasted_iota(jnp.int32, ang.shape, 1)
    cos = jnp.cos(ang)
    sin = jnp.sin(ang)
    half = QK_ROPE // 2
    c_ref[...] = jnp.where(lane < QK_NOPE, 1.0, jnp.where(lane < QK_HEAD, cos, 0.0))
    s1_ref[...] = jnp.where((lane >= QK_NOPE) & (lane < QK_NOPE + half), -sin, 0.0)
    s2_ref[...] = jnp.where((lane >= QK_NOPE + half) & (lane < QK_HEAD), sin, 0.0)


def _rope_tables(positions):
    T = positions.size
    tm = min(T, 2048)
    pos = jnp.broadcast_to(positions.reshape(T, 1).astype(F32), (T, LANES))
    inv = ROPE_BASE ** (-jnp.arange(0, QK_ROPE, 2, dtype=F32) / QK_ROPE)
    inv_pat = jnp.concatenate([jnp.zeros((QK_NOPE,), F32), inv, inv,
                               jnp.zeros((LANES - QK_HEAD,), F32)]).reshape(1, LANES)
    spec = pl.BlockSpec((tm, LANES), lambda i: (i, 0))
    return pl.pallas_call(
        _rope_kernel,
        grid=(T // tm,),
        in_specs=[spec, pl.BlockSpec((1, LANES), lambda i: (0, 0))],
        out_specs=[spec, spec, spec],
        out_shape=[jax.ShapeDtypeStruct((T, LANES), F32)] * 3,
        compiler_params=_cparams(("parallel",)),
        name="rope_tables",
    )(pos, inv_pat)


def _in_proj_kernel(x_ref, g1_ref, win_ref, qlg_ref, kvlg_ref, wuq_ref, wuk_ref, wuv_ref,
                    c_ref, s1_ref, s2_ref, qg_ref, kg_ref, bias_ref,
                    q_out, k_out, v_out, p_out, gate_out, *, sub):
    half = QK_ROPE // 2
    qg = qg_ref[...]
    kg = kg_ref[...]

    def head_norm(xh, g):
        ms = jnp.sum(xh * xh, axis=-1, keepdims=True) * (1.0 / QK_HEAD)
        return xh * lax.rsqrt(ms + RMS_EPS) * g

    for r0 in range(0, x_ref.shape[0], sub):
        rows = slice(r0, r0 + sub)
        xn = _rms(x_ref[rows, :], g1_ref[...])
        y = _dot(xn.astype(CDT), win_ref[...])
        o = 0
        cq = y[:, o:o + Q_LORA]; o += Q_LORA
        ckv = y[:, o:o + KV_LORA]; o += KV_LORA
        kpe = y[:, o:o + LANES]; o += LANES
        p_out[rows, :] = y[:, o:o + POOL_WIDTH]; o += POOL_WIDTH
        gate_out[rows, :] = jax.nn.sigmoid(y[:, o:] + bias_ref[...]).astype(gate_out.dtype)

        cqn = _rms(cq, qlg_ref[...]).astype(CDT)
        ckvn = _rms(ckv, kvlg_ref[...]).astype(CDT)
        q_raw = _dot(cqn, wuq_ref[...])
        k_raw = _dot(ckvn, wuk_ref[...])
        v_out[rows, :] = _dot(ckvn, wuv_ref[...]).astype(v_out.dtype)

        cpat = c_ref[rows, :]
        s1pat = s1_ref[rows, :]
        s2pat = s2_ref[rows, :]

        def rope(xh, cpat=cpat, s1pat=s1pat, s2pat=s2pat):
            return (xh * cpat + pltpu.roll(xh, LANES - half, 1) * s1pat
                    + pltpu.roll(xh, half, 1) * s2pat)

        kpe_r = rope(kpe)
        for h in range(MLA_HEADS):
            sl = slice(h * HEAD_PAD, (h + 1) * HEAD_PAD)
            q_out[rows, sl] = head_norm(rope(q_raw[:, sl]), qg).astype(q_out.dtype)
            k_out[rows, sl] = head_norm(k_raw[:, sl] + kpe_r, kg).astype(k_out.dtype)


def _in_proj(x2d, lw, ropes, tm):
    T = x2d.shape[0]
    tm = min(tm, T)
    row = lambda w: pl.BlockSpec((tm, w), lambda i: (i, 0))
    full = lambda a: pl.BlockSpec(a.shape, lambda i: (0,) * a.ndim)
    consts = [lw["g1"], lw["w_in"], lw["qlg"], lw["kvlg"], lw["w_uq"], lw["w_uk"], lw["w_uv"]]
    tail = [lw["qg"], lw["kg"], lw["gate_bias"]]
    return pl.pallas_call(
        functools.partial(_in_proj_kernel, sub=min(tm, ROW_SUBTILE)),
        grid=(T // tm,),
        in_specs=[row(D_MODEL)] + [full(a) for a in consts] + [row(LANES)] * 3 + [full(a) for a in tail],
        out_specs=[row(MLA_HEADS * HEAD_PAD), row(MLA_HEADS * HEAD_PAD), row(MLA_WIDTH),
                   row(POOL_WIDTH), row(2 * D_MODEL)],
        out_shape=[jax.ShapeDtypeStruct((T, MLA_HEADS * HEAD_PAD), CDT),
                   jax.ShapeDtypeStruct((T, MLA_HEADS * HEAD_PAD), CDT),
                   jax.ShapeDtypeStruct((T, MLA_WIDTH), CDT),
                   jax.ShapeDtypeStruct((T, POOL_WIDTH), F32),
                   jax.ShapeDtypeStruct((T, 2 * D_MODEL), CDT)],
        compiler_params=_cparams(("parallel",)),
        name="in_proj",
    )(x2d, *consts, *ropes, *tail)


def _attn_kernel(q_ref, k_ref, v_ref, o_ref):
    outs = []
    for j in range(MLA_HEADS):
        q = q_ref[:, j * HEAD_PAD:(j + 1) * HEAD_PAD]
        k = k_ref[:, j * HEAD_PAD:(j + 1) * HEAD_PAD]
        v = v_ref[:, j * V_HEAD:(j + 1) * V_HEAD]
        s = _dot_nt(q, k)
        m = jnp.max(s, axis=-1, keepdims=True)
        p = jnp.exp(s - m)
        l = jnp.sum(p, axis=-1, keepdims=True)
        outs.append(_dot(p.astype(CDT), v) / l)
    o_ref[...] = jnp.concatenate(outs, axis=-1).astype(o_ref.dtype)


def _attention(q, k, v, B, S, tq):
    tq = min(tq, S)
    nq = S // tq
    return pl.pallas_call(
        _attn_kernel,
        grid=(B, nq),
        in_specs=[pl.BlockSpec((tq, MLA_HEADS * HEAD_PAD), lambda b, i: (b * nq + i, 0)),
                  pl.BlockSpec((S, MLA_HEADS * HEAD_PAD), lambda b, i: (b, 0)),
                  pl.BlockSpec((S, MLA_WIDTH), lambda b, i: (b, 0))],
        out_specs=pl.BlockSpec((tq, MLA_WIDTH), lambda b, i: (b * nq + i, 0)),
        out_shape=jax.ShapeDtypeStruct((B * S, MLA_WIDTH), CDT),
        compiler_params=_cparams(("parallel", "parallel")),
        name="attention",
    )(q, k, v)


def _pool_kernel(p_ref, pw_ref, ps_ref, y_ref):
    S = p_ref.shape[0]
    t = lax.broadcasted_iota(jnp.int32, (S, POOL_GC), 0)
    for g, w in enumerate(POOL_WINDOWS):
        half = w // 2
        sl = slice(g * POOL_GC, (g + 1) * POOL_GC)
        pg = p_ref[:, sl]
        acc = pg
        for d in range(-half, half):
            if d == 0:
                continue
            shifted = pltpu.roll(pg, (-d) % S, 0)
            valid = (t + d >= 0) & (t + d < S)
            acc = acc + jnp.where(valid, shifted, 0.0)
        count = (jnp.minimum(t + half, S) - jnp.maximum(t - half, 0)).astype(F32)
        mixed = acc / count - pg
        yg = _dot(mixed.astype(CDT), pw_ref[g]) * ps_ref[:, sl]
        y_ref[:, sl] = yg.astype(y_ref.dtype)


def _pool(p, lw, B, S):
    return pl.pallas_call(
        _pool_kernel,
        grid=(B,),
        in_specs=[pl.BlockSpec((S, POOL_WIDTH), lambda b: (b, 0)),
                  pl.BlockSpec(lw["pool_w"].shape, lambda b: (0, 0, 0)),
                  pl.BlockSpec((1, POOL_WIDTH), lambda b: (0, 0))],
        out_specs=pl.BlockSpec((S, POOL_WIDTH), lambda b: (b, 0)),
        out_shape=jax.ShapeDtypeStruct((B * S, POOL_WIDTH), CDT),
        compiler_params=_cparams(("parallel",)),
        name="pool",
    )(p, lw["pool_w"], lw["pool_scale"])


def _out_proj_kernel(x_ref, o_ref, y_ref, gate_ref, wo_ref, wpo_ref, wout_ref, g2_ref, wq_ref, keys_ref,
                     xnew_out, xnt_out, st_out, *, sub):
    for r0 in range(0, x_ref.shape[0], sub):
        rows = slice(r0, r0 + sub)
        a = _dot(o_ref[rows, :], wo_ref[...])
        bp = _dot(y_ref[rows, :], wpo_ref[...])
        ga = gate_ref[rows, :D_MODEL].astype(F32)
        gb = gate_ref[rows, D_MODEL:].astype(F32)
        mix = ga * a + gb * bp
        xnew = x_ref[rows, :] + _dot(mix.astype(CDT), wout_ref[...])
        xnew_out[rows, :] = xnew
        xn = _rms(xnew, g2_ref[...])
        xnt_out[:, rows] = xn.T.astype(xnt_out.dtype)
        qp = _dot(xn.astype(CDT), wq_ref[...])
        for hp in range(2 * PEER_HEADS):
            qh = qp[:, hp * PEER_HALF:(hp + 1) * PEER_HALF].astype(CDT)
            st_out[hp, :, rows] = _dot_nt(keys_ref[hp], qh)


def _out_proj(x2d, o, y, gates, lw, tm):
    T = x2d.shape[0]
    tm = min(tm, T)
    row = lambda w: pl.BlockSpec((tm, w), lambda i: (i, 0))
    full = lambda a: pl.BlockSpec(a.shape, lambda i: (0,) * a.ndim)
    consts = [lw["w_o_mla"], lw["w_pool_out"], lw["w_out"], lw["g2"], lw["peer_wq"], lw["peer_keys"]]
    return pl.pallas_call(
        functools.partial(_out_proj_kernel, sub=min(tm, ROW_SUBTILE)),
        grid=(T // tm,),
        in_specs=[row(D_MODEL), row(MLA_WIDTH), row(POOL_WIDTH), row(2 * D_MODEL)] + [full(a) for a in consts],
        out_specs=[row(D_MODEL),
                   pl.BlockSpec((D_MODEL, tm), lambda i: (0, i)),
                   pl.BlockSpec((2 * PEER_HEADS, N_KEYS, tm), lambda i: (0, 0, i))],
        out_shape=[jax.ShapeDtypeStruct((T, D_MODEL), F32),
                   jax.ShapeDtypeStruct((D_MODEL, T), CDT),
                   jax.ShapeDtypeStruct((2 * PEER_HEADS, N_KEYS, T), F32)],
        compiler_params=_cparams(("parallel",)),
        name="out_proj",
    )(x2d, o, y, gates, *consts)


def _top16(s, key_iota, row16, exact):
    work = s
    rank = jnp.full(s.shape, float(PEER_TOPK), F32)
    tops = jnp.zeros((PEER_TOPK, s.shape[1]), F32)
    for r in range(PEER_TOPK):
        m = jnp.max(work, axis=0, keepdims=True)
        sel = work == m
        if exact:
            first = jnp.min(jnp.where(sel, key_iota, float(N_KEYS)), axis=0, keepdims=True)
            sel = key_iota == first
            rank = jnp.where(sel, float(r), rank)
            work = jnp.where(sel, NEG_INF, work)
        else:
            work = jnp.where(sel, -_FLT_MAX + r * _FLT_TOP_ULP, work)
        tops = jnp.where(row16 == r, m, tops)
    if not exact:
        rank = jnp.minimum((work + _FLT_MAX) * (1.0 / _FLT_TOP_ULP), float(PEER_TOPK))
    return rank, tops


def _cand_grid(ta, tb, combine):
    pieces = [combine(ta[0:1], tb)]
    for r1 in range(1, 8):
        pieces.append(combine(ta[r1:r1 + 1], tb[0:8]))
    pieces.append(combine(ta[8:16], tb[0:1]))
    return jnp.concatenate(pieces, axis=0)


def _count_true(mask):
    return jnp.sum(jnp.where(mask, 1.0, 0.0), axis=0, keepdims=True)


def _route(s1, s2, key_iota, row16, fidx, exact):
    rank1, ta = _top16(s1, key_iota, row16, exact)
    rank2, tb = _top16(s2, key_iota, row16, exact)
    a0 = ta[0:1]
    b0 = tb[0:1]
    cand0 = jnp.where(fidx < _FIDX_INVALID, _cand_grid(ta, tb, lambda a, b: a + b), NEG_INF)
    ecand = _cand_grid(jnp.exp(ta - a0), jnp.exp(tb - b0), lambda a, b: a * b)
    cand = cand0
    if exact:
        selected = jnp.zeros(cand.shape, F32)
        for _ in range(PEER_TOPK):
            m = jnp.max(cand, axis=0, keepdims=True)
            first = jnp.min(jnp.where(cand == m, fidx, float(_FIDX_INVALID)), axis=0, keepdims=True)
            sel = fidx == first
            selected = jnp.where(sel, 1.0, selected)
            cand = jnp.where(sel, NEG_INF, cand)
        bad = jnp.zeros_like(a0)
    else:
        for _ in range(PEER_TOPK):
            m = jnp.max(cand, axis=0, keepdims=True)
            cand = jnp.where(cand == m, NEG_INF, cand)
        selected = jnp.where(cand0 >= m, 1.0, 0.0)
        k = float(PEER_TOPK)
        bad = jnp.where((_count_true(rank1 < k) != k) | (_count_true(rank2 < k) != k)
                        | (jnp.sum(selected, axis=0, keepdims=True) != k), 1.0, 0.0)
    z = jnp.sum(selected * ecand, axis=0, keepdims=True)
    cntr = [jnp.sum(selected[0:16], axis=0, keepdims=True)]
    for r1 in range(1, 8):
        cntr.append(jnp.sum(selected[8 + 8 * r1:16 + 8 * r1], axis=0, keepdims=True))
    for r1 in range(8, 16):
        cntr.append(selected[64 + r1:65 + r1])
    cntr_rows = jnp.concatenate(cntr[0:8] + [selected[72:80]], axis=0)
    cnt = jnp.zeros(s1.shape, F32)
    for v in range(1, 5):
        ranks_with_v = jnp.sum(jnp.where(cntr_rows >= float(v), 1.0, 0.0), axis=0, keepdims=True)
        cnt = jnp.where(rank1 < ranks_with_v, float(v), cnt)
    for r1 in range(3):
        cnt = jnp.where(rank1 == float(r1), cntr[r1], cnt)
    return rank2, jnp.exp(s2 - b0), cnt, jnp.exp(s1 - a0) / z, bad


def _peer_topk_kernel(s_ref, fidx_ref, r2_out, e2_out, cnt_out, c_out, *, n_chunks):
    key_iota = lax.broadcasted_iota(jnp.int32, (N_KEYS, ROUTE_CHUNK), 0).astype(F32)
    row16 = lax.broadcasted_iota(jnp.int32, (PEER_TOPK, ROUTE_CHUNK), 0)
    fidx = fidx_ref[...]

    def body(it, carry):
        h = it // n_chunks
        cols = pl.ds(pl.multiple_of((it % n_chunks) * ROUTE_CHUNK, ROUTE_CHUNK), ROUTE_CHUNK)
        s1 = s_ref[2 * h, :, cols]
        s2 = s_ref[2 * h + 1, :, cols]

        def store(r2, e2, cnt, c):
            r2_out[h, :, cols] = r2.astype(r2_out.dtype)
            e2_out[h, :, cols] = e2.astype(e2_out.dtype)
            cnt_out[h, :, cols] = cnt.astype(cnt_out.dtype)
            c_out[h, :, cols] = c.astype(c_out.dtype)

        *fast, bad = _route(s1, s2, key_iota, row16, fidx, exact=False)
        store(*fast)

        @pl.when(jnp.max(bad) > 0.0)
        def _():
            *slow, _ = _route(s1, s2, key_iota, row16, fidx, exact=True)
            store(*slow)

        return carry

    lax.fori_loop(0, PEER_HEADS * n_chunks, body, 0)


def _cand_fidx():
    rows = []
    rows += [0 * 16 + r2 for r2 in range(16)]
    for r1 in range(1, 8):
        n = PEER_TOPK // (r1 + 1)
        rows += [r1 * 16 + r2 if r2 < n else _FIDX_INVALID + r1 * 16 + r2 for r2 in range(8)]
    rows += [r1 * 16 for r1 in range(8, 16)]
    assert len(rows) == _CAND_ROWS
    return jnp.broadcast_to(jnp.asarray(rows, F32)[:, None], (_CAND_ROWS, ROUTE_CHUNK))


def _peer_topk(st, tmk):
    T = st.shape[-1]
    tmk = min(tmk, T)
    spec = pl.BlockSpec((PEER_HEADS, N_KEYS, tmk), lambda i: (0, 0, i))
    shp = jax.ShapeDtypeStruct((PEER_HEADS, N_KEYS, T), GDT)
    shp32 = jax.ShapeDtypeStruct((PEER_HEADS, N_KEYS, T), F32)
    return pl.pallas_call(
        functools.partial(_peer_topk_kernel, n_chunks=tmk // ROUTE_CHUNK),
        grid=(T // tmk,),
        in_specs=[pl.BlockSpec((2 * PEER_HEADS, N_KEYS, tmk), lambda i: (0, 0, i)),
                  pl.BlockSpec((_CAND_ROWS, ROUTE_CHUNK), lambda i: (0, 0))],
        out_specs=[spec] * 4,
        out_shape=[shp, shp, shp32, shp32],
        compiler_params=_cparams(("parallel",)),
        name="peer_topk",
    )(st, _cand_fidx())


def _peer_dense_kernel(x_ref, xnt_ref, r2_ref, e2_ref, cnt_ref, c_ref, u_ref, vt_ref, out_ref,
                       acc_ref, ht_ref, at_ref, bc_ref, *, eb, tc, nsub):
    e = pl.program_id(1)
    tm = xnt_ref.shape[1]
    jpb = eb // N_KEYS
    sub = eb // nsub
    pk = bc_ref.shape[-2]

    @pl.when(e == 0)
    def _():
        acc_ref[...] = jnp.zeros_like(acc_ref)

    def fill(h, carry):
        for jb in range(jpb):
            j = e * jpb + jb
            bc_ref[0, h, jb] = jnp.broadcast_to(cnt_ref[h, pl.ds(j, 1), :], (pk, tm)).astype(GDT)
            bc_ref[1, h, jb] = jnp.broadcast_to(c_ref[h, pl.ds(j, 1), :], (pk, tm)).astype(GDT)
        return carry

    lax.fori_loop(0, PEER_HEADS, fill, 0)

    for sb in range(nsub):
        rows = slice(sb * sub, (sb + 1) * sub)
        ht_ref[rows, :] = _dot(u_ref[rows, :], xnt_ref[...])
    for sb in range(nsub):
        rows = slice(sb * sub, (sb + 1) * sub)
        for jb in range(sb * sub // N_KEYS, (sb + 1) * sub // N_KEYS):
            krows = slice(jb * N_KEYS, (jb + 1) * N_KEYS)
            for ci in range(tm // tc):
                cols = slice(ci * tc, (ci + 1) * tc)
                g = jnp.zeros((N_KEYS // pk, pk, tc), GDT)
                for h in range(PEER_HEADS):
                    r2 = r2_ref[h, :, cols].reshape(N_KEYS // pk, pk, tc)
                    e2 = e2_ref[h, :, cols].reshape(N_KEYS // pk, pk, tc)
                    g = g + jnp.where(r2 < bc_ref[0, h, jb, :, cols], e2 * bc_ref[1, h, jb, :, cols], 0)
                a = g.reshape(N_KEYS, tc) * jax.nn.gelu(ht_ref[krows, cols]).astype(GDT)
                at_ref[krows, cols] = a.astype(at_ref.dtype)
        acc_ref[...] += _dot(vt_ref[:, rows], at_ref[rows, :])

    @pl.when(e == pl.num_programs(1) - 1)
    def _():
        out_ref[...] = x_ref[...] + acc_ref[...].T


def _peer_dense(x2d, xnt, r2, e2, cnt, c, u_all, vt_all, layer, tm, eb, tc):
    T = x2d.shape[0]
    tm = min(tm, T)
    tc = min(tc, tm)
    tok3 = pl.BlockSpec((PEER_HEADS, N_KEYS, tm), lambda i, e: (0, 0, i))
    return pl.pallas_call(
        functools.partial(_peer_dense_kernel, eb=eb, tc=tc, nsub=eb // 256),
        grid=(T // tm, N_EXPERTS // eb),
        in_specs=[pl.BlockSpec((tm, D_MODEL), lambda i, e: (i, 0)),
                  pl.BlockSpec((D_MODEL, tm), lambda i, e: (0, i)),
                  tok3, tok3, tok3, tok3,
                  pl.BlockSpec((None, eb, D_MODEL), lambda i, e: (layer, e, 0)),
                  pl.BlockSpec((None, D_MODEL, eb), lambda i, e: (layer, 0, e))],
        out_specs=pl.BlockSpec((tm, D_MODEL), lambda i, e: (i, 0)),
        out_shape=jax.ShapeDtypeStruct((T, D_MODEL), F32),
        scratch_shapes=[pltpu.VMEM((D_MODEL, tm), F32),
                        pltpu.VMEM((eb, tm), F32),
                        pltpu.VMEM((eb, tm), CDT),
                        pltpu.VMEM((2, PEER_HEADS, eb // N_KEYS, GATE_TILE_ROWS, tm), GDT)],
        compiler_params=_cparams(("parallel", "arbitrary"), vmem=PEER_DENSE_VMEM_LIMIT),
        name="peer_dense",
    )(x2d, xnt, r2, e2, cnt, c, u_all, vt_all)


def _pad_heads(w, width):
    r = w.shape[0]
    w = w.reshape(r, MLA_HEADS, width)
    w = jnp.pad(w, ((0, 0), (0, 0), (0, HEAD_PAD - width)))
    return w.reshape(r, MLA_HEADS * HEAD_PAD)


def _layer_weights(l, norm1_g, w_in, q_lora_g, kv_lora_g, w_uq, w_ukv, q_head_g, k_head_g, w_o_mla,
                   pool_w, pool_scale, w_pool_out, gate_bias, w_out, norm2_g, peer_wq, peer_keys):
    wi = w_in[l]
    c0 = Q_LORA
    c1 = c0 + KV_LORA
    c2 = c1 + QK_ROPE
    c3 = c2 + POOL_WIDTH
    zeros = lambda n: jnp.zeros((D_MODEL, n), wi.dtype)
    w_in_pad = jnp.concatenate([wi[:, :c1], zeros(QK_NOPE), wi[:, c1:c2], zeros(LANES - QK_HEAD),
                                wi[:, c2:]], axis=1)
    wkv = w_ukv[l].reshape(KV_LORA, MLA_HEADS, QK_NOPE + V_HEAD)
    w_uk = _pad_heads(wkv[:, :, :QK_NOPE].reshape(KV_LORA, MLA_HEADS * QK_NOPE), QK_NOPE)
    w_uv = wkv[:, :, QK_NOPE:].reshape(KV_LORA, MLA_WIDTH)
    pad_g = lambda g: jnp.pad(g, (0, LANES - QK_HEAD)).reshape(1, LANES)
    return {
        "g1": norm1_g[l].reshape(1, D_MODEL),
        "w_in": w_in_pad.astype(CDT),
        "qlg": q_lora_g[l].reshape(1, Q_LORA),
        "kvlg": kv_lora_g[l].reshape(1, KV_LORA),
        "w_uq": _pad_heads(w_uq[l], QK_HEAD).astype(CDT),
        "w_uk": w_uk.astype(CDT),
        "w_uv": w_uv.astype(CDT),
        "qg": pad_g(q_head_g[l] * (1.0 / math.sqrt(QK_HEAD))),
        "kg": pad_g(k_head_g[l]),
        "gate_bias": gate_bias[l].reshape(1, 2 * D_MODEL),
        "w_o_mla": w_o_mla[l].astype(CDT),
        "pool_w": pool_w[l].astype(CDT),
        "pool_scale": pool_scale[l].reshape(1, POOL_WIDTH),
        "w_pool_out": w_pool_out[l].astype(CDT),
        "w_out": w_out[l].astype(CDT),
        "g2": norm2_g[l].reshape(1, D_MODEL),
        "peer_wq": peer_wq[l].astype(CDT),
        "peer_keys": peer_keys[l].reshape(2 * PEER_HEADS, N_KEYS, PEER_HALF).astype(CDT),
    }


def kernel(x, positions, norm1_g, w_in, q_lora_g, kv_lora_g, w_uq, w_ukv, q_head_g, k_head_g, w_o_mla,
           pool_w, pool_scale, w_pool_out, gate_bias, w_out, norm2_g, peer_wq, peer_keys, peer_u, peer_v):
    B, S, D = x.shape
    assert D == D_MODEL and S % LANES == 0
    T = B * S
    depth = norm1_g.shape[0]
    ropes = _rope_tables(positions)
    x2d = x.reshape(T, D)
    u_all = peer_u.astype(CDT)
    vt_all = jnp.swapaxes(peer_v, 1, 2).astype(CDT)
    for l in range(depth):
        lw = _layer_weights(l, norm1_g, w_in, q_lora_g, kv_lora_g, w_uq, w_ukv, q_head_g, k_head_g,
                            w_o_mla, pool_w, pool_scale, w_pool_out, gate_bias, w_out, norm2_g,
                            peer_wq, peer_keys)
        q, k, v, p, gates = _in_proj(x2d, lw, ropes, tm=1024)
        o = _attention(q, k, v, B, S, tq=256)
        y = _pool(p, lw, B, S)
        x2d, xnt, st = _out_proj(x2d, o, y, gates, lw, tm=512)
        r2, e2, cnt, c = _peer_topk(st, tmk=512)
        x2d = _peer_dense(x2d, xnt, r2, e2, cnt, c, u_all, vt_all, l, tm=512, eb=2048, tc=256)
    return x2d.reshape(B, S, D)
```

```python
import functools
import math

import jax
import jax.numpy as jnp
import numpy as np
from jax import lax
from jax.experimental import pallas as pl
from jax.experimental.pallas import tpu as pltpu

D_MODEL = 1024
MLA_HEADS = 8
Q_LORA = 384
KV_LORA = 256
QK_NOPE = 64
QK_ROPE = 32
QK_HEAD = QK_NOPE + QK_ROPE
V_HEAD = 64
MLA_WIDTH = MLA_HEADS * V_HEAD
ROPE_BASE = 10000.0
POOL_WINDOWS = (2, 4, 8, 16)
POOL_WIDTH = 512
POOL_GC = 128
PEER_HEADS = 8
N_KEYS = 128
N_EXPERTS = N_KEYS * N_KEYS
PEER_HALF = 128
PEER_TOPK = 16
RMS_EPS = 1e-6

LANES = 128
GATE_TILE_ROWS = 16
ROW_SUBTILE = 256
ROUTE_CHUNK = 256
HEAD_PAD = LANES
IN_PAD = Q_LORA + KV_LORA + LANES + POOL_WIDTH + 2 * D_MODEL
VMEM_LIMIT = 48 * 1024 * 1024
PEER_DENSE_VMEM_LIMIT = 56 * 1024 * 1024

IN_PROJ_TM = 1024
ATTN_TQ = 256
OUT_PROJ_TM = 512
ROUTE_TM = 512
PEER_TM = 512
PEER_EB = 2048
PEER_CHAIN = 512
PEER_TC = 256

CDT = jnp.bfloat16
GDT = jnp.bfloat16
F32 = jnp.float32
NEG_INF = float("-inf")
_FLT_MAX = float(np.finfo(np.float32).max)
_FLT_TOP_ULP = 2.0 ** 104

_CAND_ROWS = 16 + 8 * 7 + 8
_FIDX_INVALID = 1 << 20


def _cparams(sem, vmem=VMEM_LIMIT):
    return pltpu.CompilerParams(dimension_semantics=sem, vmem_limit_bytes=vmem)


def _rms(x, g):
    return x * lax.rsqrt(jnp.mean(x * x, axis=-1, keepdims=True) + RMS_EPS) * g


def _dot(a, b):
    return jnp.dot(a, b, preferred_element_type=F32)


def _dot_nt(a, b):
    return lax.dot_general(a, b, (((1,), (1,)), ((), ())), preferred_element_type=F32)


def _rope_kernel(pos_ref, inv_ref, c_ref, s1_ref, s2_ref):
    ang = pos_ref[...] * inv_ref[...]
    lane = lax.broadcasted_iota(jnp.int32, ang.shape, 1)
    cos = jnp.cos(ang)
    sin = jnp.sin(ang)
    half = QK_ROPE // 2
    c_ref[...] = jnp.where(lane < QK_NOPE, 1.0, jnp.where(lane < QK_HEAD, cos, 0.0))
    s1_ref[...] = jnp.where((lane >= QK_NOPE) & (lane < QK_NOPE + half), -sin, 0.0)
    s2_ref[...] = jnp.where((lane >= QK_NOPE + half) & (lane < QK_HEAD), sin, 0.0)


def _rope_tables(positions):
    T = positions.size
    tm = min(T, 2048)
    pos = jnp.broadcast_to(positions.reshape(T, 1).astype(F32), (T, LANES))
    inv = ROPE_BASE ** (-jnp.arange(0, QK_ROPE, 2, dtype=F32) / QK_ROPE)
    inv_pat = jnp.concatenate([jnp.zeros((QK_NOPE,), F32), inv, inv,
                               jnp.zeros((LANES - QK_HEAD,), F32)]).reshape(1, LANES)
    spec = pl.BlockSpec((tm, LANES), lambda i: (i, 0))
    return pl.pallas_call(
        _rope_kernel,
        grid=(T // tm,),
        in_specs=[spec, pl.BlockSpec((1, LANES), lambda i: (0, 0))],
        out_specs=[spec, spec, spec],
        out_shape=[jax.ShapeDtypeStruct((T, LANES), F32)] * 3,
        compiler_params=_cparams(("parallel",)),
        name="rope_tables",
    )(pos, inv_pat)


def _in_proj_kernel(x_ref, g1_ref, win_ref, qlg_ref, kvlg_ref, wuq_ref, wuk_ref, wuv_ref,
                    c_ref, s1_ref, s2_ref, qg_ref, kg_ref, bias_ref,
                    q_out, k_out, v_out, p_out, gate_out, *, sub):
    half = QK_ROPE // 2
    qg = qg_ref[...]
    kg = kg_ref[...]

    def head_norm(xh, g):
        ms = jnp.sum(xh * xh, axis=-1, keepdims=True) * (1.0 / QK_HEAD)
        return xh * lax.rsqrt(ms + RMS_EPS) * g

    for r0 in range(0, x_ref.shape[0], sub):
        rows = slice(r0, r0 + sub)
        xn = _rms(x_ref[rows, :], g1_ref[...])
        y = _dot(xn.astype(CDT), win_ref[...])
        o = 0
        cq = y[:, o:o + Q_LORA]; o += Q_LORA
        ckv = y[:, o:o + KV_LORA]; o += KV_LORA
        kpe = y[:, o:o + LANES]; o += LANES
        p_out[rows, :] = y[:, o:o + POOL_WIDTH]; o += POOL_WIDTH
        gate_out[rows, :] = jax.nn.sigmoid(y[:, o:] + bias_ref[...]).astype(gate_out.dtype)

        cqn = _rms(cq, qlg_ref[...]).astype(CDT)
        ckvn = _rms(ckv, kvlg_ref[...]).astype(CDT)
        q_raw = _dot(cqn, wuq_ref[...])
        k_raw = _dot(ckvn, wuk_ref[...])
        v_out[rows, :] = _dot(ckvn, wuv_ref[...]).astype(v_out.dtype)

        cpat = c_ref[rows, :]
        s1pat = s1_ref[rows, :]
        s2pat = s2_ref[rows, :]

        def rope(xh, cpat=cpat, s1pat=s1pat, s2pat=s2pat):
            return (xh * cpat + pltpu.roll(xh, LANES - half, 1) * s1pat
                    + pltpu.roll(xh, half, 1) * s2pat)

        kpe_r = rope(kpe)
        for h in range(MLA_HEADS):
            sl = slice(h * HEAD_PAD, (h + 1) * HEAD_PAD)
            q_out[rows, sl] = head_norm(rope(q_raw[:, sl]), qg).astype(q_out.dtype)
            k_out[rows, sl] = head_norm(k_raw[:, sl] + kpe_r, kg).astype(k_out.dtype)


def _in_proj(x2d, lw, ropes, tm):
    T = x2d.shape[0]
    tm = min(tm, T)
    row = lambda w: pl.BlockSpec((tm, w), lambda i: (i, 0))
    full = lambda a: pl.BlockSpec(a.shape, lambda i: (0,) * a.ndim)
    consts = [lw["g1"], lw["w_in"], lw["qlg"], lw["kvlg"], lw["w_uq"], lw["w_uk"], lw["w_uv"]]
    tail = [lw["qg"], lw["kg"], lw["gate_bias"]]
    return pl.pallas_call(
        functools.partial(_in_proj_kernel, sub=min(tm, ROW_SUBTILE)),
        grid=(T // tm,),
        in_specs=[row(D_MODEL)] + [full(a) for a in consts] + [row(LANES)] * 3 + [full(a) for a in tail],
        out_specs=[row(MLA_HEADS * HEAD_PAD), row(MLA_HEADS * HEAD_PAD), row(MLA_WIDTH),
                   row(POOL_WIDTH), row(2 * D_MODEL)],
        out_shape=[jax.ShapeDtypeStruct((T, MLA_HEADS * HEAD_PAD), CDT),
                   jax.ShapeDtypeStruct((T, MLA_HEADS * HEAD_PAD), CDT),
                   jax.ShapeDtypeStruct((T, MLA_WIDTH), CDT),
                   jax.ShapeDtypeStruct((T, POOL_WIDTH), F32),
                   jax.ShapeDtypeStruct((T, 2 * D_MODEL), CDT)],
        compiler_params=_cparams(("parallel",)),
        name="in_proj",
    )(x2d, *consts, *ropes, *tail)


def _attn_kernel(q_ref, k_ref, v_ref, o_ref):
    outs = []
    for j in range(MLA_HEADS):
        q = q_ref[:, j * HEAD_PAD:(j + 1) * HEAD_PAD]
        k = k_ref[:, j * HEAD_PAD:(j + 1) * HEAD_PAD]
        v = v_ref[:, j * V_HEAD:(j + 1) * V_HEAD]
        s = _dot_nt(q, k)
        m = jnp.max(s, axis=-1, keepdims=True)
        p = jnp.exp(s - m)
        l = jnp.sum(p, axis=-1, keepdims=True)
        outs.append(_dot(p.astype(CDT), v) / l)
    o_ref[...] = jnp.concatenate(outs, axis=-1).astype(o_ref.dtype)


def _attention(q, k, v, B, S, tq):
    tq = min(tq, S)
    nq = S // tq
    return pl.pallas_call(
        _attn_kernel,
        grid=(B, nq),
        in_specs=[pl.BlockSpec((tq, MLA_HEADS * HEAD_PAD), lambda b, i: (b * nq + i, 0)),
                  pl.BlockSpec((S, MLA_HEADS * HEAD_PAD), lambda b, i: (b, 0)),
                  pl.BlockSpec((S, MLA_WIDTH), lambda b, i: (b, 0))],
        out_specs=pl.BlockSpec((tq, MLA_WIDTH), lambda b, i: (b * nq + i, 0)),
        out_shape=jax.ShapeDtypeStruct((B * S, MLA_WIDTH), CDT),
        compiler_params=_cparams(("parallel", "parallel")),
        name="attention",
    )(q, k, v)


def _pool_kernel(p_ref, pw_ref, ps_ref, y_ref):
    S = p_ref.shape[0]
    t = lax.broadcasted_iota(jnp.int32, (S, POOL_GC), 0)
    for g, w in enumerate(POOL_WINDOWS):
        half = w // 2
        sl = slice(g * POOL_GC, (g + 1) * POOL_GC)
        pg = p_ref[:, sl]
        acc = pg
        for d in range(-half, half):
            if d == 0:
                continue
            shifted = pltpu.roll(pg, (-d) % S, 0)
            valid = (t + d >= 0) & (t + d < S)
            acc = acc + jnp.where(valid, shifted, 0.0)
        count = (jnp.minimum(t + half, S) - jnp.maximum(t - half, 0)).astype(F32)
        mixed = acc / count - pg
        yg = _dot(mixed.astype(CDT), pw_ref[g]) * ps_ref[:, sl]
        y_ref[:, sl] = yg.astype(y_ref.dtype)


def _pool(p, lw, B, S):
    return pl.pallas_call(
        _pool_kernel,
        grid=(B,),
        in_specs=[pl.BlockSpec((S, POOL_WIDTH), lambda b: (b, 0)),
                  pl.BlockSpec(lw["pool_w"].shape, lambda b: (0, 0, 0)),
                  pl.BlockSpec((1, POOL_WIDTH), lambda b: (0, 0))],
        out_specs=pl.BlockSpec((S, POOL_WIDTH), lambda b: (b, 0)),
        out_shape=jax.ShapeDtypeStruct((B * S, POOL_WIDTH), CDT),
        compiler_params=_cparams(("parallel",)),
        name="pool",
    )(p, lw["pool_w"], lw["pool_scale"])


def _out_proj_kernel(x_ref, o_ref, y_ref, gate_ref, wo_ref, wpo_ref, wout_ref, g2_ref, wq_ref, keys_ref,
                     xnew_out, xnt_out, st_out, *, sub):
    for r0 in range(0, x_ref.shape[0], sub):
        rows = slice(r0, r0 + sub)
        a = _dot(o_ref[rows, :], wo_ref[...])
        bp = _dot(y_ref[rows, :], wpo_ref[...])
        ga = gate_ref[rows, :D_MODEL].astype(F32)
        gb = gate_ref[rows, D_MODEL:].astype(F32)
        mix = ga * a + gb * bp
        xnew = x_ref[rows, :] + _dot(mix.astype(CDT), wout_ref[...])
        xnew_out[rows, :] = xnew
        xn = _rms(xnew, g2_ref[...])
        xnt_out[:, rows] = xn.T.astype(xnt_out.dtype)
        qp = _dot(xn.astype(CDT), wq_ref[...])
        for hp in range(2 * PEER_HEADS):
            qh = qp[:, hp * PEER_HALF:(hp + 1) * PEER_HALF].astype(CDT)
            st_out[hp, :, rows] = _dot_nt(keys_ref[hp], qh)


def _out_proj(x2d, o, y, gates, lw, tm):
    T = x2d.shape[0]
    tm = min(tm, T)
    row = lambda w: pl.BlockSpec((tm, w), lambda i: (i, 0))
    full = lambda a: pl.BlockSpec(a.shape, lambda i: (0,) * a.ndim)
    consts = [lw["w_o_mla"], lw["w_pool_out"], lw["w_out"], lw["g2"], lw["peer_wq"], lw["peer_keys"]]
    return pl.pallas_call(
        functools.partial(_out_proj_kernel, sub=min(tm, ROW_SUBTILE)),
        grid=(T // tm,),
        in_specs=[row(D_MODEL), row(MLA_WIDTH), row(POOL_WIDTH), row(2 * D_MODEL)] + [full(a) for a in consts],
        out_specs=[row(D_MODEL),
                   pl.BlockSpec((D_MODEL, tm), lambda i: (0, i)),
                   pl.BlockSpec((2 * PEER_HEADS, N_KEYS, tm), lambda i: (0, 0, i))],
        out_shape=[jax.ShapeDtypeStruct((T, D_MODEL), F32),
                   jax.ShapeDtypeStruct((D_MODEL, T), CDT),
                   jax.ShapeDtypeStruct((2 * PEER_HEADS, N_KEYS, T), F32)],
        compiler_params=_cparams(("parallel",)),
        name="out_proj",
    )(x2d, o, y, gates, *consts)


def _top16(s, key_iota, row16, exact):
    work = s
    rank = jnp.full(s.shape, float(PEER_TOPK), F32)
    tops = jnp.zeros((PEER_TOPK, s.shape[1]), F32)
    for r in range(PEER_TOPK):
        m = jnp.max(work, axis=0, keepdims=True)
        sel = work == m
        if exact:
            first = jnp.min(jnp.where(sel, key_iota, float(N_KEYS)), axis=0, keepdims=True)
            sel = key_iota == first
            rank = jnp.where(sel, float(r), rank)
            work = jnp.where(sel, NEG_INF, work)
        else:
            work = jnp.where(sel, -_FLT_MAX + r * _FLT_TOP_ULP, work)
        tops = jnp.where(row16 == r, m, tops)
    if not exact:
        rank = jnp.minimum((work + _FLT_MAX) * (1.0 / _FLT_TOP_ULP), float(PEER_TOPK))
    return rank, tops


def _cand_grid(ta, tb, combine):
    pieces = [combine(ta[0:1], tb)]
    for r1 in range(1, 8):
        pieces.append(combine(ta[r1:r1 + 1], tb[0:8]))
    pieces.append(combine(ta[8:16], tb[0:1]))
    return jnp.concatenate(pieces, axis=0)


def _count_true(mask):
    return jnp.sum(jnp.where(mask, 1.0, 0.0), axis=0, keepdims=True)


def _route(s1, s2, key_iota, row16, fidx, exact):
    rank1, ta = _top16(s1, key_iota, row16, exact)
    rank2, tb = _top16(s2, key_iota, row16, exact)
    a0 = ta[0:1]
    b0 = tb[0:1]
    cand0 = jnp.where(fidx < _FIDX_INVALID, _cand_grid(ta, tb, lambda a, b: a + b), NEG_INF)
    ecand = _cand_grid(jnp.exp(ta - a0), jnp.exp(tb - b0), lambda a, b: a * b)
    cand = cand0
    if exact:
        selected = jnp.zeros(cand.shape, F32)
        for _ in range(PEER_TOPK):
            m = jnp.max(cand, axis=0, keepdims=True)
            first = jnp.min(jnp.where(cand == m, fidx, float(_FIDX_INVALID)), axis=0, keepdims=True)
            sel = fidx == first
            selected = jnp.where(sel, 1.0, selected)
            cand = jnp.where(sel, NEG_INF, cand)
        bad = jnp.zeros_like(a0)
    else:
        for _ in range(PEER_TOPK):
            m = jnp.max(cand, axis=0, keepdims=True)
            cand = jnp.where(cand == m, NEG_INF, cand)
        selected = jnp.where(cand0 >= m, 1.0, 0.0)
        k = float(PEER_TOPK)
        bad = jnp.where((_count_true(rank1 < k) != k) | (_count_true(rank2 < k) != k)
                        | (jnp.sum(selected, axis=0, keepdims=True) != k), 1.0, 0.0)
    z = jnp.sum(selected * ecand, axis=0, keepdims=True)
    cntr = [jnp.sum(selected[0:16], axis=0, keepdims=True)]
    for r1 in range(1, 8):
        cntr.append(jnp.sum(selected[8 + 8 * r1:16 + 8 * r1], axis=0, keepdims=True))
    for r1 in range(8, 16):
        cntr.append(selected[64 + r1:65 + r1])
    cntr_rows = jnp.concatenate(cntr[0:8] + [selected[72:80]], axis=0)
    cnt = jnp.zeros(s1.shape, F32)
    for v in range(1, 5):
        ranks_with_v = jnp.sum(jnp.where(cntr_rows >= float(v), 1.0, 0.0), axis=0, keepdims=True)
        cnt = jnp.where(rank1 < ranks_with_v, float(v), cnt)
    for r1 in range(3):
        cnt = jnp.where(rank1 == float(r1), cntr[r1], cnt)
    return rank2, jnp.exp(s2 - b0), cnt, jnp.exp(s1 - a0) / z, bad


def _peer_topk_kernel(s_ref, fidx_ref, r2_out, e2_out, cnt_out, c_out, *, n_chunks):
    key_iota = lax.broadcasted_iota(jnp.int32, (N_KEYS, ROUTE_CHUNK), 0).astype(F32)
    row16 = lax.broadcasted_iota(jnp.int32, (PEER_TOPK, ROUTE_CHUNK), 0)
    fidx = fidx_ref[...]

    def body(it, carry):
        h = it // n_chunks
        cols = pl.ds(pl.multiple_of((it % n_chunks) * ROUTE_CHUNK, ROUTE_CHUNK), ROUTE_CHUNK)
        s1 = s_ref[2 * h, :, cols]
        s2 = s_ref[2 * h + 1, :, cols]

        def store(r2, e2, cnt, c):
            r2_out[h, :, cols] = r2.astype(r2_out.dtype)
            e2_out[h, :, cols] = e2.astype(e2_out.dtype)
            cnt_out[h, :, cols] = cnt.astype(cnt_out.dtype)
            c_out[h, :, cols] = c.astype(c_out.dtype)

        *fast, bad = _route(s1, s2, key_iota, row16, fidx, exact=False)
        store(*fast)

        @pl.when(jnp.max(bad) > 0.0)
        def _():
            *slow, _ = _route(s1, s2, key_iota, row16, fidx, exact=True)
            store(*slow)

        return carry

    lax.fori_loop(0, PEER_HEADS * n_chunks, body, 0)


def _cand_fidx():
    rows = []
    rows += [0 * 16 + r2 for r2 in range(16)]
    for r1 in range(1, 8):
        n = PEER_TOPK // (r1 + 1)
        rows += [r1 * 16 + r2 if r2 < n else _FIDX_INVALID + r1 * 16 + r2 for r2 in range(8)]
    rows += [r1 * 16 for r1 in range(8, 16)]
    assert len(rows) == _CAND_ROWS
    return jnp.broadcast_to(jnp.asarray(rows, F32)[:, None], (_CAND_ROWS, ROUTE_CHUNK))


def _peer_topk(st, tmk):
    T = st.shape[-1]
    tmk = min(tmk, T)
    spec = pl.BlockSpec((PEER_HEADS, N_KEYS, tmk), lambda i: (0, 0, i))
    shp = jax.ShapeDtypeStruct((PEER_HEADS, N_KEYS, T), GDT)
    shp32 = jax.ShapeDtypeStruct((PEER_HEADS, N_KEYS, T), F32)
    return pl.pallas_call(
        functools.partial(_peer_topk_kernel, n_chunks=tmk // ROUTE_CHUNK),
        grid=(T // tmk,),
        in_specs=[pl.BlockSpec((2 * PEER_HEADS, N_KEYS, tmk), lambda i: (0, 0, i)),
                  pl.BlockSpec((_CAND_ROWS, ROUTE_CHUNK), lambda i: (0, 0))],
        out_specs=[spec] * 4,
        out_shape=[shp, shp, shp32, shp32],
        compiler_params=_cparams(("parallel",)),
        name="peer_topk",
    )(st, _cand_fidx())


def _peer_dense_kernel(x_ref, xnt_ref, r2_ref, e2_ref, cnt_ref, c_ref, u_ref, vt_ref, out_ref,
                       acc_ref, ht_ref, at_ref, bc_ref, *, eb, tc, nsub):
    e = pl.program_id(1)
    tm = xnt_ref.shape[1]
    jpb = eb // N_KEYS
    sub = eb // nsub
    pk = bc_ref.shape[-2]

    @pl.when(e == 0)
    def _():
        acc_ref[...] = jnp.zeros_like(acc_ref)

    def fill(h, carry):
        for jb in range(jpb):
            j = e * jpb + jb
            bc_ref[0, h, jb] = jnp.broadcast_to(cnt_ref[h, pl.ds(j, 1), :], (pk, tm)).astype(GDT)
            bc_ref[1, h, jb] = jnp.broadcast_to(c_ref[h, pl.ds(j, 1), :], (pk, tm)).astype(GDT)
        return carry

    lax.fori_loop(0, PEER_HEADS, fill, 0)

    for sb in range(nsub):
        rows = slice(sb * sub, (sb + 1) * sub)
        ht_ref[rows, :] = _dot(u_ref[rows, :], xnt_ref[...])
    for sb in range(nsub):
        rows = slice(sb * sub, (sb + 1) * sub)
        for jb in range(sb * sub // N_KEYS, (sb + 1) * sub // N_KEYS):
            krows = slice(jb * N_KEYS, (jb + 1) * N_KEYS)
            for ci in range(tm // tc):
                cols = slice(ci * tc, (ci + 1) * tc)
                g = jnp.zeros((N_KEYS // pk, pk, tc), GDT)
                for h in range(PEER_HEADS):
                    r2 = r2_ref[h, :, cols].reshape(N_KEYS // pk, pk, tc)
                    e2 = e2_ref[h, :, cols].reshape(N_KEYS // pk, pk, tc)
                    g = g + jnp.where(r2 < bc_ref[0, h, jb, :, cols], e2 * bc_ref[1, h, jb, :, cols], 0)
                a = g.reshape(N_KEYS, tc) * jax.nn.gelu(ht_ref[krows, cols]).astype(GDT)
                at_ref[krows, cols] = a.astype(at_ref.dtype)
        acc_ref[...] += _dot(vt_ref[:, rows], at_ref[rows, :])

    @pl.when(e == pl.num_programs(1) - 1)
    def _():
        out_ref[...] = x_ref[...] + acc_ref[...].T


def _peer_dense(x2d, xnt, r2, e2, cnt, c, u_all, vt_all, layer, tm, eb, tc):
    T = x2d.shape[0]
    tm = min(tm, T)
    tc = min(tc, tm)
    tok3 = pl.BlockSpec((PEER_HEADS, N_KEYS, tm), lambda i, e: (0, 0, i))
    return pl.pallas_call(
        functools.partial(_peer_dense_kernel, eb=eb, tc=tc, nsub=eb // PEER_CHAIN),
        grid=(T // tm, N_EXPERTS // eb),
        in_specs=[pl.BlockSpec((tm, D_MODEL), lambda i, e: (i, 0)),
                  pl.BlockSpec((D_MODEL, tm), lambda i, e: (0, i)),
                  tok3, tok3, tok3, tok3,
                  pl.BlockSpec((None, eb, D_MODEL), lambda i, e: (layer, e, 0)),
                  pl.BlockSpec((None, D_MODEL, eb), lambda i, e: (layer, 0, e))],
        out_specs=pl.BlockSpec((tm, D_MODEL), lambda i, e: (i, 0)),
        out_shape=jax.ShapeDtypeStruct((T, D_MODEL), F32),
        scratch_shapes=[pltpu.VMEM((D_MODEL, tm), F32),
                        pltpu.VMEM((eb, tm), F32),
                        pltpu.VMEM((eb, tm), CDT),
                        pltpu.VMEM((2, PEER_HEADS, eb // N_KEYS, GATE_TILE_ROWS, tm), GDT)],
        compiler_params=_cparams(("parallel", "arbitrary"), vmem=PEER_DENSE_VMEM_LIMIT),
        name="peer_dense",
    )(x2d, xnt, r2, e2, cnt, c, u_all, vt_all)


def _pad_heads(w, width):
    r = w.shape[0]
    w = w.reshape(r, MLA_HEADS, width)
    w = jnp.pad(w, ((0, 0), (0, 0), (0, HEAD_PAD - width)))
    return w.reshape(r, MLA_HEADS * HEAD_PAD)


def _layer_weights(l, norm1_g, w_in, q_lora_g, kv_lora_g, w_uq, w_ukv, q_head_g, k_head_g, w_o_mla,
                   pool_w, pool_scale, w_pool_out, gate_bias, w_out, norm2_g, peer_wq, peer_keys):
    wi = w_in[l]
    c0 = Q_LORA
    c1 = c0 + KV_LORA
    c2 = c1 + QK_ROPE
    c3 = c2 + POOL_WIDTH
    zeros = lambda n: jnp.zeros((D_MODEL, n), wi.dtype)
    w_in_pad = jnp.concatenate([wi[:, :c1], zeros(QK_NOPE), wi[:, c1:c2], zeros(LANES - QK_HEAD),
                                wi[:, c2:]], axis=1)
    wkv = w_ukv[l].reshape(KV_LORA, MLA_HEADS, QK_NOPE + V_HEAD)
    w_uk = _pad_heads(wkv[:, :, :QK_NOPE].reshape(KV_LORA, MLA_HEADS * QK_NOPE), QK_NOPE)
    w_uv = wkv[:, :, QK_NOPE:].reshape(KV_LORA, MLA_WIDTH)
    pad_g = lambda g: jnp.pad(g, (0, LANES - QK_HEAD)).reshape(1, LANES)
    return {
        "g1": norm1_g[l].reshape(1, D_MODEL),
        "w_in": w_in_pad.astype(CDT),
        "qlg": q_lora_g[l].reshape(1, Q_LORA),
        "kvlg": kv_lora_g[l].reshape(1, KV_LORA),
        "w_uq": _pad_heads(w_uq[l], QK_HEAD).astype(CDT),
        "w_uk": w_uk.astype(CDT),
        "w_uv": w_uv.astype(CDT),
        "qg": pad_g(q_head_g[l] * (1.0 / math.sqrt(QK_HEAD))),
        "kg": pad_g(k_head_g[l]),
        "gate_bias": gate_bias[l].reshape(1, 2 * D_MODEL),
        "w_o_mla": w_o_mla[l].astype(CDT),
        "pool_w": pool_w[l].astype(CDT),
        "pool_scale": pool_scale[l].reshape(1, POOL_WIDTH),
        "w_pool_out": w_pool_out[l].astype(CDT),
        "w_out": w_out[l].astype(CDT),
        "g2": norm2_g[l].reshape(1, D_MODEL),
        "peer_wq": peer_wq[l].astype(CDT),
        "peer_keys": peer_keys[l].reshape(2 * PEER_HEADS, N_KEYS, PEER_HALF).astype(CDT),
    }


def kernel(x, positions, norm1_g, w_in, q_lora_g, kv_lora_g, w_uq, w_ukv, q_head_g, k_head_g, w_o_mla,
           pool_w, pool_scale, w_pool_out, gate_bias, w_out, norm2_g, peer_wq, peer_keys, peer_u, peer_v):
    B, S, D = x.shape
    assert D == D_MODEL and S % LANES == 0
    T = B * S
    depth = norm1_g.shape[0]
    ropes = _rope_tables(positions)
    x2d = x.reshape(T, D)
    u_all = peer_u.astype(CDT)
    vt_all = jnp.swapaxes(peer_v, 1, 2).astype(CDT)
    for l in range(depth):
        lw = _layer_weights(l, norm1_g, w_in, q_lora_g, kv_lora_g, w_uq, w_ukv, q_head_g, k_head_g,
                            w_o_mla, pool_w, pool_scale, w_pool_out, gate_bias, w_out, norm2_g,
                            peer_wq, peer_keys)
        q, k, v, p, gates = _in_proj(x2d, lw, ropes, tm=IN_PROJ_TM)
        o = _attention(q, k, v, B, S, tq=ATTN_TQ)
        y = _pool(p, lw, B, S)
        x2d, xnt, st = _out_proj(x2d, o, y, gates, lw, tm=OUT_PROJ_TM)
        r2, e2, cnt, c = _peer_topk(st, tmk=ROUTE_TM)
        x2d = _peer_dense(x2d, xnt, r2, e2, cnt, c, u_all, vt_all, l, tm=PEER_TM, eb=PEER_EB, tc=PEER_TC)
    return x2d.reshape(B, S, D)
```

```python
import functools
import math

import jax
import jax.numpy as jnp
import numpy as np
from jax import lax
from jax.experimental import pallas as pl
from jax.experimental.pallas import tpu as pltpu

D_MODEL = 1024
MLA_HEADS = 8
Q_LORA = 384
KV_LORA = 256
QK_NOPE = 64
QK_ROPE = 32
QK_HEAD = QK_NOPE + QK_ROPE
V_HEAD = 64
MLA_WIDTH = MLA_HEADS * V_HEAD
ROPE_BASE = 10000.0
POOL_WINDOWS = (2, 4, 8, 16)
POOL_WIDTH = 512
POOL_GC = 128
PEER_HEADS = 8
N_KEYS = 128
N_EXPERTS = N_KEYS * N_KEYS
PEER_HALF = 128
PEER_TOPK = 16
RMS_EPS = 1e-6

LANES = 128
GATE_TILE_ROWS = 16
ROW_SUBTILE = 256
ROUTE_CHUNK = 256
HEAD_PAD = LANES
IN_PAD = Q_LORA + KV_LORA + LANES + POOL_WIDTH + 2 * D_MODEL
VMEM_LIMIT = 48 * 1024 * 1024
PEER_DENSE_VMEM_LIMIT = 56 * 1024 * 1024

IN_PROJ_TM = 1024
ATTN_TQ = 256
OUT_PROJ_TM = 512
ROUTE_TM = 512
PEER_TM = 512
PEER_EB = 2048
PEER_CHAIN = 512
PEER_TC = 256

CDT = jnp.bfloat16
GDT = jnp.bfloat16
F32 = jnp.float32
NEG_INF = float("-inf")
_FLT_MAX = float(np.finfo(np.float32).max)
_FLT_TOP_ULP = 2.0 ** 104

_CAND_ROWS = 16 + 8 * 7 + 8
_FIDX_INVALID = 1 << 20


def _cparams(sem, vmem=VMEM_LIMIT):
    return pltpu.CompilerParams(dimension_semantics=sem, vmem_limit_bytes=vmem)


def _rms(x, g):
    return x * lax.rsqrt(jnp.mean(x * x, axis=-1, keepdims=True) + RMS_EPS) * g


def _dot(a, b):
    return jnp.dot(a, b, preferred_element_type=F32)


def _dot_nt(a, b):
    return lax.dot_general(a, b, (((1,), (1,)), ((), ())), preferred_element_type=F32)


def _dot_tn(a, b):
    return lax.dot_general(a, b, (((0,), (0,)), ((), ())), preferred_element_type=F32)


def _rope_kernel(pos_ref, inv_ref, c_ref, s1_ref, s2_ref):
    ang = pos_ref[...] * inv_ref[...]
    lane = lax.broadcasted_iota(jnp.int32, ang.shape, 1)
    cos = jnp.cos(ang)
    sin = jnp.sin(ang)
    half = QK_ROPE // 2
    c_ref[...] = jnp.where(lane < QK_NOPE, 1.0, jnp.where(lane < QK_HEAD, cos, 0.0))
    s1_ref[...] = jnp.where((lane >= QK_NOPE) & (lane < QK_NOPE + half), -sin, 0.0)
    s2_ref[...] = jnp.where((lane >= QK_NOPE + half) & (lane < QK_HEAD), sin, 0.0)


def _rope_tables(positions):
    T = positions.size
    tm = min(T, 2048)
    pos = jnp.broadcast_to(positions.reshape(T, 1).astype(F32), (T, LANES))
    inv = ROPE_BASE ** (-jnp.arange(0, QK_ROPE, 2, dtype=F32) / QK_ROPE)
    inv_pat = jnp.concatenate([jnp.zeros((QK_NOPE,), F32), inv, inv,
                               jnp.zeros((LANES - QK_HEAD,), F32)]).reshape(1, LANES)
    spec = pl.BlockSpec((tm, LANES), lambda i: (i, 0))
    return pl.pallas_call(
        _rope_kernel,
        grid=(T // tm,),
        in_specs=[spec, pl.BlockSpec((1, LANES), lambda i: (0, 0))],
        out_specs=[spec, spec, spec],
        out_shape=[jax.ShapeDtypeStruct((T, LANES), F32)] * 3,
        compiler_params=_cparams(("parallel",)),
        name="rope_tables",
    )(pos, inv_pat)


def _in_proj_kernel(x_ref, g1_ref, win_ref, qlg_ref, kvlg_ref, wuq_ref, wuk_ref, wuv_ref,
                    c_ref, s1_ref, s2_ref, qg_ref, kg_ref, bias_ref,
                    q_out, k_out, v_out, p_out, gate_out, *, sub):
    half = QK_ROPE // 2
    qg = qg_ref[...]
    kg = kg_ref[...]

    def head_norm(xh, g):
        ms = jnp.sum(xh * xh, axis=-1, keepdims=True) * (1.0 / QK_HEAD)
        return xh * lax.rsqrt(ms + RMS_EPS) * g

    for r0 in range(0, x_ref.shape[0], sub):
        rows = slice(r0, r0 + sub)
        xn = _rms(x_ref[rows, :], g1_ref[...])
        y = _dot(xn.astype(CDT), win_ref[...])
        o = 0
        cq = y[:, o:o + Q_LORA]; o += Q_LORA
        ckv = y[:, o:o + KV_LORA]; o += KV_LORA
        kpe = y[:, o:o + LANES]; o += LANES
        p_out[rows, :] = y[:, o:o + POOL_WIDTH]; o += POOL_WIDTH
        gate_out[rows, :] = jax.nn.sigmoid(y[:, o:] + bias_ref[...]).astype(gate_out.dtype)

        cqn = _rms(cq, qlg_ref[...]).astype(CDT)
        ckvn = _rms(ckv, kvlg_ref[...]).astype(CDT)
        q_raw = _dot(cqn, wuq_ref[...])
        k_raw = _dot(ckvn, wuk_ref[...])
        v_out[rows, :] = _dot(ckvn, wuv_ref[...]).astype(v_out.dtype)

        cpat = c_ref[rows, :]
        s1pat = s1_ref[rows, :]
        s2pat = s2_ref[rows, :]

        def rope(xh, cpat=cpat, s1pat=s1pat, s2pat=s2pat):
            return (xh * cpat + pltpu.roll(xh, LANES - half, 1) * s1pat
                    + pltpu.roll(xh, half, 1) * s2pat)

        kpe_r = rope(kpe)
        for h in range(MLA_HEADS):
            sl = slice(h * HEAD_PAD, (h + 1) * HEAD_PAD)
            q_out[rows, sl] = head_norm(rope(q_raw[:, sl]), qg).astype(q_out.dtype)
            k_out[rows, sl] = head_norm(k_raw[:, sl] + kpe_r, kg).astype(k_out.dtype)


def _in_proj(x2d, lw, ropes, tm):
    T = x2d.shape[0]
    tm = min(tm, T)
    row = lambda w: pl.BlockSpec((tm, w), lambda i: (i, 0))
    full = lambda a: pl.BlockSpec(a.shape, lambda i: (0,) * a.ndim)
    consts = [lw["g1"], lw["w_in"], lw["qlg"], lw["kvlg"], lw["w_uq"], lw["w_uk"], lw["w_uv"]]
    tail = [lw["qg"], lw["kg"], lw["gate_bias"]]
    return pl.pallas_call(
        functools.partial(_in_proj_kernel, sub=min(tm, ROW_SUBTILE)),
        grid=(T // tm,),
        in_specs=[row(D_MODEL)] + [full(a) for a in consts] + [row(LANES)] * 3 + [full(a) for a in tail],
        out_specs=[row(MLA_HEADS * HEAD_PAD), row(MLA_HEADS * HEAD_PAD), row(MLA_WIDTH),
                   row(POOL_WIDTH), row(2 * D_MODEL)],
        out_shape=[jax.ShapeDtypeStruct((T, MLA_HEADS * HEAD_PAD), CDT),
                   jax.ShapeDtypeStruct((T, MLA_HEADS * HEAD_PAD), CDT),
                   jax.ShapeDtypeStruct((T, MLA_WIDTH), CDT),
                   jax.ShapeDtypeStruct((T, POOL_WIDTH), F32),
                   jax.ShapeDtypeStruct((T, 2 * D_MODEL), CDT)],
        compiler_params=_cparams(("parallel",)),
        name="in_proj",
    )(x2d, *consts, *ropes, *tail)


def _attn_kernel(q_ref, k_ref, v_ref, o_ref):
    outs = []
    for j in range(MLA_HEADS):
        q = q_ref[:, j * HEAD_PAD:(j + 1) * HEAD_PAD]
        k = k_ref[:, j * HEAD_PAD:(j + 1) * HEAD_PAD]
        v = v_ref[:, j * V_HEAD:(j + 1) * V_HEAD]
        s = _dot_nt(q, k)
        m = jnp.max(s, axis=-1, keepdims=True)
        p = jnp.exp(s - m)
        l = jnp.sum(p, axis=-1, keepdims=True)
        outs.append(_dot(p.astype(CDT), v) / l)
    o_ref[...] = jnp.concatenate(outs, axis=-1).astype(o_ref.dtype)


def _attention(q, k, v, B, S, tq):
    tq = min(tq, S)
    nq = S // tq
    return pl.pallas_call(
        _attn_kernel,
        grid=(B, nq),
        in_specs=[pl.BlockSpec((tq, MLA_HEADS * HEAD_PAD), lambda b, i: (b * nq + i, 0)),
                  pl.BlockSpec((S, MLA_HEADS * HEAD_PAD), lambda b, i: (b, 0)),
                  pl.BlockSpec((S, MLA_WIDTH), lambda b, i: (b, 0))],
        out_specs=pl.BlockSpec((tq, MLA_WIDTH), lambda b, i: (b * nq + i, 0)),
        out_shape=jax.ShapeDtypeStruct((B * S, MLA_WIDTH), CDT),
        compiler_params=_cparams(("parallel", "parallel")),
        name="attention",
    )(q, k, v)


def _pool_kernel(p_ref, pw_ref, ps_ref, y_ref):
    S = p_ref.shape[0]
    t = lax.broadcasted_iota(jnp.int32, (S, POOL_GC), 0)
    for g, w in enumerate(POOL_WINDOWS):
        half = w // 2
        sl = slice(g * POOL_GC, (g + 1) * POOL_GC)
        pg = p_ref[:, sl]
        acc = pg
        for d in range(-half, half):
            if d == 0:
                continue
            shifted = pltpu.roll(pg, (-d) % S, 0)
            valid = (t + d >= 0) & (t + d < S)
            acc = acc + jnp.where(valid, shifted, 0.0)
        count = (jnp.minimum(t + half, S) - jnp.maximum(t - half, 0)).astype(F32)
        mixed = acc / count - pg
        yg = _dot(mixed.astype(CDT), pw_ref[g]) * ps_ref[:, sl]
        y_ref[:, sl] = yg.astype(y_ref.dtype)


def _pool(p, lw, B, S):
    return pl.pallas_call(
        _pool_kernel,
        grid=(B,),
        in_specs=[pl.BlockSpec((S, POOL_WIDTH), lambda b: (b, 0)),
                  pl.BlockSpec(lw["pool_w"].shape, lambda b: (0, 0, 0)),
                  pl.BlockSpec((1, POOL_WIDTH), lambda b: (0, 0))],
        out_specs=pl.BlockSpec((S, POOL_WIDTH), lambda b: (b, 0)),
        out_shape=jax.ShapeDtypeStruct((B * S, POOL_WIDTH), CDT),
        compiler_params=_cparams(("parallel",)),
        name="pool",
    )(p, lw["pool_w"], lw["pool_scale"])


def _out_proj_kernel(x_ref, o_ref, y_ref, gate_ref, wo_ref, wpo_ref, wout_ref, g2_ref, wq_ref, keys_ref,
                     xnew_out, xnt_out, st_out, *, sub):
    for r0 in range(0, x_ref.shape[0], sub):
        rows = slice(r0, r0 + sub)
        a = _dot(o_ref[rows, :], wo_ref[...])
        bp = _dot(y_ref[rows, :], wpo_ref[...])
        ga = gate_ref[rows, :D_MODEL].astype(F32)
        gb = gate_ref[rows, D_MODEL:].astype(F32)
        mix = ga * a + gb * bp
        xnew = x_ref[rows, :] + _dot(mix.astype(CDT), wout_ref[...])
        xnew_out[rows, :] = xnew
        xn = _rms(xnew, g2_ref[...])
        xnt_out[:, rows] = xn.T.astype(xnt_out.dtype)
        qp = _dot(xn.astype(CDT), wq_ref[...])
        for hp in range(2 * PEER_HEADS):
            qh = qp[:, hp * PEER_HALF:(hp + 1) * PEER_HALF].astype(CDT)
            st_out[hp, :, rows] = _dot_nt(keys_ref[hp], qh)


def _out_proj(x2d, o, y, gates, lw, tm):
    T = x2d.shape[0]
    tm = min(tm, T)
    row = lambda w: pl.BlockSpec((tm, w), lambda i: (i, 0))
    full = lambda a: pl.BlockSpec(a.shape, lambda i: (0,) * a.ndim)
    consts = [lw["w_o_mla"], lw["w_pool_out"], lw["w_out"], lw["g2"], lw["peer_wq"], lw["peer_keys"]]
    return pl.pallas_call(
        functools.partial(_out_proj_kernel, sub=min(tm, ROW_SUBTILE)),
        grid=(T // tm,),
        in_specs=[row(D_MODEL), row(MLA_WIDTH), row(POOL_WIDTH), row(2 * D_MODEL)] + [full(a) for a in consts],
        out_specs=[row(D_MODEL),
                   pl.BlockSpec((D_MODEL, tm), lambda i: (0, i)),
                   pl.BlockSpec((2 * PEER_HEADS, N_KEYS, tm), lambda i: (0, 0, i))],
        out_shape=[jax.ShapeDtypeStruct((T, D_MODEL), F32),
                   jax.ShapeDtypeStruct((D_MODEL, T), CDT),
                   jax.ShapeDtypeStruct((2 * PEER_HEADS, N_KEYS, T), F32)],
        compiler_params=_cparams(("parallel",)),
        name="out_proj",
    )(x2d, o, y, gates, *consts)


def _top16(s, key_iota, row16, exact):
    work = s
    rank = jnp.full(s.shape, float(PEER_TOPK), F32)
    tops = jnp.zeros((PEER_TOPK, s.shape[1]), F32)
    for r in range(PEER_TOPK):
        m = jnp.max(work, axis=0, keepdims=True)
        sel = work == m
        if exact:
            first = jnp.min(jnp.where(sel, key_iota, float(N_KEYS)), axis=0, keepdims=True)
            sel = key_iota == first
            rank = jnp.where(sel, float(r), rank)
            work = jnp.where(sel, NEG_INF, work)
        else:
            work = jnp.where(sel, -_FLT_MAX + r * _FLT_TOP_ULP, work)
        tops = jnp.where(row16 == r, m, tops)
    if not exact:
        rank = jnp.minimum((work + _FLT_MAX) * (1.0 / _FLT_TOP_ULP), float(PEER_TOPK))
    return rank, tops


def _cand_grid(ta, tb, combine):
    pieces = [combine(ta[0:1], tb)]
    for r1 in range(1, 8):
        pieces.append(combine(ta[r1:r1 + 1], tb[0:8]))
    pieces.append(combine(ta[8:16], tb[0:1]))
    return jnp.concatenate(pieces, axis=0)


def _count_true(mask):
    return jnp.sum(jnp.where(mask, 1.0, 0.0), axis=0, keepdims=True)


def _route(s1, s2, key_iota, row16, fidx, exact):
    rank1, ta = _top16(s1, key_iota, row16, exact)
    rank2, tb = _top16(s2, key_iota, row16, exact)
    a0 = ta[0:1]
    b0 = tb[0:1]
    cand0 = jnp.where(fidx < _FIDX_INVALID, _cand_grid(ta, tb, lambda a, b: a + b), NEG_INF)
    ecand = _cand_grid(jnp.exp(ta - a0), jnp.exp(tb - b0), lambda a, b: a * b)
    cand = cand0
    if exact:
        selected = jnp.zeros(cand.shape, F32)
        for _ in range(PEER_TOPK):
            m = jnp.max(cand, axis=0, keepdims=True)
            first = jnp.min(jnp.where(cand == m, fidx, float(_FIDX_INVALID)), axis=0, keepdims=True)
            sel = fidx == first
            selected = jnp.where(sel, 1.0, selected)
            cand = jnp.where(sel, NEG_INF, cand)
        bad = jnp.zeros_like(a0)
    else:
        for _ in range(PEER_TOPK):
            m = jnp.max(cand, axis=0, keepdims=True)
            cand = jnp.where(cand == m, NEG_INF, cand)
        selected = jnp.where(cand0 >= m, 1.0, 0.0)
        k = float(PEER_TOPK)
        bad = jnp.where((_count_true(rank1 < k) != k) | (_count_true(rank2 < k) != k)
                        | (jnp.sum(selected, axis=0, keepdims=True) != k), 1.0, 0.0)
    z = jnp.sum(selected * ecand, axis=0, keepdims=True)
    cntr = [jnp.sum(selected[0:16], axis=0, keepdims=True)]
    for r1 in range(1, 8):
        cntr.append(jnp.sum(selected[8 + 8 * r1:16 + 8 * r1], axis=0, keepdims=True))
    for r1 in range(8, 16):
        cntr.append(selected[64 + r1:65 + r1])
    cntr_rows = jnp.concatenate(cntr[0:8] + [selected[72:80]], axis=0)
    cnt = jnp.zeros(s1.shape, F32)
    for v in range(1, 5):
        ranks_with_v = jnp.sum(jnp.where(cntr_rows >= float(v), 1.0, 0.0), axis=0, keepdims=True)
        cnt = jnp.where(rank1 < ranks_with_v, float(v), cnt)
    for r1 in range(3):
        cnt = jnp.where(rank1 == float(r1), cntr[r1], cnt)
    return rank2, jnp.exp(s2 - b0), cnt, jnp.exp(s1 - a0) / z, bad


def _peer_topk_kernel(s_ref, fidx_ref, r2_out, e2_out, cnt_out, c_out, *, n_chunks):
    key_iota = lax.broadcasted_iota(jnp.int32, (N_KEYS, ROUTE_CHUNK), 0).astype(F32)
    row16 = lax.broadcasted_iota(jnp.int32, (PEER_TOPK, ROUTE_CHUNK), 0)
    fidx = fidx_ref[...]

    def body(it, carry):
        h = it // n_chunks
        cols = pl.ds(pl.multiple_of((it % n_chunks) * ROUTE_CHUNK, ROUTE_CHUNK), ROUTE_CHUNK)
        s1 = s_ref[2 * h, :, cols]
        s2 = s_ref[2 * h + 1, :, cols]

        def store(r2, e2, cnt, c):
            r2_out[h, :, cols] = r2.astype(r2_out.dtype)
            e2_out[h, :, cols] = e2.astype(e2_out.dtype)
            cnt_out[h, :, cols] = cnt.astype(cnt_out.dtype)
            c_out[h, :, cols] = c.astype(c_out.dtype)

        *fast, bad = _route(s1, s2, key_iota, row16, fidx, exact=False)
        store(*fast)

        @pl.when(jnp.max(bad) > 0.0)
        def _():
            *slow, _ = _route(s1, s2, key_iota, row16, fidx, exact=True)
            store(*slow)

        return carry

    lax.fori_loop(0, PEER_HEADS * n_chunks, body, 0)


def _cand_fidx():
    rows = []
    rows += [0 * 16 + r2 for r2 in range(16)]
    for r1 in range(1, 8):
        n = PEER_TOPK // (r1 + 1)
        rows += [r1 * 16 + r2 if r2 < n else _FIDX_INVALID + r1 * 16 + r2 for r2 in range(8)]
    rows += [r1 * 16 for r1 in range(8, 16)]
    assert len(rows) == _CAND_ROWS
    return jnp.broadcast_to(jnp.asarray(rows, F32)[:, None], (_CAND_ROWS, ROUTE_CHUNK))


def _peer_topk(st, tmk):
    T = st.shape[-1]
    tmk = min(tmk, T)
    spec = pl.BlockSpec((PEER_HEADS, N_KEYS, tmk), lambda i: (0, 0, i))
    shp = jax.ShapeDtypeStruct((PEER_HEADS, N_KEYS, T), GDT)
    shp32 = jax.ShapeDtypeStruct((PEER_HEADS, N_KEYS, T), F32)
    return pl.pallas_call(
        functools.partial(_peer_topk_kernel, n_chunks=tmk // ROUTE_CHUNK),
        grid=(T // tmk,),
        in_specs=[pl.BlockSpec((2 * PEER_HEADS, N_KEYS, tmk), lambda i: (0, 0, i)),
                  pl.BlockSpec((_CAND_ROWS, ROUTE_CHUNK), lambda i: (0, 0))],
        out_specs=[spec] * 4,
        out_shape=[shp, shp, shp32, shp32],
        compiler_params=_cparams(("parallel",)),
        name="peer_topk",
    )(st, _cand_fidx())


def _peer_dense_kernel(x_ref, xnt_ref, r2_ref, e2_ref, cnt_ref, c_ref, u_ref, v_ref, out_ref,
                       acc_ref, ht_ref, at_ref, bc_ref, *, eb, tc, nsub):
    e = pl.program_id(1)
    tm = xnt_ref.shape[1]
    jpb = eb // N_KEYS
    sub = eb // nsub
    pk = bc_ref.shape[-2]

    @pl.when(e == 0)
    def _():
        acc_ref[...] = jnp.zeros_like(acc_ref)

    def fill(h, carry):
        for jb in range(jpb):
            j = e * jpb + jb
            bc_ref[0, h, jb] = jnp.broadcast_to(cnt_ref[h, pl.ds(j, 1), :], (pk, tm)).astype(GDT)
            bc_ref[1, h, jb] = jnp.broadcast_to(c_ref[h, pl.ds(j, 1), :], (pk, tm)).astype(GDT)
        return carry

    lax.fori_loop(0, PEER_HEADS, fill, 0)

    for sb in range(nsub):
        rows = slice(sb * sub, (sb + 1) * sub)
        ht_ref[rows, :] = _dot(u_ref[rows, :], xnt_ref[...])
    for sb in range(nsub):
        rows = slice(sb * sub, (sb + 1) * sub)
        for jb in range(sb * sub // N_KEYS, (sb + 1) * sub // N_KEYS):
            krows = slice(jb * N_KEYS, (jb + 1) * N_KEYS)
            for ci in range(tm // tc):
                cols = slice(ci * tc, (ci + 1) * tc)
                g = jnp.zeros((N_KEYS // pk, pk, tc), GDT)
                for h in range(PEER_HEADS):
                    r2 = r2_ref[h, :, cols].reshape(N_KEYS // pk, pk, tc)
                    e2 = e2_ref[h, :, cols].reshape(N_KEYS // pk, pk, tc)
                    g = g + jnp.where(r2 < bc_ref[0, h, jb, :, cols], e2 * bc_ref[1, h, jb, :, cols], 0)
                a = g.reshape(N_KEYS, tc) * jax.nn.gelu(ht_ref[krows, cols]).astype(GDT)
                at_ref[krows, cols] = a.astype(at_ref.dtype)
        acc_ref[...] += _dot_tn(v_ref[rows, :], at_ref[rows, :])

    @pl.when(e == pl.num_programs(1) - 1)
    def _():
        out_ref[...] = x_ref[...] + acc_ref[...].T


def _peer_dense(x2d, xnt, r2, e2, cnt, c, u_all, v_all, layer, tm, eb, tc):
    T = x2d.shape[0]
    tm = min(tm, T)
    tc = min(tc, tm)
    tok3 = pl.BlockSpec((PEER_HEADS, N_KEYS, tm), lambda i, e: (0, 0, i))
    return pl.pallas_call(
        functools.partial(_peer_dense_kernel, eb=eb, tc=tc, nsub=eb // PEER_CHAIN),
        grid=(T // tm, N_EXPERTS // eb),
        in_specs=[pl.BlockSpec((tm, D_MODEL), lambda i, e: (i, 0)),
                  pl.BlockSpec((D_MODEL, tm), lambda i, e: (0, i)),
                  tok3, tok3, tok3, tok3,
                  pl.BlockSpec((None, eb, D_MODEL), lambda i, e: (layer, e, 0)),
                  pl.BlockSpec((None, eb, D_MODEL), lambda i, e: (layer, e, 0))],
        out_specs=pl.BlockSpec((tm, D_MODEL), lambda i, e: (i, 0)),
        out_shape=jax.ShapeDtypeStruct((T, D_MODEL), F32),
        scratch_shapes=[pltpu.VMEM((D_MODEL, tm), F32),
                        pltpu.VMEM((eb, tm), F32),
                        pltpu.VMEM((eb, tm), CDT),
                        pltpu.VMEM((2, PEER_HEADS, eb // N_KEYS, GATE_TILE_ROWS, tm), GDT)],
        compiler_params=_cparams(("parallel", "arbitrary"), vmem=PEER_DENSE_VMEM_LIMIT),
        name="peer_dense",
    )(x2d, xnt, r2, e2, cnt, c, u_all, v_all)


def _pad_heads(w, width):
    r = w.shape[0]
    w = w.reshape(r, MLA_HEADS, width)
    w = jnp.pad(w, ((0, 0), (0, 0), (0, HEAD_PAD - width)))
    return w.reshape(r, MLA_HEADS * HEAD_PAD)


def _layer_weights(l, norm1_g, w_in, q_lora_g, kv_lora_g, w_uq, w_ukv, q_head_g, k_head_g, w_o_mla,
                   pool_w, pool_scale, w_pool_out, gate_bias, w_out, norm2_g, peer_wq, peer_keys):
    wi = w_in[l]
    c0 = Q_LORA
    c1 = c0 + KV_LORA
    c2 = c1 + QK_ROPE
    c3 = c2 + POOL_WIDTH
    zeros = lambda n: jnp.zeros((D_MODEL, n), wi.dtype)
    w_in_pad = jnp.concatenate([wi[:, :c1], zeros(QK_NOPE), wi[:, c1:c2], zeros(LANES - QK_HEAD),
                                wi[:, c2:]], axis=1)
    wkv = w_ukv[l].reshape(KV_LORA, MLA_HEADS, QK_NOPE + V_HEAD)
    w_uk = _pad_heads(wkv[:, :, :QK_NOPE].reshape(KV_LORA, MLA_HEADS * QK_NOPE), QK_NOPE)
    w_uv = wkv[:, :, QK_NOPE:].reshape(KV_LORA, MLA_WIDTH)
    pad_g = lambda g: jnp.pad(g, (0, LANES - QK_HEAD)).reshape(1, LANES)
    return {
        "g1": norm1_g[l].reshape(1, D_MODEL),
        "w_in": w_in_pad.astype(CDT),
        "qlg": q_lora_g[l].reshape(1, Q_LORA),
        "kvlg": kv_lora_g[l].reshape(1, KV_LORA),
        "w_uq": _pad_heads(w_uq[l], QK_HEAD).astype(CDT),
        "w_uk": w_uk.astype(CDT),
        "w_uv": w_uv.astype(CDT),
        "qg": pad_g(q_head_g[l] * (1.0 / math.sqrt(QK_HEAD))),
        "kg": pad_g(k_head_g[l]),
        "gate_bias": gate_bias[l].reshape(1, 2 * D_MODEL),
        "w_o_mla": w_o_mla[l].astype(CDT),
        "pool_w": pool_w[l].astype(CDT),
        "pool_scale": pool_scale[l].reshape(1, POOL_WIDTH),
        "w_pool_out": w_pool_out[l].astype(CDT),
        "w_out": w_out[l].astype(CDT),
        "g2": norm2_g[l].reshape(1, D_MODEL),
        "peer_wq": peer_wq[l].astype(CDT),
        "peer_keys": peer_keys[l].reshape(2 * PEER_HEADS, N_KEYS, PEER_HALF).astype(CDT),
    }


def kernel(x, positions, norm1_g, w_in, q_lora_g, kv_lora_g, w_uq, w_ukv, q_head_g, k_head_g, w_o_mla,
           pool_w, pool_scale, w_pool_out, gate_bias, w_out, norm2_g, peer_wq, peer_keys, peer_u, peer_v):
    B, S, D = x.shape
    assert D == D_MODEL and S % LANES == 0
    T = B * S
    depth = norm1_g.shape[0]
    ropes = _rope_tables(positions)
    x2d = x.reshape(T, D)
    u_all = peer_u.astype(CDT)
    v_all = peer_v.astype(CDT)
    for l in range(depth):
        lw = _layer_weights(l, norm1_g, w_in, q_lora_g, kv_lora_g, w_uq, w_ukv, q_head_g, k_head_g,
                            w_o_mla, pool_w, pool_scale, w_pool_out, gate_bias, w_out, norm2_g,
                            peer_wq, peer_keys)
        q, k, v, p, gates = _in_proj(x2d, lw, ropes, tm=IN_PROJ_TM)
        o = _attention(q, k, v, B, S, tq=ATTN_TQ)
        y = _pool(p, lw, B, S)
        x2d, xnt, st = _out_proj(x2d, o, y, gates, lw, tm=OUT_PROJ_TM)
        r2, e2, cnt, c = _peer_topk(st, tmk=ROUTE_TM)
        x2d = _peer_dense(x2d, xnt, r2, e2, cnt, c, u_all, v_all, l, tm=PEER_TM, eb=PEER_EB, tc=PEER_TC)
    return x2d.reshape(B, S, D)
```

```python
import functools
import math

import jax
import jax.numpy as jnp
import numpy as np
from jax import lax
from jax.experimental import pallas as pl
from jax.experimental.pallas import tpu as pltpu

D_MODEL = 1024
MLA_HEADS = 8
Q_LORA = 384
KV_LORA = 256
QK_NOPE = 64
QK_ROPE = 32
QK_HEAD = QK_NOPE + QK_ROPE
V_HEAD = 64
MLA_WIDTH = MLA_HEADS * V_HEAD
ROPE_BASE = 10000.0
POOL_WINDOWS = (2, 4, 8, 16)
POOL_WIDTH = 512
POOL_GC = 128
PEER_HEADS = 8
N_KEYS = 128
N_EXPERTS = N_KEYS * N_KEYS
PEER_HALF = 128
PEER_TOPK = 16
RMS_EPS = 1e-6

LANES = 128
GATE_TILE_ROWS = 16
ROW_SUBTILE = 256
ROUTE_CHUNK = 256
HEAD_PAD = LANES
IN_PAD = Q_LORA + KV_LORA + LANES + POOL_WIDTH + 2 * D_MODEL
VMEM_LIMIT = 48 * 1024 * 1024
PEER_DENSE_VMEM_LIMIT = 56 * 1024 * 1024

IN_PROJ_TM = 1024
ATTN_TQ = 256
OUT_PROJ_TM = 512
ROUTE_TM = 512
PEER_TM = 512
PEER_EB = 2048
PEER_CHAIN = 512
PEER_TC = 256

CDT = jnp.bfloat16
GDT = jnp.bfloat16
F32 = jnp.float32
NEG_INF = float("-inf")
_FLT_MAX = float(np.finfo(np.float32).max)
_FLT_TOP_ULP = 2.0 ** 104

_CAND_ROWS = 16 + 8 * 7 + 8
_FIDX_INVALID = 1 << 20


def _cparams(sem, vmem=VMEM_LIMIT):
    return pltpu.CompilerParams(dimension_semantics=sem, vmem_limit_bytes=vmem)


def _rms(x, g):
    return x * lax.rsqrt(jnp.mean(x * x, axis=-1, keepdims=True) + RMS_EPS) * g


def _dot(a, b):
    return jnp.dot(a, b, preferred_element_type=F32)


def _dot_nt(a, b):
    return lax.dot_general(a, b, (((1,), (1,)), ((), ())), preferred_element_type=F32)


def _rope_kernel(pos_ref, inv_ref, c_ref, s1_ref, s2_ref):
    ang = pos_ref[...] * inv_ref[...]
    lane = lax.broadcasted_iota(jnp.int32, ang.shape, 1)
    cos = jnp.cos(ang)
    sin = jnp.sin(ang)
    half = QK_ROPE // 2
    c_ref[...] = jnp.where(lane < QK_NOPE, 1.0, jnp.where(lane < QK_HEAD, cos, 0.0))
    s1_ref[...] = jnp.where((lane >= QK_NOPE) & (lane < QK_NOPE + half), -sin, 0.0)
    s2_ref[...] = jnp.where((lane >= QK_NOPE + half) & (lane < QK_HEAD), sin, 0.0)


def _rope_tables(positions):
    T = positions.size
    tm = min(T, 2048)
    pos = jnp.broadcast_to(positions.reshape(T, 1).astype(F32), (T, LANES))
    inv = ROPE_BASE ** (-jnp.arange(0, QK_ROPE, 2, dtype=F32) / QK_ROPE)
    inv_pat = jnp.concatenate([jnp.zeros((QK_NOPE,), F32), inv, inv,
                               jnp.zeros((LANES - QK_HEAD,), F32)]).reshape(1, LANES)
    spec = pl.BlockSpec((tm, LANES), lambda i: (i, 0))
    return pl.pallas_call(
        _rope_kernel,
        grid=(T // tm,),
        in_specs=[spec, pl.BlockSpec((1, LANES), lambda i: (0, 0))],
        out_specs=[spec, spec, spec],
        out_shape=[jax.ShapeDtypeStruct((T, LANES), F32)] * 3,
        compiler_params=_cparams(("parallel",)),
        name="rope_tables",
    )(pos, inv_pat)


def _in_proj_kernel(x_ref, g1_ref, win_ref, qlg_ref, kvlg_ref, wuq_ref, wuk_ref, wuv_ref,
                    c_ref, s1_ref, s2_ref, qg_ref, kg_ref, bias_ref,
                    q_out, k_out, v_out, p_out, gate_out, *, sub):
    half = QK_ROPE // 2
    qg = qg_ref[...]
    kg = kg_ref[...]

    def head_norm(xh, g):
        ms = jnp.sum(xh * xh, axis=-1, keepdims=True) * (1.0 / QK_HEAD)
        return xh * lax.rsqrt(ms + RMS_EPS) * g

    for r0 in range(0, x_ref.shape[0], sub):
        rows = slice(r0, r0 + sub)
        xn = _rms(x_ref[rows, :], g1_ref[...])
        y = _dot(xn.astype(CDT), win_ref[...])
        o = 0
        cq = y[:, o:o + Q_LORA]; o += Q_LORA
        ckv = y[:, o:o + KV_LORA]; o += KV_LORA
        kpe = y[:, o:o + LANES]; o += LANES
        p_out[rows, :] = y[:, o:o + POOL_WIDTH]; o += POOL_WIDTH
        gate_out[rows, :] = jax.nn.sigmoid(y[:, o:] + bias_ref[...]).astype(gate_out.dtype)

        cqn = _rms(cq, qlg_ref[...]).astype(CDT)
        ckvn = _rms(ckv, kvlg_ref[...]).astype(CDT)
        q_raw = _dot(cqn, wuq_ref[...])
        k_raw = _dot(ckvn, wuk_ref[...])
        v_out[rows, :] = _dot(ckvn, wuv_ref[...]).astype(v_out.dtype)

        cpat = c_ref[rows, :]
        s1pat = s1_ref[rows, :]
        s2pat = s2_ref[rows, :]

        def rope(xh, cpat=cpat, s1pat=s1pat, s2pat=s2pat):
            return (xh * cpat + pltpu.roll(xh, LANES - half, 1) * s1pat
                    + pltpu.roll(xh, half, 1) * s2pat)

        kpe_r = rope(kpe)
        for h in range(MLA_HEADS):
            sl = slice(h * HEAD_PAD, (h + 1) * HEAD_PAD)
            q_out[rows, sl] = head_norm(rope(q_raw[:, sl]), qg).astype(q_out.dtype)
            k_out[rows, sl] = head_norm(k_raw[:, sl] + kpe_r, kg).astype(k_out.dtype)


def _in_proj(x2d, lw, ropes, tm):
    T = x2d.shape[0]
    tm = min(tm, T)
    row = lambda w: pl.BlockSpec((tm, w), lambda i: (i, 0))
    full = lambda a: pl.BlockSpec(a.shape, lambda i: (0,) * a.ndim)
    consts = [lw["g1"], lw["w_in"], lw["qlg"], lw["kvlg"], lw["w_uq"], lw["w_uk"], lw["w_uv"]]
    tail = [lw["qg"], lw["kg"], lw["gate_bias"]]
    return pl.pallas_call(
        functools.partial(_in_proj_kernel, sub=min(tm, ROW_SUBTILE)),
        grid=(T // tm,),
        in_specs=[row(D_MODEL)] + [full(a) for a in consts] + [row(LANES)] * 3 + [full(a) for a in tail],
        out_specs=[row(MLA_HEADS * HEAD_PAD), row(MLA_HEADS * HEAD_PAD), row(MLA_WIDTH),
                   row(POOL_WIDTH), row(2 * D_MODEL)],
        out_shape=[jax.ShapeDtypeStruct((T, MLA_HEADS * HEAD_PAD), CDT),
                   jax.ShapeDtypeStruct((T, MLA_HEADS * HEAD_PAD), CDT),
                   jax.ShapeDtypeStruct((T, MLA_WIDTH), CDT),
                   jax.ShapeDtypeStruct((T, POOL_WIDTH), F32),
                   jax.ShapeDtypeStruct((T, 2 * D_MODEL), CDT)],
        compiler_params=_cparams(("parallel",)),
        name="in_proj",
    )(x2d, *consts, *ropes, *tail)


def _attn_kernel(q_ref, k_ref, v_ref, o_ref):
    outs = []
    for j in range(MLA_HEADS):
        q = q_ref[:, j * HEAD_PAD:(j + 1) * HEAD_PAD]
        k = k_ref[:, j * HEAD_PAD:(j + 1) * HEAD_PAD]
        v = v_ref[:, j * V_HEAD:(j + 1) * V_HEAD]
        s = _dot_nt(q, k)
        m = jnp.max(s, axis=-1, keepdims=True)
        p = jnp.exp(s - m)
        l = jnp.sum(p, axis=-1, keepdims=True)
        outs.append(_dot(p.astype(CDT), v) / l)
    o_ref[...] = jnp.concatenate(outs, axis=-1).astype(o_ref.dtype)


def _attention(q, k, v, B, S, tq):
    tq = min(tq, S)
    nq = S // tq
    return pl.pallas_call(
        _attn_kernel,
        grid=(B, nq),
        in_specs=[pl.BlockSpec((tq, MLA_HEADS * HEAD_PAD), lambda b, i: (b * nq + i, 0)),
                  pl.BlockSpec((S, MLA_HEADS * HEAD_PAD), lambda b, i: (b, 0)),
                  pl.BlockSpec((S, MLA_WIDTH), lambda b, i: (b, 0))],
        out_specs=pl.BlockSpec((tq, MLA_WIDTH), lambda b, i: (b * nq + i, 0)),
        out_shape=jax.ShapeDtypeStruct((B * S, MLA_WIDTH), CDT),
        compiler_params=_cparams(("parallel", "parallel")),
        name="attention",
    )(q, k, v)


def _pool_kernel(p_ref, pw_ref, ps_ref, y_ref):
    S = p_ref.shape[0]
    t = lax.broadcasted_iota(jnp.int32, (S, POOL_GC), 0)
    for g, w in enumerate(POOL_WINDOWS):
        half = w // 2
        sl = slice(g * POOL_GC, (g + 1) * POOL_GC)
        pg = p_ref[:, sl]
        acc = pg
        for d in range(-half, half):
            if d == 0:
                continue
            shifted = pltpu.roll(pg, (-d) % S, 0)
            valid = (t + d >= 0) & (t + d < S)
            acc = acc + jnp.where(valid, shifted, 0.0)
        count = (jnp.minimum(t + half, S) - jnp.maximum(t - half, 0)).astype(F32)
        mixed = acc / count - pg
        yg = _dot(mixed.astype(CDT), pw_ref[g]) * ps_ref[:, sl]
        y_ref[:, sl] = yg.astype(y_ref.dtype)


def _pool(p, lw, B, S):
    return pl.pallas_call(
        _pool_kernel,
        grid=(B,),
        in_specs=[pl.BlockSpec((S, POOL_WIDTH), lambda b: (b, 0)),
                  pl.BlockSpec(lw["pool_w"].shape, lambda b: (0, 0, 0)),
                  pl.BlockSpec((1, POOL_WIDTH), lambda b: (0, 0))],
        out_specs=pl.BlockSpec((S, POOL_WIDTH), lambda b: (b, 0)),
        out_shape=jax.ShapeDtypeStruct((B * S, POOL_WIDTH), CDT),
        compiler_params=_cparams(("parallel",)),
        name="pool",
    )(p, lw["pool_w"], lw["pool_scale"])


def _out_proj_kernel(x_ref, o_ref, y_ref, gate_ref, wo_ref, wpo_ref, wout_ref, g2_ref, wq_ref, keys_ref,
                     xnew_out, xnt_out, st_out, *, sub):
    for r0 in range(0, x_ref.shape[0], sub):
        rows = slice(r0, r0 + sub)
        a = _dot(o_ref[rows, :], wo_ref[...])
        bp = _dot(y_ref[rows, :], wpo_ref[...])
        ga = gate_ref[rows, :D_MODEL].astype(F32)
        gb = gate_ref[rows, D_MODEL:].astype(F32)
        mix = ga * a + gb * bp
        xnew = x_ref[rows, :] + _dot(mix.astype(CDT), wout_ref[...])
        xnew_out[rows, :] = xnew
        xn = _rms(xnew, g2_ref[...])
        xnt_out[:, rows] = xn.T.astype(xnt_out.dtype)
        qp = _dot(xn.astype(CDT), wq_ref[...])
        for hp in range(2 * PEER_HEADS):
            qh = qp[:, hp * PEER_HALF:(hp + 1) * PEER_HALF].astype(CDT)
            st_out[hp, :, rows] = _dot_nt(keys_ref[hp], qh)


def _out_proj(x2d, o, y, gates, lw, tm):
    T = x2d.shape[0]
    tm = min(tm, T)
    row = lambda w: pl.BlockSpec((tm, w), lambda i: (i, 0))
    full = lambda a: pl.BlockSpec(a.shape, lambda i: (0,) * a.ndim)
    consts = [lw["w_o_mla"], lw["w_pool_out"], lw["w_out"], lw["g2"], lw["peer_wq"], lw["peer_keys"]]
    return pl.pallas_call(
        functools.partial(_out_proj_kernel, sub=min(tm, ROW_SUBTILE)),
        grid=(T // tm,),
        in_specs=[row(D_MODEL), row(MLA_WIDTH), row(POOL_WIDTH), row(2 * D_MODEL)] + [full(a) for a in consts],
        out_specs=[row(D_MODEL),
                   pl.BlockSpec((D_MODEL, tm), lambda i: (0, i)),
                   pl.BlockSpec((2 * PEER_HEADS, N_KEYS, tm), lambda i: (0, 0, i))],
        out_shape=[jax.ShapeDtypeStruct((T, D_MODEL), F32),
                   jax.ShapeDtypeStruct((D_MODEL, T), CDT),
                   jax.ShapeDtypeStruct((2 * PEER_HEADS, N_KEYS, T), F32)],
        compiler_params=_cparams(("parallel",)),
        name="out_proj",
    )(x2d, o, y, gates, *consts)


def _top16(s, key_iota, row16, exact):
    work = s
    rank = jnp.full(s.shape, float(PEER_TOPK), F32)
    tops = jnp.zeros((PEER_TOPK, s.shape[1]), F32)
    for r in range(PEER_TOPK):
        m = jnp.max(work, axis=0, keepdims=True)
        sel = work == m
        if exact:
            first = jnp.min(jnp.where(sel, key_iota, float(N_KEYS)), axis=0, keepdims=True)
            sel = key_iota == first
            rank = jnp.where(sel, float(r), rank)
            work = jnp.where(sel, NEG_INF, work)
        else:
            work = jnp.where(sel, -_FLT_MAX + r * _FLT_TOP_ULP, work)
        tops = jnp.where(row16 == r, m, tops)
    if not exact:
        rank = jnp.minimum((work + _FLT_MAX) * (1.0 / _FLT_TOP_ULP), float(PEER_TOPK))
    return rank, tops


def _cand_grid(ta, tb, combine):
    pieces = [combine(ta[0:1], tb)]
    for r1 in range(1, 8):
        pieces.append(combine(ta[r1:r1 + 1], tb[0:8]))
    pieces.append(combine(ta[8:16], tb[0:1]))
    return jnp.concatenate(pieces, axis=0)


def _count_true(mask):
    return jnp.sum(jnp.where(mask, 1.0, 0.0), axis=0, keepdims=True)


def _route(s1, s2, key_iota, row16, fidx, exact):
    rank1, ta = _top16(s1, key_iota, row16, exact)
    rank2, tb = _top16(s2, key_iota, row16, exact)
    a0 = ta[0:1]
    b0 = tb[0:1]
    cand0 = jnp.where(fidx < _FIDX_INVALID, _cand_grid(ta, tb, lambda a, b: a + b), NEG_INF)
    ecand = _cand_grid(jnp.exp(ta - a0), jnp.exp(tb - b0), lambda a, b: a * b)
    cand = cand0
    if exact:
        selected = jnp.zeros(cand.shape, F32)
        for _ in range(PEER_TOPK):
            m = jnp.max(cand, axis=0, keepdims=True)
            first = jnp.min(jnp.where(cand == m, fidx, float(_FIDX_INVALID)), axis=0, keepdims=True)
            sel = fidx == first
            selected = jnp.where(sel, 1.0, selected)
            cand = jnp.where(sel, NEG_INF, cand)
        bad = jnp.zeros_like(a0)
    else:
        for _ in range(PEER_TOPK):
            m = jnp.max(cand, axis=0, keepdims=True)
            cand = jnp.where(cand == m, NEG_INF, cand)
        selected = jnp.where(cand0 >= m, 1.0, 0.0)
        k = float(PEER_TOPK)
        bad = jnp.where((_count_true(rank1 < k) != k) | (_count_true(rank2 < k) != k)
                        | (jnp.sum(selected, axis=0, keepdims=True) != k), 1.0, 0.0)
    z = jnp.sum(selected * ecand, axis=0, keepdims=True)
    cntr = [jnp.sum(selected[0:16], axis=0, keepdims=True)]
    for r1 in range(1, 8):
        cntr.append(jnp.sum(selected[8 + 8 * r1:16 + 8 * r1], axis=0, keepdims=True))
    for r1 in range(8, 16):
        cntr.append(selected[64 + r1:65 + r1])
    cntr_rows = jnp.concatenate(cntr[0:8] + [selected[72:80]], axis=0)
    cnt = jnp.zeros(s1.shape, F32)
    for v in range(1, 5):
        ranks_with_v = jnp.sum(jnp.where(cntr_rows >= float(v), 1.0, 0.0), axis=0, keepdims=True)
        cnt = jnp.where(rank1 < ranks_with_v, float(v), cnt)
    for r1 in range(3):
        cnt = jnp.where(rank1 == float(r1), cntr[r1], cnt)
    return rank2, jnp.exp(s2 - b0), cnt, jnp.exp(s1 - a0) / z, bad


def _peer_topk_kernel(s_ref, fidx_ref, r2_out, e2_out, cnt_out, c_out, *, n_chunks):
    key_iota = lax.broadcasted_iota(jnp.int32, (N_KEYS, ROUTE_CHUNK), 0).astype(F32)
    row16 = lax.broadcasted_iota(jnp.int32, (PEER_TOPK, ROUTE_CHUNK), 0)
    fidx = fidx_ref[...]

    def body(it, carry):
        h = it // n_chunks
        cols = pl.ds(pl.multiple_of((it % n_chunks) * ROUTE_CHUNK, ROUTE_CHUNK), ROUTE_CHUNK)
        s1 = s_ref[2 * h, :, cols]
        s2 = s_ref[2 * h + 1, :, cols]

        def store(r2, e2, cnt, c):
            r2_out[h, :, cols] = r2.astype(r2_out.dtype)
            e2_out[h, :, cols] = e2.astype(e2_out.dtype)
            cnt_out[h, :, cols] = cnt.astype(cnt_out.dtype)
            c_out[h, :, cols] = c.astype(c_out.dtype)

        *fast, bad = _route(s1, s2, key_iota, row16, fidx, exact=False)
        store(*fast)

        @pl.when(jnp.max(bad) > 0.0)
        def _():
            *slow, _ = _route(s1, s2, key_iota, row16, fidx, exact=True)
            store(*slow)

        return carry

    lax.fori_loop(0, PEER_HEADS * n_chunks, body, 0)


def _cand_fidx():
    rows = []
    rows += [0 * 16 + r2 for r2 in range(16)]
    for r1 in range(1, 8):
        n = PEER_TOPK // (r1 + 1)
        rows += [r1 * 16 + r2 if r2 < n else _FIDX_INVALID + r1 * 16 + r2 for r2 in range(8)]
    rows += [r1 * 16 for r1 in range(8, 16)]
    assert len(rows) == _CAND_ROWS
    return jnp.broadcast_to(jnp.asarray(rows, F32)[:, None], (_CAND_ROWS, ROUTE_CHUNK))


def _peer_topk(st, tmk):
    T = st.shape[-1]
    tmk = min(tmk, T)
    spec = pl.BlockSpec((PEER_HEADS, N_KEYS, tmk), lambda i: (0, 0, i))
    shp = jax.ShapeDtypeStruct((PEER_HEADS, N_KEYS, T), GDT)
    shp32 = jax.ShapeDtypeStruct((PEER_HEADS, N_KEYS, T), F32)
    return pl.pallas_call(
        functools.partial(_peer_topk_kernel, n_chunks=tmk // ROUTE_CHUNK),
        grid=(T // tmk,),
        in_specs=[pl.BlockSpec((2 * PEER_HEADS, N_KEYS, tmk), lambda i: (0, 0, i)),
                  pl.BlockSpec((_CAND_ROWS, ROUTE_CHUNK), lambda i: (0, 0))],
        out_specs=[spec] * 4,
        out_shape=[shp, shp, shp32, shp32],
        compiler_params=_cparams(("parallel",)),
        name="peer_topk",
    )(st, _cand_fidx())


def _peer_dense_kernel(x_ref, xnt_ref, r2_ref, e2_ref, cnt_ref, c_ref, u_ref, vt_ref, out_ref,
                       acc_ref, ht_ref, at_ref, bc_ref, *, eb, tc, nsub):
    e = pl.program_id(1)
    tm = xnt_ref.shape[1]
    jpb = eb // N_KEYS
    sub = eb // nsub
    pk = bc_ref.shape[-2]

    @pl.when(e == 0)
    def _():
        acc_ref[...] = jnp.zeros_like(acc_ref)

    def fill(h, carry):
        for jb in range(jpb):
            j = e * jpb + jb
            bc_ref[0, h, jb] = jnp.broadcast_to(cnt_ref[h, pl.ds(j, 1), :], (pk, tm)).astype(GDT)
            bc_ref[1, h, jb] = jnp.broadcast_to(c_ref[h, pl.ds(j, 1), :], (pk, tm)).astype(GDT)
        return carry

    lax.fori_loop(0, PEER_HEADS, fill, 0)

    for sb in range(nsub):
        rows = slice(sb * sub, (sb + 1) * sub)
        ht_ref[rows, :] = _dot(u_ref[rows, :], xnt_ref[...])
    for sb in range(nsub):
        rows = slice(sb * sub, (sb + 1) * sub)
        for jb in range(sb * sub // N_KEYS, (sb + 1) * sub // N_KEYS):
            krows = slice(jb * N_KEYS, (jb + 1) * N_KEYS)
            for ci in range(tm // tc):
                cols = slice(ci * tc, (ci + 1) * tc)
                g = jnp.zeros((N_KEYS // pk, pk, tc), GDT)
                for h in range(PEER_HEADS):
                    r2 = r2_ref[h, :, cols].reshape(N_KEYS // pk, pk, tc)
                    e2 = e2_ref[h, :, cols].reshape(N_KEYS // pk, pk, tc)
                    g = g + jnp.where(r2 < bc_ref[0, h, jb, :, cols], e2 * bc_ref[1, h, jb, :, cols], 0)
                a = g.reshape(N_KEYS, tc) * jax.nn.gelu(ht_ref[krows, cols].astype(GDT))
                at_ref[krows, cols] = a.astype(at_ref.dtype)
        acc_ref[...] += _dot(vt_ref[:, rows], at_ref[rows, :])

    @pl.when(e == pl.num_programs(1) - 1)
    def _():
        out_ref[...] = x_ref[...] + acc_ref[...].T


def _peer_dense(x2d, xnt, r2, e2, cnt, c, u_all, vt_all, layer, tm, eb, tc):
    T = x2d.shape[0]
    tm = min(tm, T)
    tc = min(tc, tm)
    tok3 = pl.BlockSpec((PEER_HEADS, N_KEYS, tm), lambda i, e: (0, 0, i))
    return pl.pallas_call(
        functools.partial(_peer_dense_kernel, eb=eb, tc=tc, nsub=eb // PEER_CHAIN),
        grid=(T // tm, N_EXPERTS // eb),
        in_specs=[pl.BlockSpec((tm, D_MODEL), lambda i, e: (i, 0)),
                  pl.BlockSpec((D_MODEL, tm), lambda i, e: (0, i)),
                  tok3, tok3, tok3, tok3,
                  pl.BlockSpec((None, eb, D_MODEL), lambda i, e: (layer, e, 0)),
                  pl.BlockSpec((None, D_MODEL, eb), lambda i, e: (layer, 0, e))],
        out_specs=pl.BlockSpec((tm, D_MODEL), lambda i, e: (i, 0)),
        out_shape=jax.ShapeDtypeStruct((T, D_MODEL), F32),
        scratch_shapes=[pltpu.VMEM((D_MODEL, tm), F32),
                        pltpu.VMEM((eb, tm), F32),
                        pltpu.VMEM((eb, tm), CDT),
                        pltpu.VMEM((2, PEER_HEADS, eb // N_KEYS, GATE_TILE_ROWS, tm), GDT)],
        compiler_params=_cparams(("parallel", "arbitrary"), vmem=PEER_DENSE_VMEM_LIMIT),
        name="peer_dense",
    )(x2d, xnt, r2, e2, cnt, c, u_all, vt_all)


def _pad_heads(w, width):
    r = w.shape[0]
    w = w.reshape(r, MLA_HEADS, width)
    w = jnp.pad(w, ((0, 0), (0, 0), (0, HEAD_PAD - width)))
    return w.reshape(r, MLA_HEADS * HEAD_PAD)


def _layer_weights(l, norm1_g, w_in, q_lora_g, kv_lora_g, w_uq, w_ukv, q_head_g, k_head_g, w_o_mla,
                   pool_w, pool_scale, w_pool_out, gate_bias, w_out, norm2_g, peer_wq, peer_keys):
    wi = w_in[l]
    c0 = Q_LORA
    c1 = c0 + KV_LORA
    c2 = c1 + QK_ROPE
    c3 = c2 + POOL_WIDTH
    zeros = lambda n: jnp.zeros((D_MODEL, n), wi.dtype)
    w_in_pad = jnp.concatenate([wi[:, :c1], zeros(QK_NOPE), wi[:, c1:c2], zeros(LANES - QK_HEAD),
                                wi[:, c2:]], axis=1)
    wkv = w_ukv[l].reshape(KV_LORA, MLA_HEADS, QK_NOPE + V_HEAD)
    w_uk = _pad_heads(wkv[:, :, :QK_NOPE].reshape(KV_LORA, MLA_HEADS * QK_NOPE), QK_NOPE)
    w_uv = wkv[:, :, QK_NOPE:].reshape(KV_LORA, MLA_WIDTH)
    pad_g = lambda g: jnp.pad(g, (0, LANES - QK_HEAD)).reshape(1, LANES)
    return {
        "g1": norm1_g[l].reshape(1, D_MODEL),
        "w_in": w_in_pad.astype(CDT),
        "qlg": q_lora_g[l].reshape(1, Q_LORA),
        "kvlg": kv_lora_g[l].reshape(1, KV_LORA),
        "w_uq": _pad_heads(w_uq[l], QK_HEAD).astype(CDT),
        "w_uk": w_uk.astype(CDT),
        "w_uv": w_uv.astype(CDT),
        "qg": pad_g(q_head_g[l] * (1.0 / math.sqrt(QK_HEAD))),
        "kg": pad_g(k_head_g[l]),
        "gate_bias": gate_bias[l].reshape(1, 2 * D_MODEL),
        "w_o_mla": w_o_mla[l].astype(CDT),
        "pool_w": pool_w[l].astype(CDT),
        "pool_scale": pool_scale[l].reshape(1, POOL_WIDTH),
        "w_pool_out": w_pool_out[l].astype(CDT),
        "w_out": w_out[l].astype(CDT),
        "g2": norm2_g[l].reshape(1, D_MODEL),
        "peer_wq": peer_wq[l].astype(CDT),
        "peer_keys": peer_keys[l].reshape(2 * PEER_HEADS, N_KEYS, PEER_HALF).astype(CDT),
    }


def kernel(x, positions, norm1_g, w_in, q_lora_g, kv_lora_g, w_uq, w_ukv, q_head_g, k_head_g, w_o_mla,
           pool_w, pool_scale, w_pool_out, gate_bias, w_out, norm2_g, peer_wq, peer_keys, peer_u, peer_v):
    B, S, D = x.shape
    assert D == D_MODEL and S % LANES == 0
    T = B * S
    depth = norm1_g.shape[0]
    ropes = _rope_tables(positions)
    x2d = x.reshape(T, D)
    u_all = peer_u.astype(CDT)
    vt_all = jnp.swapaxes(peer_v, 1, 2).astype(CDT)
    for l in range(depth):
        lw = _layer_weights(l, norm1_g, w_in, q_lora_g, kv_lora_g, w_uq, w_ukv, q_head_g, k_head_g,
                            w_o_mla, pool_w, pool_scale, w_pool_out, gate_bias, w_out, norm2_g,
                            peer_wq, peer_keys)
        q, k, v, p, gates = _in_proj(x2d, lw, ropes, tm=IN_PROJ_TM)
        o = _attention(q, k, v, B, S, tq=ATTN_TQ)
        y = _pool(p, lw, B, S)
        x2d, xnt, st = _out_proj(x2d, o, y, gates, lw, tm=OUT_PROJ_TM)
        r2, e2, cnt, c = _peer_topk(st, tmk=ROUTE_TM)
        x2d = _peer_dense(x2d, xnt, r2, e2, cnt, c, u_all, vt_all, l, tm=PEER_TM, eb=PEER_EB, tc=PEER_TC)
    return x2d.reshape(B, S, D)
```

```python
import functools
import math

import jax
import jax.numpy as jnp
import numpy as np
from jax import lax
from jax.experimental import pallas as pl
from jax.experimental.pallas import tpu as pltpu

D_MODEL = 1024
MLA_HEADS = 8
Q_LORA = 384
KV_LORA = 256
QK_NOPE = 64
QK_ROPE = 32
QK_HEAD = QK_NOPE + QK_ROPE
V_HEAD = 64
MLA_WIDTH = MLA_HEADS * V_HEAD
ROPE_BASE = 10000.0
POOL_WINDOWS = (2, 4, 8, 16)
POOL_WIDTH = 512
POOL_GC = 128
PEER_HEADS = 8
N_KEYS = 128
N_EXPERTS = N_KEYS * N_KEYS
PEER_HALF = 128
PEER_TOPK = 16
RMS_EPS = 1e-6

LANES = 128
GATE_TILE_ROWS = 16
ROW_SUBTILE = 256
ROUTE_CHUNK = 512
HEAD_PAD = LANES
IN_PAD = Q_LORA + KV_LORA + LANES + POOL_WIDTH + 2 * D_MODEL
VMEM_LIMIT = 48 * 1024 * 1024
PEER_DENSE_VMEM_LIMIT = 56 * 1024 * 1024

IN_PROJ_TM = 1024
ATTN_TQ = 256
OUT_PROJ_TM = 512
ROUTE_TM = 1024
PEER_TM = 512
PEER_EB = 2048
PEER_CHAIN = 512
PEER_TC = 256

CDT = jnp.bfloat16
GDT = jnp.bfloat16
F32 = jnp.float32
NEG_INF = float("-inf")
_FLT_MAX = float(np.finfo(np.float32).max)
_FLT_TOP_ULP = 2.0 ** 104

_CAND_ROWS = 16 + 8 * 7 + 8
_FIDX_INVALID = 1 << 20


def _cparams(sem, vmem=VMEM_LIMIT):
    return pltpu.CompilerParams(dimension_semantics=sem, vmem_limit_bytes=vmem)


def _rms(x, g):
    return x * lax.rsqrt(jnp.mean(x * x, axis=-1, keepdims=True) + RMS_EPS) * g


def _dot(a, b):
    return jnp.dot(a, b, preferred_element_type=F32)


def _dot_nt(a, b):
    return lax.dot_general(a, b, (((1,), (1,)), ((), ())), preferred_element_type=F32)


def _rope_kernel(pos_ref, inv_ref, c_ref, s1_ref, s2_ref):
    ang = pos_ref[...] * inv_ref[...]
    lane = lax.broadcasted_iota(jnp.int32, ang.shape, 1)
    cos = jnp.cos(ang)
    sin = jnp.sin(ang)
    half = QK_ROPE // 2
    c_ref[...] = jnp.where(lane < QK_NOPE, 1.0, jnp.where(lane < QK_HEAD, cos, 0.0))
    s1_ref[...] = jnp.where((lane >= QK_NOPE) & (lane < QK_NOPE + half), -sin, 0.0)
    s2_ref[...] = jnp.where((lane >= QK_NOPE + half) & (lane < QK_HEAD), sin, 0.0)


def _rope_tables(positions):
    T = positions.size
    tm = min(T, 2048)
    pos = jnp.broadcast_to(positions.reshape(T, 1).astype(F32), (T, LANES))
    inv = ROPE_BASE ** (-jnp.arange(0, QK_ROPE, 2, dtype=F32) / QK_ROPE)
    inv_pat = jnp.concatenate([jnp.zeros((QK_NOPE,), F32), inv, inv,
                               jnp.zeros((LANES - QK_HEAD,), F32)]).reshape(1, LANES)
    spec = pl.BlockSpec((tm, LANES), lambda i: (i, 0))
    return pl.pallas_call(
        _rope_kernel,
        grid=(T // tm,),
        in_specs=[spec, pl.BlockSpec((1, LANES), lambda i: (0, 0))],
        out_specs=[spec, spec, spec],
        out_shape=[jax.ShapeDtypeStruct((T, LANES), F32)] * 3,
        compiler_params=_cparams(("parallel",)),
        name="rope_tables",
    )(pos, inv_pat)


def _in_proj_kernel(x_ref, g1_ref, win_ref, qlg_ref, kvlg_ref, wuq_ref, wuk_ref, wuv_ref,
                    c_ref, s1_ref, s2_ref, qg_ref, kg_ref, bias_ref,
                    q_out, k_out, v_out, p_out, gate_out, *, sub):
    half = QK_ROPE // 2
    qg = qg_ref[...]
    kg = kg_ref[...]

    def head_norm(xh, g):
        ms = jnp.sum(xh * xh, axis=-1, keepdims=True) * (1.0 / QK_HEAD)
        return xh * lax.rsqrt(ms + RMS_EPS) * g

    for r0 in range(0, x_ref.shape[0], sub):
        rows = slice(r0, r0 + sub)
        xn = _rms(x_ref[rows, :], g1_ref[...])
        y = _dot(xn.astype(CDT), win_ref[...])
        o = 0
        cq = y[:, o:o + Q_LORA]; o += Q_LORA
        ckv = y[:, o:o + KV_LORA]; o += KV_LORA
        kpe = y[:, o:o + LANES]; o += LANES
        p_out[rows, :] = y[:, o:o + POOL_WIDTH]; o += POOL_WIDTH
        gate_out[rows, :] = jax.nn.sigmoid(y[:, o:] + bias_ref[...]).astype(gate_out.dtype)

        cqn = _rms(cq, qlg_ref[...]).astype(CDT)
        ckvn = _rms(ckv, kvlg_ref[...]).astype(CDT)
        q_raw = _dot(cqn, wuq_ref[...])
        k_raw = _dot(ckvn, wuk_ref[...])
        v_out[rows, :] = _dot(ckvn, wuv_ref[...]).astype(v_out.dtype)

        cpat = c_ref[rows, :]
        s1pat = s1_ref[rows, :]
        s2pat = s2_ref[rows, :]

        def rope(xh, cpat=cpat, s1pat=s1pat, s2pat=s2pat):
            return (xh * cpat + pltpu.roll(xh, LANES - half, 1) * s1pat
                    + pltpu.roll(xh, half, 1) * s2pat)

        kpe_r = rope(kpe)
        for h in range(MLA_HEADS):
            sl = slice(h * HEAD_PAD, (h + 1) * HEAD_PAD)
            q_out[rows, sl] = head_norm(rope(q_raw[:, sl]), qg).astype(q_out.dtype)
            k_out[rows, sl] = head_norm(k_raw[:, sl] + kpe_r, kg).astype(k_out.dtype)


def _in_proj(x2d, lw, ropes, tm):
    T = x2d.shape[0]
    tm = min(tm, T)
    row = lambda w: pl.BlockSpec((tm, w), lambda i: (i, 0))
    full = lambda a: pl.BlockSpec(a.shape, lambda i: (0,) * a.ndim)
    consts = [lw["g1"], lw["w_in"], lw["qlg"], lw["kvlg"], lw["w_uq"], lw["w_uk"], lw["w_uv"]]
    tail = [lw["qg"], lw["kg"], lw["gate_bias"]]
    return pl.pallas_call(
        functools.partial(_in_proj_kernel, sub=min(tm, ROW_SUBTILE)),
        grid=(T // tm,),
        in_specs=[row(D_MODEL)] + [full(a) for a in consts] + [row(LANES)] * 3 + [full(a) for a in tail],
        out_specs=[row(MLA_HEADS * HEAD_PAD), row(MLA_HEADS * HEAD_PAD), row(MLA_WIDTH),
                   row(POOL_WIDTH), row(2 * D_MODEL)],
        out_shape=[jax.ShapeDtypeStruct((T, MLA_HEADS * HEAD_PAD), CDT),
                   jax.ShapeDtypeStruct((T, MLA_HEADS * HEAD_PAD), CDT),
                   jax.ShapeDtypeStruct((T, MLA_WIDTH), CDT),
                   jax.ShapeDtypeStruct((T, POOL_WIDTH), F32),
                   jax.ShapeDtypeStruct((T, 2 * D_MODEL), CDT)],
        compiler_params=_cparams(("parallel",)),
        name="in_proj",
    )(x2d, *consts, *ropes, *tail)


def _attn_kernel(q_ref, k_ref, v_ref, o_ref):
    outs = []
    for j in range(MLA_HEADS):
        q = q_ref[:, j * HEAD_PAD:(j + 1) * HEAD_PAD]
        k = k_ref[:, j * HEAD_PAD:(j + 1) * HEAD_PAD]
        v = v_ref[:, j * V_HEAD:(j + 1) * V_HEAD]
        s = _dot_nt(q, k)
        m = jnp.max(s, axis=-1, keepdims=True)
        p = jnp.exp(s - m)
        l = jnp.sum(p, axis=-1, keepdims=True)
        outs.append(_dot(p.astype(CDT), v) / l)
    o_ref[...] = jnp.concatenate(outs, axis=-1).astype(o_ref.dtype)


def _attention(q, k, v, B, S, tq):
    tq = min(tq, S)
    nq = S // tq
    return pl.pallas_call(
        _attn_kernel,
        grid=(B, nq),
        in_specs=[pl.BlockSpec((tq, MLA_HEADS * HEAD_PAD), lambda b, i: (b * nq + i, 0)),
                  pl.BlockSpec((S, MLA_HEADS * HEAD_PAD), lambda b, i: (b, 0)),
                  pl.BlockSpec((S, MLA_WIDTH), lambda b, i: (b, 0))],
        out_specs=pl.BlockSpec((tq, MLA_WIDTH), lambda b, i: (b * nq + i, 0)),
        out_shape=jax.ShapeDtypeStruct((B * S, MLA_WIDTH), CDT),
        compiler_params=_cparams(("parallel", "parallel")),
        name="attention",
    )(q, k, v)


def _pool_kernel(p_ref, pw_ref, ps_ref, y_ref):
    S = p_ref.shape[0]
    t = lax.broadcasted_iota(jnp.int32, (S, POOL_GC), 0)
    for g, w in enumerate(POOL_WINDOWS):
        half = w // 2
        sl = slice(g * POOL_GC, (g + 1) * POOL_GC)
        pg = p_ref[:, sl]
        acc = pg
        for d in range(-half, half):
            if d == 0:
                continue
            shifted = pltpu.roll(pg, (-d) % S, 0)
            valid = (t + d >= 0) & (t + d < S)
            acc = acc + jnp.where(valid, shifted, 0.0)
        count = (jnp.minimum(t + half, S) - jnp.maximum(t - half, 0)).astype(F32)
        mixed = acc / count - pg
        yg = _dot(mixed.astype(CDT), pw_ref[g]) * ps_ref[:, sl]
        y_ref[:, sl] = yg.astype(y_ref.dtype)


def _pool(p, lw, B, S):
    return pl.pallas_call(
        _pool_kernel,
        grid=(B,),
        in_specs=[pl.BlockSpec((S, POOL_WIDTH), lambda b: (b, 0)),
                  pl.BlockSpec(lw["pool_w"].shape, lambda b: (0, 0, 0)),
                  pl.BlockSpec((1, POOL_WIDTH), lambda b: (0, 0))],
        out_specs=pl.BlockSpec((S, POOL_WIDTH), lambda b: (b, 0)),
        out_shape=jax.ShapeDtypeStruct((B * S, POOL_WIDTH), CDT),
        compiler_params=_cparams(("parallel",)),
        name="pool",
    )(p, lw["pool_w"], lw["pool_scale"])


def _out_proj_kernel(x_ref, o_ref, y_ref, gate_ref, wo_ref, wpo_ref, wout_ref, g2_ref, wq_ref, keys_ref,
                     xnew_out, xnt_out, st_out, *, sub):
    for r0 in range(0, x_ref.shape[0], sub):
        rows = slice(r0, r0 + sub)
        a = _dot(o_ref[rows, :], wo_ref[...])
        bp = _dot(y_ref[rows, :], wpo_ref[...])
        ga = gate_ref[rows, :D_MODEL].astype(F32)
        gb = gate_ref[rows, D_MODEL:].astype(F32)
        mix = ga * a + gb * bp
        xnew = x_ref[rows, :] + _dot(mix.astype(CDT), wout_ref[...])
        xnew_out[rows, :] = xnew
        xn = _rms(xnew, g2_ref[...])
        xnt_out[:, rows] = xn.T.astype(xnt_out.dtype)
        qp = _dot(xn.astype(CDT), wq_ref[...])
        for hp in range(2 * PEER_HEADS):
            qh = qp[:, hp * PEER_HALF:(hp + 1) * PEER_HALF].astype(CDT)
            st_out[hp, :, rows] = _dot_nt(keys_ref[hp], qh)


def _out_proj(x2d, o, y, gates, lw, tm):
    T = x2d.shape[0]
    tm = min(tm, T)
    row = lambda w: pl.BlockSpec((tm, w), lambda i: (i, 0))
    full = lambda a: pl.BlockSpec(a.shape, lambda i: (0,) * a.ndim)
    consts = [lw["w_o_mla"], lw["w_pool_out"], lw["w_out"], lw["g2"], lw["peer_wq"], lw["peer_keys"]]
    return pl.pallas_call(
        functools.partial(_out_proj_kernel, sub=min(tm, ROW_SUBTILE)),
        grid=(T // tm,),
        in_specs=[row(D_MODEL), row(MLA_WIDTH), row(POOL_WIDTH), row(2 * D_MODEL)] + [full(a) for a in consts],
        out_specs=[row(D_MODEL),
                   pl.BlockSpec((D_MODEL, tm), lambda i: (0, i)),
                   pl.BlockSpec((2 * PEER_HEADS, N_KEYS, tm), lambda i: (0, 0, i))],
        out_shape=[jax.ShapeDtypeStruct((T, D_MODEL), F32),
                   jax.ShapeDtypeStruct((D_MODEL, T), CDT),
                   jax.ShapeDtypeStruct((2 * PEER_HEADS, N_KEYS, T), F32)],
        compiler_params=_cparams(("parallel",)),
        name="out_proj",
    )(x2d, o, y, gates, *consts)


def _top16(s, key_iota, row16, exact):
    work = s
    rank = jnp.full(s.shape, float(PEER_TOPK), F32)
    tops = jnp.zeros((PEER_TOPK, s.shape[1]), F32)
    for r in range(PEER_TOPK):
        m = jnp.max(work, axis=0, keepdims=True)
        sel = work == m
        if exact:
            first = jnp.min(jnp.where(sel, key_iota, float(N_KEYS)), axis=0, keepdims=True)
            sel = key_iota == first
            rank = jnp.where(sel, float(r), rank)
            work = jnp.where(sel, NEG_INF, work)
        else:
            work = jnp.where(sel, -_FLT_MAX + r * _FLT_TOP_ULP, work)
        tops = jnp.where(row16 == r, m, tops)
    if not exact:
        rank = jnp.minimum((work + _FLT_MAX) * (1.0 / _FLT_TOP_ULP), float(PEER_TOPK))
    return rank, tops


def _cand_grid(ta, tb, combine):
    pieces = [combine(ta[0:1], tb)]
    for r1 in range(1, 8):
        pieces.append(combine(ta[r1:r1 + 1], tb[0:8]))
    pieces.append(combine(ta[8:16], tb[0:1]))
    return jnp.concatenate(pieces, axis=0)


def _count_true(mask):
    return jnp.sum(jnp.where(mask, 1.0, 0.0), axis=0, keepdims=True)


def _route(s1, s2, key_iota, row16, fidx, exact):
    rank1, ta = _top16(s1, key_iota, row16, exact)
    rank2, tb = _top16(s2, key_iota, row16, exact)
    a0 = ta[0:1]
    b0 = tb[0:1]
    cand0 = jnp.where(fidx < _FIDX_INVALID, _cand_grid(ta, tb, lambda a, b: a + b), NEG_INF)
    ecand = _cand_grid(jnp.exp(ta - a0), jnp.exp(tb - b0), lambda a, b: a * b)
    cand = cand0
    if exact:
        selected = jnp.zeros(cand.shape, F32)
        for _ in range(PEER_TOPK):
            m = jnp.max(cand, axis=0, keepdims=True)
            first = jnp.min(jnp.where(cand == m, fidx, float(_FIDX_INVALID)), axis=0, keepdims=True)
            sel = fidx == first
            selected = jnp.where(sel, 1.0, selected)
            cand = jnp.where(sel, NEG_INF, cand)
        bad = jnp.zeros_like(a0)
    else:
        for _ in range(PEER_TOPK):
            m = jnp.max(cand, axis=0, keepdims=True)
            cand = jnp.where(cand == m, NEG_INF, cand)
        selected = jnp.where(cand0 >= m, 1.0, 0.0)
        k = float(PEER_TOPK)
        bad = jnp.where((_count_true(rank1 < k) != k) | (_count_true(rank2 < k) != k)
                        | (jnp.sum(selected, axis=0, keepdims=True) != k), 1.0, 0.0)
    z = jnp.sum(selected * ecand, axis=0, keepdims=True)
    cntr = [jnp.sum(selected[0:16], axis=0, keepdims=True)]
    for r1 in range(1, 8):
        cntr.append(jnp.sum(selected[8 + 8 * r1:16 + 8 * r1], axis=0, keepdims=True))
    for r1 in range(8, 16):
        cntr.append(selected[64 + r1:65 + r1])
    cntr_rows = jnp.concatenate(cntr[0:8] + [selected[72:80]], axis=0)
    cnt = jnp.zeros(s1.shape, F32)
    for v in range(1, 5):
        ranks_with_v = jnp.sum(jnp.where(cntr_rows >= float(v), 1.0, 0.0), axis=0, keepdims=True)
        cnt = jnp.where(rank1 < ranks_with_v, float(v), cnt)
    for r1 in range(3):
        cnt = jnp.where(rank1 == float(r1), cntr[r1], cnt)
    return rank2, jnp.exp(s2 - b0), cnt, jnp.exp(s1 - a0) / z, bad


def _peer_topk_kernel(s_ref, fidx_ref, r2_out, e2_out, cnt_out, c_out, *, n_chunks):
    key_iota = lax.broadcasted_iota(jnp.int32, (N_KEYS, ROUTE_CHUNK), 0).astype(F32)
    row16 = lax.broadcasted_iota(jnp.int32, (PEER_TOPK, ROUTE_CHUNK), 0)
    fidx = fidx_ref[...]

    def body(it, carry):
        h = it // n_chunks
        cols = pl.ds(pl.multiple_of((it % n_chunks) * ROUTE_CHUNK, ROUTE_CHUNK), ROUTE_CHUNK)
        s1 = s_ref[2 * h, :, cols]
        s2 = s_ref[2 * h + 1, :, cols]

        def store(r2, e2, cnt, c):
            r2_out[h, :, cols] = r2.astype(r2_out.dtype)
            e2_out[h, :, cols] = e2.astype(e2_out.dtype)
            cnt_out[h, :, cols] = cnt.astype(cnt_out.dtype)
            c_out[h, :, cols] = c.astype(c_out.dtype)

        *fast, bad = _route(s1, s2, key_iota, row16, fidx, exact=False)
        store(*fast)

        @pl.when(jnp.max(bad) > 0.0)
        def _():
            *slow, _ = _route(s1, s2, key_iota, row16, fidx, exact=True)
            store(*slow)

        return carry

    lax.fori_loop(0, PEER_HEADS * n_chunks, body, 0)


def _cand_fidx():
    rows = []
    rows += [0 * 16 + r2 for r2 in range(16)]
    for r1 in range(1, 8):
        n = PEER_TOPK // (r1 + 1)
        rows += [r1 * 16 + r2 if r2 < n else _FIDX_INVALID + r1 * 16 + r2 for r2 in range(8)]
    rows += [r1 * 16 for r1 in range(8, 16)]
    assert len(rows) == _CAND_ROWS
    return jnp.broadcast_to(jnp.asarray(rows, F32)[:, None], (_CAND_ROWS, ROUTE_CHUNK))


def _peer_topk(st, tmk):
    T = st.shape[-1]
    tmk = min(tmk, T)
    spec = pl.BlockSpec((PEER_HEADS, N_KEYS, tmk), lambda i: (0, 0, i))
    shp = jax.ShapeDtypeStruct((PEER_HEADS, N_KEYS, T), GDT)
    shp32 = jax.ShapeDtypeStruct((PEER_HEADS, N_KEYS, T), F32)
    return pl.pallas_call(
        functools.partial(_peer_topk_kernel, n_chunks=tmk // ROUTE_CHUNK),
        grid=(T // tmk,),
        in_specs=[pl.BlockSpec((2 * PEER_HEADS, N_KEYS, tmk), lambda i: (0, 0, i)),
                  pl.BlockSpec((_CAND_ROWS, ROUTE_CHUNK), lambda i: (0, 0))],
        out_specs=[spec] * 4,
        out_shape=[shp, shp, shp32, shp32],
        compiler_params=_cparams(("parallel",)),
        name="peer_topk",
    )(st, _cand_fidx())


def _peer_dense_kernel(x_ref, xnt_ref, r2_ref, e2_ref, cnt_ref, c_ref, u_ref, vt_ref, out_ref,
                       acc_ref, ht_ref, at_ref, bc_ref, *, eb, tc, nsub):
    e = pl.program_id(1)
    tm = xnt_ref.shape[1]
    jpb = eb // N_KEYS
    sub = eb // nsub
    pk = bc_ref.shape[-2]

    @pl.when(e == 0)
    def _():
        acc_ref[...] = jnp.zeros_like(acc_ref)

    def fill(h, carry):
        for jb in range(jpb):
            j = e * jpb + jb
            bc_ref[0, h, jb] = jnp.broadcast_to(cnt_ref[h, pl.ds(j, 1), :], (pk, tm)).astype(GDT)
            bc_ref[1, h, jb] = jnp.broadcast_to(c_ref[h, pl.ds(j, 1), :], (pk, tm)).astype(GDT)
        return carry

    lax.fori_loop(0, PEER_HEADS, fill, 0)

    for sb in range(nsub):
        rows = slice(sb * sub, (sb + 1) * sub)
        ht_ref[rows, :] = _dot(u_ref[rows, :], xnt_ref[...])
    for sb in range(nsub):
        rows = slice(sb * sub, (sb + 1) * sub)
        for jb in range(sb * sub // N_KEYS, (sb + 1) * sub // N_KEYS):
            krows = slice(jb * N_KEYS, (jb + 1) * N_KEYS)
            for ci in range(tm // tc):
                cols = slice(ci * tc, (ci + 1) * tc)
                g = jnp.zeros((N_KEYS // pk, pk, tc), GDT)
                for h in range(PEER_HEADS):
                    r2 = r2_ref[h, :, cols].reshape(N_KEYS // pk, pk, tc)
                    e2 = e2_ref[h, :, cols].reshape(N_KEYS // pk, pk, tc)
                    g = g + jnp.where(r2 < bc_ref[0, h, jb, :, cols], e2 * bc_ref[1, h, jb, :, cols], 0)
                a = g.reshape(N_KEYS, tc) * jax.nn.gelu(ht_ref[krows, cols].astype(GDT))
                at_ref[krows, cols] = a.astype(at_ref.dtype)
        acc_ref[...] += _dot(vt_ref[:, rows], at_ref[rows, :])

    @pl.when(e == pl.num_programs(1) - 1)
    def _():
        out_ref[...] = x_ref[...] + acc_ref[...].T


def _peer_dense(x2d, xnt, r2, e2, cnt, c, u_all, vt_all, layer, tm, eb, tc):
    T = x2d.shape[0]
    tm = min(tm, T)
    tc = min(tc, tm)
    tok3 = pl.BlockSpec((PEER_HEADS, N_KEYS, tm), lambda i, e: (0, 0, i))
    return pl.pallas_call(
        functools.partial(_peer_dense_kernel, eb=eb, tc=tc, nsub=eb // PEER_CHAIN),
        grid=(T // tm, N_EXPERTS // eb),
        in_specs=[pl.BlockSpec((tm, D_MODEL), lambda i, e: (i, 0)),
                  pl.BlockSpec((D_MODEL, tm), lambda i, e: (0, i)),
                  tok3, tok3, tok3, tok3,
                  pl.BlockSpec((None, eb, D_MODEL), lambda i, e: (layer, e, 0)),
                  pl.BlockSpec((None, D_MODEL, eb), lambda i, e: (layer, 0, e))],
        out_specs=pl.BlockSpec((tm, D_MODEL), lambda i, e: (i, 0)),
        out_shape=jax.ShapeDtypeStruct((T, D_MODEL), F32),
        scratch_shapes=[pltpu.VMEM((D_MODEL, tm), F32),
                        pltpu.VMEM((eb, tm), F32),
                        pltpu.VMEM((eb, tm), CDT),
                        pltpu.VMEM((2, PEER_HEADS, eb // N_KEYS, GATE_TILE_ROWS, tm), GDT)],
        compiler_params=_cparams(("parallel", "arbitrary"), vmem=PEER_DENSE_VMEM_LIMIT),
        name="peer_dense",
    )(x2d, xnt, r2, e2, cnt, c, u_all, vt_all)


def _pad_heads(w, width):
    r = w.shape[0]
    w = w.reshape(r, MLA_HEADS, width)
    w = jnp.pad(w, ((0, 0), (0, 0), (0, HEAD_PAD - width)))
    return w.reshape(r, MLA_HEADS * HEAD_PAD)


def _layer_weights(l, norm1_g, w_in, q_lora_g, kv_lora_g, w_uq, w_ukv, q_head_g, k_head_g, w_o_mla,
                   pool_w, pool_scale, w_pool_out, gate_bias, w_out, norm2_g, peer_wq, peer_keys):
    wi = w_in[l]
    c0 = Q_LORA
    c1 = c0 + KV_LORA
    c2 = c1 + QK_ROPE
    c3 = c2 + POOL_WIDTH
    zeros = lambda n: jnp.zeros((D_MODEL, n), wi.dtype)
    w_in_pad = jnp.concatenate([wi[:, :c1], zeros(QK_NOPE), wi[:, c1:c2], zeros(LANES - QK_HEAD),
                                wi[:, c2:]], axis=1)
    wkv = w_ukv[l].reshape(KV_LORA, MLA_HEADS, QK_NOPE + V_HEAD)
    w_uk = _pad_heads(wkv[:, :, :QK_NOPE].reshape(KV_LORA, MLA_HEADS * QK_NOPE), QK_NOPE)
    w_uv = wkv[:, :, QK_NOPE:].reshape(KV_LORA, MLA_WIDTH)
    pad_g = lambda g: jnp.pad(g, (0, LANES - QK_HEAD)).reshape(1, LANES)
    return {
        "g1": norm1_g[l].reshape(1, D_MODEL),
        "w_in": w_in_pad.astype(CDT),
        "qlg": q_lora_g[l].reshape(1, Q_LORA),
        "kvlg": kv_lora_g[l].reshape(1, KV_LORA),
        "w_uq": _pad_heads(w_uq[l], QK_HEAD).astype(CDT),
        "w_uk": w_uk.astype(CDT),
        "w_uv": w_uv.astype(CDT),
        "qg": pad_g(q_head_g[l] * (1.0 / math.sqrt(QK_HEAD))),
        "kg": pad_g(k_head_g[l]),
        "gate_bias": gate_bias[l].reshape(1, 2 * D_MODEL),
        "w_o_mla": w_o_mla[l].astype(CDT),
        "pool_w": pool_w[l].astype(CDT),
        "pool_scale": pool_scale[l].reshape(1, POOL_WIDTH),
        "w_pool_out": w_pool_out[l].astype(CDT),
        "w_out": w_out[l].astype(CDT),
        "g2": norm2_g[l].reshape(1, D_MODEL),
        "peer_wq": peer_wq[l].astype(CDT),
        "peer_keys": peer_keys[l].reshape(2 * PEER_HEADS, N_KEYS, PEER_HALF).astype(CDT),
    }


def kernel(x, positions, norm1_g, w_in, q_lora_g, kv_lora_g, w_uq, w_ukv, q_head_g, k_head_g, w_o_mla,
           pool_w, pool_scale, w_pool_out, gate_bias, w_out, norm2_g, peer_wq, peer_keys, peer_u, peer_v):
    B, S, D = x.shape
    assert D == D_MODEL and S % LANES == 0
    T = B * S
    depth = norm1_g.shape[0]
    ropes = _rope_tables(positions)
    x2d = x.reshape(T, D)
    u_all = peer_u.astype(CDT)
    vt_all = jnp.swapaxes(peer_v, 1, 2).astype(CDT)
    for l in range(depth):
        lw = _layer_weights(l, norm1_g, w_in, q_lora_g, kv_lora_g, w_uq, w_ukv, q_head_g, k_head_g,
                            w_o_mla, pool_w, pool_scale, w_pool_out, gate_bias, w_out, norm2_g,
                            peer_wq, peer_keys)
        q, k, v, p, gates = _in_proj(x2d, lw, ropes, tm=IN_PROJ_TM)
        o = _attention(q, k, v, B, S, tq=ATTN_TQ)
        y = _pool(p, lw, B, S)
        x2d, xnt, st = _out_proj(x2d, o, y, gates, lw, tm=OUT_PROJ_TM)
        r2, e2, cnt, c = _peer_topk(st, tmk=ROUTE_TM)
        x2d = _peer_dense(x2d, xnt, r2, e2, cnt, c, u_all, vt_all, l, tm=PEER_TM, eb=PEER_EB, tc=PEER_TC)
    return x2d.reshape(B, S, D)
```

```python
import functools
import math

import jax
import jax.numpy as jnp
import numpy as np
from jax import lax
from jax.experimental import pallas as pl
from jax.experimental.pallas import tpu as pltpu

D_MODEL = 1024
MLA_HEADS = 8
Q_LORA = 384
KV_LORA = 256
QK_NOPE = 64
QK_ROPE = 32
QK_HEAD = QK_NOPE + QK_ROPE
V_HEAD = 64
MLA_WIDTH = MLA_HEADS * V_HEAD
ROPE_BASE = 10000.0
POOL_WINDOWS = (2, 4, 8, 16)
POOL_WIDTH = 512
POOL_GC = 128
PEER_HEADS = 8
N_KEYS = 128
N_EXPERTS = N_KEYS * N_KEYS
PEER_HALF = 128
PEER_TOPK = 16
RMS_EPS = 1e-6

LANES = 128
GATE_TILE_ROWS = 16
ROW_SUBTILE = 256
ROUTE_CHUNK = 512
HEAD_PAD = LANES
IN_PAD = Q_LORA + KV_LORA + LANES + POOL_WIDTH + 2 * D_MODEL
VMEM_LIMIT = 48 * 1024 * 1024
PEER_DENSE_VMEM_LIMIT = 56 * 1024 * 1024

IN_PROJ_TM = 1024
ATTN_TQ = 256
OUT_PROJ_TM = 512
ROUTE_TM = 1024
PEER_TM = 512
PEER_EB = 2048
PEER_CHAIN = 512
PEER_TC = 256

CDT = jnp.bfloat16
GDT = jnp.bfloat16
F32 = jnp.float32
NEG_INF = float("-inf")
_FLT_MAX = float(np.finfo(np.float32).max)
_FLT_TOP_ULP = 2.0 ** 104

_CAND_ROWS = 16 + 8 * 7 + 8
_FIDX_INVALID = 1 << 20


def _cparams(sem, vmem=VMEM_LIMIT):
    return pltpu.CompilerParams(dimension_semantics=sem, vmem_limit_bytes=vmem)


def _rms(x, g):
    return x * lax.rsqrt(jnp.mean(x * x, axis=-1, keepdims=True) + RMS_EPS) * g


def _dot(a, b):
    return jnp.dot(a, b, preferred_element_type=F32)


def _dot_nt(a, b):
    return lax.dot_general(a, b, (((1,), (1,)), ((), ())), preferred_element_type=F32)


def _rope_kernel(pos_ref, inv_ref, c_ref, s1_ref, s2_ref):
    ang = pos_ref[...] * inv_ref[...]
    lane = lax.broadcasted_iota(jnp.int32, ang.shape, 1)
    cos = jnp.cos(ang)
    sin = jnp.sin(ang)
    half = QK_ROPE // 2
    c_ref[...] = jnp.where(lane < QK_NOPE, 1.0, jnp.where(lane < QK_HEAD, cos, 0.0))
    s1_ref[...] = jnp.where((lane >= QK_NOPE) & (lane < QK_NOPE + half), -sin, 0.0)
    s2_ref[...] = jnp.where((lane >= QK_NOPE + half) & (lane < QK_HEAD), sin, 0.0)


def _rope_tables(positions):
    T = positions.size
    tm = min(T, 2048)
    pos = jnp.broadcast_to(positions.reshape(T, 1).astype(F32), (T, LANES))
    inv = ROPE_BASE ** (-jnp.arange(0, QK_ROPE, 2, dtype=F32) / QK_ROPE)
    inv_pat = jnp.concatenate([jnp.zeros((QK_NOPE,), F32), inv, inv,
                               jnp.zeros((LANES - QK_HEAD,), F32)]).reshape(1, LANES)
    spec = pl.BlockSpec((tm, LANES), lambda i: (i, 0))
    return pl.pallas_call(
        _rope_kernel,
        grid=(T // tm,),
        in_specs=[spec, pl.BlockSpec((1, LANES), lambda i: (0, 0))],
        out_specs=[spec, spec, spec],
        out_shape=[jax.ShapeDtypeStruct((T, LANES), F32)] * 3,
        compiler_params=_cparams(("parallel",)),
        name="rope_tables",
    )(pos, inv_pat)


def _in_proj_kernel(x_ref, g1_ref, win_ref, qlg_ref, kvlg_ref, wuq_ref, wuk_ref, wuv_ref,
                    c_ref, s1_ref, s2_ref, qg_ref, kg_ref, bias_ref,
                    q_out, k_out, v_out, p_out, gate_out, *, sub):
    half = QK_ROPE // 2
    qg = qg_ref[...]
    kg = kg_ref[...]

    def head_norm(xh, g):
        ms = jnp.sum(xh * xh, axis=-1, keepdims=True) * (1.0 / QK_HEAD)
        return xh * lax.rsqrt(ms + RMS_EPS) * g

    for r0 in range(0, x_ref.shape[0], sub):
        rows = slice(r0, r0 + sub)
        xn = _rms(x_ref[rows, :], g1_ref[...])
        y = _dot(xn.astype(CDT), win_ref[...])
        o = 0
        cq = y[:, o:o + Q_LORA]; o += Q_LORA
        ckv = y[:, o:o + KV_LORA]; o += KV_LORA
        kpe = y[:, o:o + LANES]; o += LANES
        p_out[rows, :] = y[:, o:o + POOL_WIDTH]; o += POOL_WIDTH
        gate_out[rows, :] = jax.nn.sigmoid(y[:, o:] + bias_ref[...]).astype(gate_out.dtype)

        cqn = _rms(cq, qlg_ref[...]).astype(CDT)
        ckvn = _rms(ckv, kvlg_ref[...]).astype(CDT)
        q_raw = _dot(cqn, wuq_ref[...])
        k_raw = _dot(ckvn, wuk_ref[...])
        v_out[rows, :] = _dot(ckvn, wuv_ref[...]).astype(v_out.dtype)

        cpat = c_ref[rows, :]
        s1pat = s1_ref[rows, :]
        s2pat = s2_ref[rows, :]

        def rope(xh, cpat=cpat, s1pat=s1pat, s2pat=s2pat):
            return (xh * cpat + pltpu.roll(xh, LANES - half, 1) * s1pat
                    + pltpu.roll(xh, half, 1) * s2pat)

        kpe_r = rope(kpe)
        for h in range(MLA_HEADS):
            sl = slice(h * HEAD_PAD, (h + 1) * HEAD_PAD)
            q_out[rows, sl] = head_norm(rope(q_raw[:, sl]), qg).astype(q_out.dtype)
            k_out[rows, sl] = head_norm(k_raw[:, sl] + kpe_r, kg).astype(k_out.dtype)


def _in_proj(x2d, lw, ropes, tm):
    T = x2d.shape[0]
    tm = min(tm, T)
    row = lambda w: pl.BlockSpec((tm, w), lambda i: (i, 0))
    full = lambda a: pl.BlockSpec(a.shape, lambda i: (0,) * a.ndim)
    consts = [lw["g1"], lw["w_in"], lw["qlg"], lw["kvlg"], lw["w_uq"], lw["w_uk"], lw["w_uv"]]
    tail = [lw["qg"], lw["kg"], lw["gate_bias"]]
    return pl.pallas_call(
        functools.partial(_in_proj_kernel, sub=min(tm, ROW_SUBTILE)),
        grid=(T // tm,),
        in_specs=[row(D_MODEL)] + [full(a) for a in consts] + [row(LANES)] * 3 + [full(a) for a in tail],
        out_specs=[row(MLA_HEADS * HEAD_PAD), row(MLA_HEADS * HEAD_PAD), row(MLA_WIDTH),
                   row(POOL_WIDTH), row(2 * D_MODEL)],
        out_shape=[jax.ShapeDtypeStruct((T, MLA_HEADS * HEAD_PAD), CDT),
                   jax.ShapeDtypeStruct((T, MLA_HEADS * HEAD_PAD), CDT),
                   jax.ShapeDtypeStruct((T, MLA_WIDTH), CDT),
                   jax.ShapeDtypeStruct((T, POOL_WIDTH), F32),
                   jax.ShapeDtypeStruct((T, 2 * D_MODEL), CDT)],
        compiler_params=_cparams(("parallel",)),
        name="in_proj",
    )(x2d, *consts, *ropes, *tail)


def _attn_kernel(q_ref, k_ref, v_ref, o_ref):
    outs = []
    for j in range(MLA_HEADS):
        q = q_ref[:, j * HEAD_PAD:(j + 1) * HEAD_PAD]
        k = k_ref[:, j * HEAD_PAD:(j + 1) * HEAD_PAD]
        v = v_ref[:, j * V_HEAD:(j + 1) * V_HEAD]
        s = _dot_nt(q, k)
        m = jnp.max(s, axis=-1, keepdims=True)
        p = jnp.exp2(s - m)
        l = jnp.sum(p, axis=-1, keepdims=True)
        outs.append(_dot(p.astype(CDT), v) / l)
    o_ref[...] = jnp.concatenate(outs, axis=-1).astype(o_ref.dtype)


def _attention(q, k, v, B, S, tq):
    tq = min(tq, S)
    nq = S // tq
    return pl.pallas_call(
        _attn_kernel,
        grid=(B, nq),
        in_specs=[pl.BlockSpec((tq, MLA_HEADS * HEAD_PAD), lambda b, i: (b * nq + i, 0)),
                  pl.BlockSpec((S, MLA_HEADS * HEAD_PAD), lambda b, i: (b, 0)),
                  pl.BlockSpec((S, MLA_WIDTH), lambda b, i: (b, 0))],
        out_specs=pl.BlockSpec((tq, MLA_WIDTH), lambda b, i: (b * nq + i, 0)),
        out_shape=jax.ShapeDtypeStruct((B * S, MLA_WIDTH), CDT),
        compiler_params=_cparams(("parallel", "parallel")),
        name="attention",
    )(q, k, v)


def _pool_kernel(p_ref, pw_ref, ps_ref, y_ref):
    S = p_ref.shape[0]
    t = lax.broadcasted_iota(jnp.int32, (S, POOL_GC), 0)
    for g, w in enumerate(POOL_WINDOWS):
        half = w // 2
        sl = slice(g * POOL_GC, (g + 1) * POOL_GC)
        pg = p_ref[:, sl]
        acc = pg
        for d in range(-half, half):
            if d == 0:
                continue
            shifted = pltpu.roll(pg, (-d) % S, 0)
            valid = (t + d >= 0) & (t + d < S)
            acc = acc + jnp.where(valid, shifted, 0.0)
        count = (jnp.minimum(t + half, S) - jnp.maximum(t - half, 0)).astype(F32)
        mixed = acc / count - pg
        yg = _dot(mixed.astype(CDT), pw_ref[g]) * ps_ref[:, sl]
        y_ref[:, sl] = yg.astype(y_ref.dtype)


def _pool(p, lw, B, S):
    return pl.pallas_call(
        _pool_kernel,
        grid=(B,),
        in_specs=[pl.BlockSpec((S, POOL_WIDTH), lambda b: (b, 0)),
                  pl.BlockSpec(lw["pool_w"].shape, lambda b: (0, 0, 0)),
                  pl.BlockSpec((1, POOL_WIDTH), lambda b: (0, 0))],
        out_specs=pl.BlockSpec((S, POOL_WIDTH), lambda b: (b, 0)),
        out_shape=jax.ShapeDtypeStruct((B * S, POOL_WIDTH), CDT),
        compiler_params=_cparams(("parallel",)),
        name="pool",
    )(p, lw["pool_w"], lw["pool_scale"])


def _out_proj_kernel(x_ref, o_ref, y_ref, gate_ref, wo_ref, wpo_ref, wout_ref, g2_ref, wq_ref, keys_ref,
                     xnew_out, xnt_out, st_out, *, sub):
    for r0 in range(0, x_ref.shape[0], sub):
        rows = slice(r0, r0 + sub)
        a = _dot(o_ref[rows, :], wo_ref[...])
        bp = _dot(y_ref[rows, :], wpo_ref[...])
        ga = gate_ref[rows, :D_MODEL].astype(F32)
        gb = gate_ref[rows, D_MODEL:].astype(F32)
        mix = ga * a + gb * bp
        xnew = x_ref[rows, :] + _dot(mix.astype(CDT), wout_ref[...])
        xnew_out[rows, :] = xnew
        xn = _rms(xnew, g2_ref[...])
        xnt_out[:, rows] = xn.T.astype(xnt_out.dtype)
        qp = _dot(xn.astype(CDT), wq_ref[...])
        for hp in range(2 * PEER_HEADS):
            qh = qp[:, hp * PEER_HALF:(hp + 1) * PEER_HALF].astype(CDT)
            st_out[hp, :, rows] = _dot_nt(keys_ref[hp], qh)


def _out_proj(x2d, o, y, gates, lw, tm):
    T = x2d.shape[0]
    tm = min(tm, T)
    row = lambda w: pl.BlockSpec((tm, w), lambda i: (i, 0))
    full = lambda a: pl.BlockSpec(a.shape, lambda i: (0,) * a.ndim)
    consts = [lw["w_o_mla"], lw["w_pool_out"], lw["w_out"], lw["g2"], lw["peer_wq"], lw["peer_keys"]]
    return pl.pallas_call(
        functools.partial(_out_proj_kernel, sub=min(tm, ROW_SUBTILE)),
        grid=(T // tm,),
        in_specs=[row(D_MODEL), row(MLA_WIDTH), row(POOL_WIDTH), row(2 * D_MODEL)] + [full(a) for a in consts],
        out_specs=[row(D_MODEL),
                   pl.BlockSpec((D_MODEL, tm), lambda i: (0, i)),
                   pl.BlockSpec((2 * PEER_HEADS, N_KEYS, tm), lambda i: (0, 0, i))],
        out_shape=[jax.ShapeDtypeStruct((T, D_MODEL), F32),
                   jax.ShapeDtypeStruct((D_MODEL, T), CDT),
                   jax.ShapeDtypeStruct((2 * PEER_HEADS, N_KEYS, T), F32)],
        compiler_params=_cparams(("parallel",)),
        name="out_proj",
    )(x2d, o, y, gates, *consts)


def _top16(s, key_iota, row16, exact):
    work = s
    rank = jnp.full(s.shape, float(PEER_TOPK), F32)
    tops = jnp.zeros((PEER_TOPK, s.shape[1]), F32)
    for r in range(PEER_TOPK):
        m = jnp.max(work, axis=0, keepdims=True)
        sel = work == m
        if exact:
            first = jnp.min(jnp.where(sel, key_iota, float(N_KEYS)), axis=0, keepdims=True)
            sel = key_iota == first
            rank = jnp.where(sel, float(r), rank)
            work = jnp.where(sel, NEG_INF, work)
        else:
            work = jnp.where(sel, -_FLT_MAX + r * _FLT_TOP_ULP, work)
        tops = jnp.where(row16 == r, m, tops)
    if not exact:
        rank = jnp.minimum((work + _FLT_MAX) * (1.0 / _FLT_TOP_ULP), float(PEER_TOPK))
    return rank, tops


def _cand_grid(ta, tb, combine):
    pieces = [combine(ta[0:1], tb)]
    for r1 in range(1, 8):
        pieces.append(combine(ta[r1:r1 + 1], tb[0:8]))
    pieces.append(combine(ta[8:16], tb[0:1]))
    return jnp.concatenate(pieces, axis=0)


def _count_true(mask):
    return jnp.sum(jnp.where(mask, 1.0, 0.0), axis=0, keepdims=True)


def _route(s1, s2, key_iota, row16, fidx, exact):
    rank1, ta = _top16(s1, key_iota, row16, exact)
    rank2, tb = _top16(s2, key_iota, row16, exact)
    a0 = ta[0:1]
    b0 = tb[0:1]
    cand0 = jnp.where(fidx < _FIDX_INVALID, _cand_grid(ta, tb, lambda a, b: a + b), NEG_INF)
    ecand = _cand_grid(jnp.exp(ta - a0), jnp.exp(tb - b0), lambda a, b: a * b)
    cand = cand0
    if exact:
        selected = jnp.zeros(cand.shape, F32)
        for _ in range(PEER_TOPK):
            m = jnp.max(cand, axis=0, keepdims=True)
            first = jnp.min(jnp.where(cand == m, fidx, float(_FIDX_INVALID)), axis=0, keepdims=True)
            sel = fidx == first
            selected = jnp.where(sel, 1.0, selected)
            cand = jnp.where(sel, NEG_INF, cand)
        bad = jnp.zeros_like(a0)
    else:
        for _ in range(PEER_TOPK):
            m = jnp.max(cand, axis=0, keepdims=True)
            cand = jnp.where(cand == m, NEG_INF, cand)
        selected = jnp.where(cand0 >= m, 1.0, 0.0)
        k = float(PEER_TOPK)
        bad = jnp.where((_count_true(rank1 < k) != k) | (_count_true(rank2 < k) != k)
                        | (jnp.sum(selected, axis=0, keepdims=True) != k), 1.0, 0.0)
    z = jnp.sum(selected * ecand, axis=0, keepdims=True)
    cntr = [jnp.sum(selected[0:16], axis=0, keepdims=True)]
    for r1 in range(1, 8):
        cntr.append(jnp.sum(selected[8 + 8 * r1:16 + 8 * r1], axis=0, keepdims=True))
    for r1 in range(8, 16):
        cntr.append(selected[64 + r1:65 + r1])
    cntr_rows = jnp.concatenate(cntr[0:8] + [selected[72:80]], axis=0)
    cnt = jnp.zeros(s1.shape, F32)
    for v in range(1, 5):
        ranks_with_v = jnp.sum(jnp.where(cntr_rows >= float(v), 1.0, 0.0), axis=0, keepdims=True)
        cnt = jnp.where(rank1 < ranks_with_v, float(v), cnt)
    for r1 in range(3):
        cnt = jnp.where(rank1 == float(r1), cntr[r1], cnt)
    return rank2, jnp.exp(s2 - b0), cnt, jnp.exp(s1 - a0) / z, bad


def _peer_topk_kernel(s_ref, fidx_ref, r2_out, e2_out, cnt_out, c_out, *, n_chunks):
    key_iota = lax.broadcasted_iota(jnp.int32, (N_KEYS, ROUTE_CHUNK), 0).astype(F32)
    row16 = lax.broadcasted_iota(jnp.int32, (PEER_TOPK, ROUTE_CHUNK), 0)
    fidx = fidx_ref[...]

    def body(it, carry):
        h = it // n_chunks
        cols = pl.ds(pl.multiple_of((it % n_chunks) * ROUTE_CHUNK, ROUTE_CHUNK), ROUTE_CHUNK)
        s1 = s_ref[2 * h, :, cols]
        s2 = s_ref[2 * h + 1, :, cols]

        def store(r2, e2, cnt, c):
            r2_out[h, :, cols] = r2.astype(r2_out.dtype)
            e2_out[h, :, cols] = e2.astype(e2_out.dtype)
            cnt_out[h, :, cols] = cnt.astype(cnt_out.dtype)
            c_out[h, :, cols] = c.astype(c_out.dtype)

        *fast, bad = _route(s1, s2, key_iota, row16, fidx, exact=False)
        store(*fast)

        @pl.when(jnp.max(bad) > 0.0)
        def _():
            *slow, _ = _route(s1, s2, key_iota, row16, fidx, exact=True)
            store(*slow)

        return carry

    lax.fori_loop(0, PEER_HEADS * n_chunks, body, 0)


def _cand_fidx():
    rows = []
    rows += [0 * 16 + r2 for r2 in range(16)]
    for r1 in range(1, 8):
        n = PEER_TOPK // (r1 + 1)
        rows += [r1 * 16 + r2 if r2 < n else _FIDX_INVALID + r1 * 16 + r2 for r2 in range(8)]
    rows += [r1 * 16 for r1 in range(8, 16)]
    assert len(rows) == _CAND_ROWS
    return jnp.broadcast_to(jnp.asarray(rows, F32)[:, None], (_CAND_ROWS, ROUTE_CHUNK))


def _peer_topk(st, tmk):
    T = st.shape[-1]
    tmk = min(tmk, T)
    spec = pl.BlockSpec((PEER_HEADS, N_KEYS, tmk), lambda i: (0, 0, i))
    shp = jax.ShapeDtypeStruct((PEER_HEADS, N_KEYS, T), GDT)
    shp32 = jax.ShapeDtypeStruct((PEER_HEADS, N_KEYS, T), F32)
    return pl.pallas_call(
        functools.partial(_peer_topk_kernel, n_chunks=tmk // ROUTE_CHUNK),
        grid=(T // tmk,),
        in_specs=[pl.BlockSpec((2 * PEER_HEADS, N_KEYS, tmk), lambda i: (0, 0, i)),
                  pl.BlockSpec((_CAND_ROWS, ROUTE_CHUNK), lambda i: (0, 0))],
        out_specs=[spec] * 4,
        out_shape=[shp, shp, shp32, shp32],
        compiler_params=_cparams(("parallel",)),
        name="peer_topk",
    )(st, _cand_fidx())


def _peer_dense_kernel(x_ref, xnt_ref, r2_ref, e2_ref, cnt_ref, c_ref, u_ref, vt_ref, out_ref,
                       acc_ref, ht_ref, at_ref, bc_ref, *, eb, tc, nsub):
    e = pl.program_id(1)
    tm = xnt_ref.shape[1]
    jpb = eb // N_KEYS
    sub = eb // nsub
    pk = bc_ref.shape[-2]

    @pl.when(e == 0)
    def _():
        acc_ref[...] = jnp.zeros_like(acc_ref)

    def fill(h, carry):
        for jb in range(jpb):
            j = e * jpb + jb
            bc_ref[0, h, jb] = jnp.broadcast_to(cnt_ref[h, pl.ds(j, 1), :], (pk, tm)).astype(GDT)
            bc_ref[1, h, jb] = jnp.broadcast_to(c_ref[h, pl.ds(j, 1), :], (pk, tm)).astype(GDT)
        return carry

    lax.fori_loop(0, PEER_HEADS, fill, 0)

    for sb in range(nsub):
        rows = slice(sb * sub, (sb + 1) * sub)
        ht_ref[rows, :] = _dot(u_ref[rows, :], xnt_ref[...])
    for sb in range(nsub):
        rows = slice(sb * sub, (sb + 1) * sub)
        for jb in range(sb * sub // N_KEYS, (sb + 1) * sub // N_KEYS):
            krows = slice(jb * N_KEYS, (jb + 1) * N_KEYS)
            for ci in range(tm // tc):
                cols = slice(ci * tc, (ci + 1) * tc)
                g = jnp.zeros((N_KEYS // pk, pk, tc), GDT)
                for h in range(PEER_HEADS):
                    r2 = r2_ref[h, :, cols].reshape(N_KEYS // pk, pk, tc)
                    e2 = e2_ref[h, :, cols].reshape(N_KEYS // pk, pk, tc)
                    g = g + jnp.where(r2 < bc_ref[0, h, jb, :, cols], e2 * bc_ref[1, h, jb, :, cols], 0)
                a = g.reshape(N_KEYS, tc) * jax.nn.gelu(ht_ref[krows, cols].astype(GDT))
                at_ref[krows, cols] = a.astype(at_ref.dtype)
        acc_ref[...] += _dot(vt_ref[:, rows], at_ref[rows, :])

    @pl.when(e == pl.num_programs(1) - 1)
    def _():
        out_ref[...] = x_ref[...] + acc_ref[...].T


def _peer_dense(x2d, xnt, r2, e2, cnt, c, u_all, vt_all, layer, tm, eb, tc):
    T = x2d.shape[0]
    tm = min(tm, T)
    tc = min(tc, tm)
    tok3 = pl.BlockSpec((PEER_HEADS, N_KEYS, tm), lambda i, e: (0, 0, i))
    return pl.pallas_call(
        functools.partial(_peer_dense_kernel, eb=eb, tc=tc, nsub=eb // PEER_CHAIN),
        grid=(T // tm, N_EXPERTS // eb),
        in_specs=[pl.BlockSpec((tm, D_MODEL), lambda i, e: (i, 0)),
                  pl.BlockSpec((D_MODEL, tm), lambda i, e: (0, i)),
                  tok3, tok3, tok3, tok3,
                  pl.BlockSpec((None, eb, D_MODEL), lambda i, e: (layer, e, 0)),
                  pl.BlockSpec((None, D_MODEL, eb), lambda i, e: (layer, 0, e))],
        out_specs=pl.BlockSpec((tm, D_MODEL), lambda i, e: (i, 0)),
        out_shape=jax.ShapeDtypeStruct((T, D_MODEL), F32),
        scratch_shapes=[pltpu.VMEM((D_MODEL, tm), F32),
                        pltpu.VMEM((eb, tm), F32),
                        pltpu.VMEM((eb, tm), CDT),
                        pltpu.VMEM((2, PEER_HEADS, eb // N_KEYS, GATE_TILE_ROWS, tm), GDT)],
        compiler_params=_cparams(("parallel", "arbitrary"), vmem=PEER_DENSE_VMEM_LIMIT),
        name="peer_dense",
    )(x2d, xnt, r2, e2, cnt, c, u_all, vt_all)


def _pad_heads(w, width):
    r = w.shape[0]
    w = w.reshape(r, MLA_HEADS, width)
    w = jnp.pad(w, ((0, 0), (0, 0), (0, HEAD_PAD - width)))
    return w.reshape(r, MLA_HEADS * HEAD_PAD)


def _layer_weights(l, norm1_g, w_in, q_lora_g, kv_lora_g, w_uq, w_ukv, q_head_g, k_head_g, w_o_mla,
                   pool_w, pool_scale, w_pool_out, gate_bias, w_out, norm2_g, peer_wq, peer_keys):
    wi = w_in[l]
    c0 = Q_LORA
    c1 = c0 + KV_LORA
    c2 = c1 + QK_ROPE
    c3 = c2 + POOL_WIDTH
    zeros = lambda n: jnp.zeros((D_MODEL, n), wi.dtype)
    w_in_pad = jnp.concatenate([wi[:, :c1], zeros(QK_NOPE), wi[:, c1:c2], zeros(LANES - QK_HEAD),
                                wi[:, c2:]], axis=1)
    wkv = w_ukv[l].reshape(KV_LORA, MLA_HEADS, QK_NOPE + V_HEAD)
    w_uk = _pad_heads(wkv[:, :, :QK_NOPE].reshape(KV_LORA, MLA_HEADS * QK_NOPE), QK_NOPE)
    w_uv = wkv[:, :, QK_NOPE:].reshape(KV_LORA, MLA_WIDTH)
    pad_g = lambda g: jnp.pad(g, (0, LANES - QK_HEAD)).reshape(1, LANES)
    return {
        "g1": norm1_g[l].reshape(1, D_MODEL),
        "w_in": w_in_pad.astype(CDT),
        "qlg": q_lora_g[l].reshape(1, Q_LORA),
        "kvlg": kv_lora_g[l].reshape(1, KV_LORA),
        "w_uq": _pad_heads(w_uq[l], QK_HEAD).astype(CDT),
        "w_uk": w_uk.astype(CDT),
        "w_uv": w_uv.astype(CDT),
        "qg": pad_g(q_head_g[l] * (math.log2(math.e) / math.sqrt(QK_HEAD))),
        "kg": pad_g(k_head_g[l]),
        "gate_bias": gate_bias[l].reshape(1, 2 * D_MODEL),
        "w_o_mla": w_o_mla[l].astype(CDT),
        "pool_w": pool_w[l].astype(CDT),
        "pool_scale": pool_scale[l].reshape(1, POOL_WIDTH),
        "w_pool_out": w_pool_out[l].astype(CDT),
        "w_out": w_out[l].astype(CDT),
        "g2": norm2_g[l].reshape(1, D_MODEL),
        "peer_wq": peer_wq[l].astype(CDT),
        "peer_keys": peer_keys[l].reshape(2 * PEER_HEADS, N_KEYS, PEER_HALF).astype(CDT),
    }


def kernel(x, positions, norm1_g, w_in, q_lora_g, kv_lora_g, w_uq, w_ukv, q_head_g, k_head_g, w_o_mla,
           pool_w, pool_scale, w_pool_out, gate_bias, w_out, norm2_g, peer_wq, peer_keys, peer_u, peer_v):
    B, S, D = x.shape
    assert D == D_MODEL and S % LANES == 0
    T = B * S
    depth = norm1_g.shape[0]
    ropes = _rope_tables(positions)
    x2d = x.reshape(T, D)
    u_all = peer_u.astype(CDT)
    vt_all = jnp.swapaxes(peer_v, 1, 2).astype(CDT)
    for l in range(depth):
        lw = _layer_weights(l, norm1_g, w_in, q_lora_g, kv_lora_g, w_uq, w_ukv, q_head_g, k_head_g,
                            w_o_mla, pool_w, pool_scale, w_pool_out, gate_bias, w_out, norm2_g,
                            peer_wq, peer_keys)
        q, k, v, p, gates = _in_proj(x2d, lw, ropes, tm=IN_PROJ_TM)
        o = _attention(q, k, v, B, S, tq=ATTN_TQ)
        y = _pool(p, lw, B, S)
        x2d, xnt, st = _out_proj(x2d, o, y, gates, lw, tm=OUT_PROJ_TM)
        r2, e2, cnt, c = _peer_topk(st, tmk=ROUTE_TM)
        x2d = _peer_dense(x2d, xnt, r2, e2, cnt, c, u_all, vt_all, l, tm=PEER_TM, eb=PEER_EB, tc=PEER_TC)
    return x2d.reshape(B, S, D)
```
